```python
import math
import jax, jax.numpy as jnp
from jax import lax
import numpy as np

D_MODEL = 1024
BATCH = 8
SEQ = 4096
DEPTH = 2

BLOCK = 128
FOX_HEADS = 8
FOX_HD = 64
SWA_HEADS = 8
SWA_KV = 2
SWA_HD = 64
WINDOW = 128
POOL_WINDOWS = (2, 4, 8, 16)
POOL_GROUPS = 4
POOL_GW = 128
POOL_W = POOL_GROUPS * POOL_GW
CONV_W = 512
CONV_K = 31
N_BRANCH = 4
PEER_HEADS = 8
PEER_DQ = 256
N_KEYS = 128
N_EXPERTS = N_KEYS * N_KEYS
PEER_TOPK = 16
PEER_CHUNK = 128
LN_EPS = 1e-5
ALPHA = (2 * DEPTH) ** 0.25
BETA = (8 * DEPTH) ** -0.25
IN_SIZES = (FOX_HEADS * FOX_HD, FOX_HEADS * FOX_HD, FOX_HEADS * FOX_HD, FOX_HEADS,
            SWA_HEADS * SWA_HD, SWA_KV * SWA_HD, SWA_KV * SWA_HD,
            POOL_W, 2 * CONV_W, N_BRANCH * D_MODEL)
D_IN = 3 * FOX_HEADS * FOX_HD + FOX_HEADS + SWA_HEADS * SWA_HD + 2 * SWA_KV * SWA_HD + POOL_W + 2 * CONV_W + N_BRANCH * D_MODEL

kernel_name = "fox_pool_conv_swa_peer_deepnorm_hybrid"


def layer_norm(x, g, b):
    xf = x.astype(jnp.float32)
    mu = xf.mean(-1, keepdims=True)
    var = jnp.mean(jnp.square(xf - mu), -1, keepdims=True)
    return ((xf - mu) * lax.rsqrt(var + LN_EPS) * g + b).astype(x.dtype)


def alibi_slopes(n):
    return 2.0 ** (-8.0 * jnp.arange(1, n + 1, dtype=jnp.float32) / n)


def forgetting_attention(q, k, v, f_logit):
    B, S, H, Dh = q.shape
    nb = S // BLOCK
    c = jnp.cumsum(jax.nn.log_sigmoid(f_logit.astype(jnp.float32)), axis=1)
    ck = c.transpose(0, 2, 1)
    qb = q.reshape(B, nb, BLOCK, H, Dh).swapaxes(0, 1)
    cb = c.reshape(B, nb, BLOCK, H).swapaxes(0, 1)
    starts = jnp.arange(nb) * BLOCK
    k_pos = jnp.arange(S)
    scale = Dh ** -0.5

    def one_block(args):
        q_blk, c_blk, start = args
        s = jnp.einsum('bqhd,bkhd->bhqk', q_blk, k, preferred_element_type=jnp.float32) * scale
        s = s + c_blk.transpose(0, 2, 1)[..., None] - ck[:, :, None, :]
        q_pos = start + jnp.arange(BLOCK)
        s = jnp.where(k_pos[None, :] <= q_pos[:, None], s, -jnp.inf)
        p = jax.nn.softmax(s, axis=-1)
        return jnp.einsum('bhqk,bkhd->bqhd', p.astype(v.dtype), v)

    out = lax.map(one_block, (qb, cb, starts))
    return out.swapaxes(0, 1).reshape(B, S, H * Dh)


def sliding_window_gqa(q, k, v, sinks):
    B, S, Hq, Dh = q.shape
    G = k.shape[2]
    R = Hq // G
    nb = S // BLOCK
    qb = q.reshape(B, nb, BLOCK, G, R, Dh)

    def band(t):
        tb = t.reshape(B, nb, BLOCK, G, Dh)
        prev = jnp.concatenate([jnp.zeros_like(tb[:, :1]), tb[:, :-1]], axis=1)
        return jnp.concatenate([prev, tb], axis=2)

    kb, vb = band(k), band(v)
    s = jnp.einsum('bnqgrd,bnkgd->bngrqk', qb, kb, preferred_element_type=jnp.float32) * (Dh ** -0.5)
    qi = jnp.arange(BLOCK)[:, None]
    kj = jnp.arange(2 * BLOCK)[None, :]
    dist = qi + BLOCK - kj
    key_pos = jnp.arange(nb)[:, None, None] * BLOCK - BLOCK + kj[None]
    valid = (dist >= 0)[None] & (dist < WINDOW)[None] & (key_pos >= 0)
    slopes = alibi_slopes(Hq).reshape(G, R)
    s = s - slopes[None, None, :, :, None, None] * dist.astype(jnp.float32)
    s = jnp.where(valid[None, :, None, None], s, -jnp.inf)
    sink = sinks.astype(jnp.float32).reshape(G, R)[None, None, :, :, None, None]
    m = jnp.maximum(s.max(-1, keepdims=True), sink)
    e = jnp.exp(s - m)
    p = e / (e.sum(-1, keepdims=True) + jnp.exp(sink - m))
    o = jnp.einsum('bngrqk,bnkgd->bnqgrd', p.astype(v.dtype), vb)
    return o.reshape(B, S, Hq * Dh)


def multiscale_pool(x, w_grp, scale):
    B, S, C = x.shape
    xf = x.astype(jnp.float32)
    maxw = max(POOL_WINDOWS)
    csp = jnp.concatenate([jnp.zeros((B, maxw, C), jnp.float32), jnp.cumsum(xf, axis=1)], axis=1)
    outs = []
    for g, w in enumerate(POOL_WINDOWS):
        lo, hi = g * POOL_GW, (g + 1) * POOL_GW
        win_sum = csp[:, maxw:, lo:hi] - csp[:, maxw - w:maxw - w + S, lo:hi]
        cnt = jnp.minimum(jnp.arange(1, S + 1), w).astype(jnp.float32)[None, :, None]
        outs.append(win_sum / cnt - xf[:, :, lo:hi])
    p = jnp.stack(outs, axis=2).astype(x.dtype)
    y = jnp.einsum('bsgc,gcd->bsgd', p, w_grp).reshape(B, S, C)
    return y * scale


def conformer_conv(u, w_dw, b_dw, ln_g, ln_b):
    a, gt = jnp.split(u, 2, axis=-1)
    h = a * jax.nn.sigmoid(gt)
    hp = jnp.pad(h, ((0, 0), (CONV_K - 1, 0), (0, 0)))
    h = lax.conv_general_dilated(hp, w_dw[:, None, :].astype(h.dtype), window_strides=(1,), padding='VALID',
                                 dimension_numbers=('NWC', 'WIO', 'NWC'), feature_group_count=CONV_W) + b_dw
    return jax.nn.silu(layer_norm(h, ln_g, ln_b))


def hybrid_mixer(x, w_in, b_f, sinks, pool_w, pool_scale, dw_w, dw_b, cln_g, cln_b,
                 p_fox, p_swa, p_pool, p_conv, w_out):
    B, S, D = x.shape
    h = jnp.einsum('bsd,de->bse', x, w_in)
    pts, acc = [], 0
    for n in IN_SIZES[:-1]:
        acc += n
        pts.append(acc)
    q_f, k_f, v_f, f_l, q_s, k_s, v_s, x_pool, x_conv, gl = jnp.split(h, pts, axis=-1)
    y_fox = forgetting_attention(q_f.reshape(B, S, FOX_HEADS, FOX_HD), k_f.reshape(B, S, FOX_HEADS, FOX_HD),
                                 v_f.reshape(B, S, FOX_HEADS, FOX_HD), f_l + b_f)
    y_swa = sliding_window_gqa(q_s.reshape(B, S, SWA_HEADS, SWA_HD), k_s.reshape(B, S, SWA_KV, SWA_HD),
                               v_s.reshape(B, S, SWA_KV, SWA_HD), sinks)
    y_pool = multiscale_pool(x_pool, pool_w, pool_scale)
    y_conv = conformer_conv(x_conv, dw_w, dw_b, cln_g, cln_b)
    gates = jax.nn.sigmoid(gl.reshape(B, S, N_BRANCH, D))
    merged = (gates[:, :, 0] * (y_fox @ p_fox) + gates[:, :, 1] * (y_pool @ p_pool)
              + gates[:, :, 2] * (y_conv @ p_conv) + gates[:, :, 3] * (y_swa @ p_swa))
    return merged @ w_out


def peer_ffn(x, w_q, sub_k1, sub_k2, u_tab, v_tab):
    B, S, D = x.shape
    T = B * S
    xt = x.reshape(T, D)
    q = (xt @ w_q).reshape(T, PEER_HEADS, 2, PEER_DQ // 2)
    s1 = jnp.einsum('thc,nc->thn', q[:, :, 0], sub_k1, preferred_element_type=jnp.float32)
    s2 = jnp.einsum('thc,nc->thn', q[:, :, 1], sub_k2, preferred_element_type=jnp.float32)
    v1, i1 = lax.top_k(s1, PEER_TOPK)
    v2, i2 = lax.top_k(s2, PEER_TOPK)
    cand_s = (v1[..., :, None] + v2[..., None, :]).reshape(T, PEER_HEADS, PEER_TOPK * PEER_TOPK)
    cand_i = (i1[..., :, None] * N_KEYS + i2[..., None, :]).reshape(T, PEER_HEADS, PEER_TOPK * PEER_TOPK)
    top_s, top_j = lax.top_k(cand_s, PEER_TOPK)
    idx = jnp.take_along_axis(cand_i, top_j, axis=-1)
    g = jax.nn.softmax(top_s, axis=-1).astype(x.dtype)
    nc = T // PEER_CHUNK
    idx = idx.reshape(nc, PEER_CHUNK, PEER_HEADS * PEER_TOPK)
    g = g.reshape(nc, PEER_CHUNK, PEER_HEADS * PEER_TOPK)
    xc = xt.reshape(nc, PEER_CHUNK, D)

    def one_chunk(args):
        xb, ib, gb = args
        act = jax.nn.gelu(jnp.einsum('cd,ced->ce', xb, u_tab[ib]), approximate=False)
        return jnp.einsum('ce,ced->cd', gb * act, v_tab[ib])

    return lax.map(one_chunk, (xc, idx, g)).reshape(B, S, D)


def setup_inputs(seed: int = 0) -> dict:
    key = jax.random.key(seed)
    ks = jax.random.split(key, 32)
    n = lambda k, shape: jax.random.normal(k, shape, jnp.float32)
    L, D = DEPTH, D_MODEL
    return {
        "x": n(ks[0], (BATCH, SEQ, D)),
        "w_in": n(ks[1], (L, D, D_IN)) * D ** -0.5,
        "b_f": 3.0 + 0.5 * n(ks[2], (L, FOX_HEADS)),
        "swa_sinks": 0.5 * n(ks[3], (L, SWA_HEADS)),
        "pool_w": n(ks[4], (L, POOL_GROUPS, POOL_GW, POOL_GW)) * POOL_GW ** -0.5,
        "pool_scale": 1.0 + 0.1 * n(ks[5], (L, POOL_W)),
        "dw_w": n(ks[6], (L, CONV_K, CONV_W)) * CONV_K ** -0.5,
        "dw_b": 0.02 * n(ks[7], (L, CONV_W)),
        "conv_ln_g": 1.0 + 0.1 * n(ks[8], (L, CONV_W)),
        "conv_ln_b": 0.02 * n(ks[9], (L, CONV_W)),
        "p_fox": n(ks[10], (L, FOX_HEADS * FOX_HD, D)) * (FOX_HEADS * FOX_HD) ** -0.5 * BETA,
        "p_swa": n(ks[11], (L, SWA_HEADS * SWA_HD, D)) * (SWA_HEADS * SWA_HD) ** -0.5 * BETA,
        "p_pool": n(ks[12], (L, POOL_W, D)) * POOL_W ** -0.5 * BETA,
        "p_conv": n(ks[13], (L, CONV_W, D)) * CONV_W ** -0.5 * BETA,
        "w_out": n(ks[14], (L, D, D)) * D ** -0.5 * BETA,
        "ln1_g": 1.0 + 0.1 * n(ks[15], (L, D)),
        "ln1_b": 0.02 * n(ks[16], (L, D)),
        "peer_wq": n(ks[17], (L, D, PEER_HEADS * PEER_DQ)) * D ** -0.5,
        "peer_k1": n(ks[18], (L, N_KEYS, PEER_DQ // 2)) * (PEER_DQ // 2) ** -0.5,
        "peer_k2": n(ks[19], (L, N_KEYS, PEER_DQ // 2)) * (PEER_DQ // 2) ** -0.5,
        "peer_u": n(ks[20], (L, N_EXPERTS, D)) * D ** -0.5,
        "peer_v": n(ks[21], (L, N_EXPERTS, D)) * BETA,
        "ln2_g": 1.0 + 0.1 * n(ks[22], (L, D)),
        "ln2_b": 0.02 * n(ks[23], (L, D)),
    }


def reference(x, w_in, b_f, swa_sinks, pool_w, pool_scale, dw_w, dw_b, conv_ln_g, conv_ln_b,
              p_fox, p_swa, p_pool, p_conv, w_out, ln1_g, ln1_b,
              peer_wq, peer_k1, peer_k2, peer_u, peer_v, ln2_g, ln2_b):
    for l in range(DEPTH):
        mix = hybrid_mixer(x, w_in[l], b_f[l], swa_sinks[l], pool_w[l], pool_scale[l], dw_w[l], dw_b[l],
                           conv_ln_g[l], conv_ln_b[l], p_fox[l], p_swa[l], p_pool[l], p_conv[l], w_out[l])
        x = layer_norm(ALPHA * x + mix, ln1_g[l], ln1_b[l])
        ffn = peer_ffn(x, peer_wq[l], peer_k1[l], peer_k2[l], peer_u[l], peer_v[l])
        x = layer_norm(ALPHA * x + ffn, ln2_g[l], ln2_b[l])
    return x
```

```python
import functools
import math

import numpy as np
import jax
import jax.numpy as jnp
from jax import lax
from jax.experimental import pallas as pl
from jax.experimental.pallas import tpu as pltpu

F32 = jnp.float32
BF16 = jnp.bfloat16

D_MODEL = 1024
DEPTH = 2
FOX_HEADS = 8
HEAD_DIM = 64
SWA_HEADS = 8
SWA_KV = 2
WINDOW = 128
POOL_WINDOWS = (2, 4, 8, 16)
POOL_GW = 128
POOL_W = 512
CONV_W = 512
CONV_K = 31
N_BRANCH = 4
PEER_HEADS = 8
N_KEYS = 128
N_EXPERTS = N_KEYS * N_KEYS
PEER_TOPK = 16
LN_EPS = 1e-5
ALPHA = (2 * DEPTH) ** 0.25
NEG_BIG = -1e30

COL_QF, COL_KF, COL_VF, COL_QS, COL_POOL, COL_KS, COL_VS, COL_FL, COL_CONV = (
    0, 512, 1024, 1536, 2048, 2560, 2688, 2816, 3072)
H_COLS = 4096
LANE = 128
AUG_A = 6
AUG_B = 12

VMEM_LIMIT = 56 * 1024 * 1024


def _cparams(sem):
    return pltpu.CompilerParams(dimension_semantics=sem, vmem_limit_bytes=VMEM_LIMIT)


def _layer_norm(z, g, b):
    mu = jnp.mean(z, axis=-1, keepdims=True)
    zc = z - mu
    var = jnp.mean(zc * zc, axis=-1, keepdims=True)
    return zc * lax.rsqrt(var + LN_EPS) * g + b


def _sigmoid(z):
    return 1.0 / (1.0 + jnp.exp(-z))


def _inproj_kernel(x_ref, w_ref, o_ref, *, n_chunk):
    x = x_ref[...]
    for c in range(0, o_ref.shape[1], n_chunk):
        o_ref[:, c:c + n_chunk] = jnp.dot(
            x, w_ref[:, c:c + n_chunk], preferred_element_type=F32).astype(o_ref.dtype)


def _inproj(xb, w, tm=512):
    T, K = xb.shape
    N = w.shape[1]
    return pl.pallas_call(
        functools.partial(_inproj_kernel, n_chunk=1024),
        grid=(T // tm,),
        in_specs=[pl.BlockSpec((tm, K), lambda i: (i, 0)),
                  pl.BlockSpec((K, N), lambda i: (0, 0))],
        out_specs=pl.BlockSpec((tm, N), lambda i: (i, 0)),
        out_shape=jax.ShapeDtypeStruct((T, N), BF16),
        compiler_params=_cparams(("parallel",)),
        name="inproj",
    )(xb, w)


def _split3(v):
    hi = v.astype(BF16)
    r1 = v - hi.astype(F32)
    mid = r1.astype(BF16)
    r2 = r1 - mid.astype(F32)
    return hi, mid, r2.astype(BF16)


def _decay_kernel(fl_ref, bf_ref, selq_ref, selk_ref, cq_ref, ck_ref, aq_ref, ak_ref, carry_ref):
    ts = fl_ref.shape[0]

    @pl.when(pl.program_id(1) == 0)
    def _():
        carry_ref[...] = jnp.zeros_like(carry_ref)

    z = fl_ref[...].astype(F32) + bf_ref[...]
    ls = jnp.minimum(z, 0.0) - jnp.log1p(jnp.exp(-jnp.abs(z)))
    row = lax.broadcasted_iota(jnp.int32, (ts, ts), 0)
    col = lax.broadcasted_iota(jnp.int32, (ts, ts), 1)
    tri = jnp.where(col <= row, 1.0, 0.0).astype(BF16)
    parts = jnp.concatenate(_split3(ls), axis=1)
    cs = jnp.dot(tri, parts, preferred_element_type=F32)
    c = cs[:, :LANE] + cs[:, LANE:2 * LANE] + cs[:, 2 * LANE:] + carry_ref[0:1, :]
    carry_ref[...] = jnp.broadcast_to(c[ts - 1:ts, :], carry_ref.shape)
    cparts = jnp.concatenate(_split3(c), axis=1)
    aq_ref[...] = (jnp.dot(cparts, selq_ref[...], preferred_element_type=F32) + cq_ref[...]).astype(BF16)
    ak_ref[...] = (jnp.dot(cparts, selk_ref[...], preferred_element_type=F32) + ck_ref[...]).astype(BF16)


def _decay_consts():
    selq = np.zeros((3 * LANE, 4 * LANE), np.float32)
    selk = np.zeros((3 * LANE, 4 * LANE), np.float32)
    cq = np.zeros((1, 4 * LANE), np.float32)
    ck = np.zeros((1, 4 * LANE), np.float32)
    for p in range(4):
        for part in range(3):
            for hh in range(2):
                base = p * LANE + hh * AUG_A
                selq[part * LANE + 2 * p + hh, base + part] = 1.0
                cq[0, base + 3 + part] = 1.0
                ck[0, base + part] = 1.0
                selk[part * LANE + 2 * p + hh, base + 3 + part] = -1.0
    return (jnp.asarray(selq, BF16), jnp.asarray(selk, BF16), jnp.asarray(cq), jnp.asarray(ck))


def _decay(h, bf_pad, B, S, ts=512):
    T = B * S
    nt = S // ts
    selq, selk, cq, ck = _decay_consts()
    const = lambda b, j: (0, 0)
    return pl.pallas_call(
        _decay_kernel,
        grid=(B, nt),
        in_specs=[pl.BlockSpec((ts, LANE), lambda b, j: (b * nt + j, COL_FL // LANE)),
                  pl.BlockSpec((1, LANE), const),
                  pl.BlockSpec((3 * LANE, 4 * LANE), const),
                  pl.BlockSpec((3 * LANE, 4 * LANE), const),
                  pl.BlockSpec((1, 4 * LANE), const),
                  pl.BlockSpec((1, 4 * LANE), const)],
        out_specs=[pl.BlockSpec((ts, 4 * LANE), lambda b, j: (b * nt + j, 0)),
                   pl.BlockSpec((ts, 4 * LANE), lambda b, j: (b * nt + j, 0))],
        out_shape=[jax.ShapeDtypeStruct((T, 4 * LANE), BF16)] * 2,
        scratch_shapes=[pltpu.VMEM((8, LANE), F32)],
        compiler_params=_cparams(("parallel", "arbitrary")),
        name="fox_decay",
    )(h, bf_pad, selq, selk, cq, ck)


def _fox_kernel(q_ref, aq_ref, k_ref, ak_ref, v_ref, o_ref, m_scr, acc_scr):
    tq = q_ref.shape[0]
    qi = pl.program_id(2)
    lane2 = lax.broadcasted_iota(jnp.int32, (1, 2 * LANE), 1)
    head_mask = (
        (lane2 < HEAD_DIM) | ((lane2 >= LANE) & (lane2 < LANE + AUG_A)),
        ((lane2 >= HEAD_DIM) & (lane2 < LANE)) | ((lane2 >= LANE + AUG_A) & (lane2 < LANE + AUG_B)),
    )
    qf = jnp.concatenate([q_ref[...], aq_ref[...]], axis=1)
    qs = [jnp.where(mk, qf, jnp.zeros_like(qf)) for mk in head_mask]
    ones_col = jnp.where(lax.broadcasted_iota(jnp.int32, (tq, LANE), 1) == 0, 1.0, 0.0).astype(BF16)
    row = lax.broadcasted_iota(jnp.int32, (tq, tq), 0)
    col = lax.broadcasted_iota(jnp.int32, (tq, tq), 1)

    m_scr[...] = jnp.full(m_scr.shape, NEG_BIG, F32)
    acc_scr[...] = jnp.zeros(acc_scr.shape, F32)

    def step(j, masked):
        off = pl.multiple_of(j * tq, tq)
        kf = jnp.concatenate([k_ref[pl.ds(off, tq), :], ak_ref[pl.ds(off, tq), :]], axis=1)
        vf = jnp.concatenate([v_ref[pl.ds(off, tq), :], ones_col], axis=1)
        for x in range(2):
            s = lax.dot_general(qs[x], kf, (((1,), (1,)), ((), ())), preferred_element_type=F32)
            if masked:
                s = jnp.where(col <= row, s, NEG_BIG)
            m_prev = m_scr[x]
            m_new = jnp.maximum(m_prev, jnp.max(s, axis=1, keepdims=True))
            alpha = jnp.exp(m_prev - m_new)
            p = jnp.exp(s - jnp.concatenate([m_new] * (tq // LANE), axis=1))
            acc_scr[x] = (acc_scr[x] * jnp.concatenate([alpha, alpha], axis=1)
                          + jnp.dot(p.astype(BF16), vf, preferred_element_type=F32))
            m_scr[x] = m_new

    def body(j, carry):
        step(j, False)
        return carry

    lax.fori_loop(0, qi, body, 0)
    step(qi, True)

    outs = []
    for x in range(2):
        acc = acc_scr[x]
        outs.append(acc[:, :LANE] / acc[:, LANE:LANE + 1])
    lane = lax.broadcasted_iota(jnp.int32, (tq, LANE), 1)
    o_ref[...] = jnp.where(lane < HEAD_DIM, outs[0], outs[1]).astype(o_ref.dtype)


def _fox(h, aq, ak, B, S, tq=512):
    T = B * S
    nq = S // tq
    return pl.pallas_call(
        _fox_kernel,
        grid=(B, 4, nq),
        in_specs=[pl.BlockSpec((tq, LANE), lambda b, p, i: (b * nq + i, COL_QF // LANE + p)),
                  pl.BlockSpec((tq, LANE), lambda b, p, i: (b * nq + i, p)),
                  pl.BlockSpec((S, LANE), lambda b, p, i: (b, COL_KF // LANE + p)),
                  pl.BlockSpec((S, LANE), lambda b, p, i: (b, p)),
                  pl.BlockSpec((S, LANE), lambda b, p, i: (b, COL_VF // LANE + p))],
        out_specs=pl.BlockSpec((tq, LANE), lambda b, p, i: (b * nq + i, p)),
        out_shape=jax.ShapeDtypeStruct((T, 4 * LANE), BF16),
        scratch_shapes=[pltpu.VMEM((2, tq, LANE), F32), pltpu.VMEM((2, tq, 2 * LANE), F32)],
        compiler_params=_cparams(("parallel", "parallel", "arbitrary")),
        name="fox_attn",
    )(h, aq, h, ak, h)


def _swa_kernel(sink_ref, q_ref, kc_ref, kp_ref, vc_ref, vp_ref, o_ref):
    n = pl.program_id(1)
    blk = q_ref.shape[0]
    kb = jnp.concatenate([kp_ref[...], kc_ref[...]], axis=0)
    vb = jnp.concatenate([vp_ref[...], vc_ref[...]], axis=0)
    qi = lax.broadcasted_iota(jnp.int32, (blk, 2 * blk), 0)
    kj = lax.broadcasted_iota(jnp.int32, (blk, 2 * blk), 1)
    dist = qi + blk - kj
    valid = (dist >= 0) & (dist < WINDOW) & ((kj >= blk) | (n > 0))
    distf = dist.astype(F32)
    lane = lax.broadcasted_iota(jnp.int32, (blk, LANE), 1)
    lo = lane < HEAD_DIM
    for m in range(4):
        qm = q_ref[:, m * LANE:(m + 1) * LANE]
        outs = []
        for half in range(2):
            hd = m + 4 * half
            slope = 2.0 ** (-8.0 * (hd + 1) / SWA_HEADS)
            qh = jnp.where(lo if half == 0 else jnp.logical_not(lo), qm, jnp.zeros_like(qm))
            s = lax.dot_general(qh, kb, (((1,), (1,)), ((), ())), preferred_element_type=F32)
            s = jnp.where(valid, s - slope * distf, NEG_BIG)
            sink = sink_ref[hd]
            mx = jnp.maximum(jnp.max(s, axis=1, keepdims=True), sink)
            e = jnp.exp(s - mx)
            den = jnp.sum(e, axis=1, keepdims=True) + jnp.exp(sink - mx)
            p = (e / den).astype(BF16)
            outs.append(jnp.dot(p, vb, preferred_element_type=F32))
        o_ref[:, m * LANE:(m + 1) * LANE] = jnp.where(lo, outs[0], outs[1]).astype(o_ref.dtype)


def _swa(h, sinks, B, S, blk=128):
    T = B * S
    nb = S // blk
    cur = lambda c: (lambda b, n: (b * nb + n, c))
    prev = lambda c: (lambda b, n: (b * nb + jnp.maximum(n - 1, 0), c))
    return pl.pallas_call(
        _swa_kernel,
        grid=(B, nb),
        in_specs=[pl.BlockSpec(memory_space=pltpu.SMEM),
                  pl.BlockSpec((blk, 4 * LANE), cur(COL_QS // (4 * LANE))),
                  pl.BlockSpec((blk, LANE), cur(COL_KS // LANE)),
                  pl.BlockSpec((blk, LANE), prev(COL_KS // LANE)),
                  pl.BlockSpec((blk, LANE), cur(COL_VS // LANE)),
                  pl.BlockSpec((blk, LANE), prev(COL_VS // LANE))],
        out_specs=pl.BlockSpec((blk, 4 * LANE), lambda b, n: (b * nb + n, 0)),
        out_shape=jax.ShapeDtypeStruct((T, 4 * LANE), BF16),
        compiler_params=_cparams(("parallel", "arbitrary")),
        name="swa_attn",
    )(sinks, h, h, h, h, h)


def _pool_kernel(xc_ref, xp_ref, w_ref, sc_ref, o_ref):
    j = pl.program_id(1)
    ts = xc_ref.shape[0]
    hal = xp_ref.shape[0]
    r = lax.broadcasted_iota(jnp.int32, (ts, ts + hal), 0)
    c = lax.broadcasted_iota(jnp.int32, (ts, ts + hal), 1) - hal
    t_glob = (lax.broadcasted_iota(jnp.int32, (ts, LANE), 0) + j * ts + 1).astype(F32)
    has_prev = j > 0
    for g, w in enumerate(POOL_WINDOWS):
        xg = xc_ref[:, g * LANE:(g + 1) * LANE]
        xp = xp_ref[:, g * LANE:(g + 1) * LANE]
        xp = jnp.where(has_prev, xp, jnp.zeros_like(xp))
        ext = jnp.concatenate([xp, xg], axis=0)
        band = jnp.where((c <= r) & (c > r - w), 1.0, 0.0).astype(BF16)
        win = jnp.dot(band, ext, preferred_element_type=F32)
        cnt = jnp.minimum(t_glob, float(w))
        pooled = win / cnt - xg.astype(F32)
        y = jnp.dot(pooled.astype(BF16), w_ref[g], preferred_element_type=F32)
        o_ref[:, g * LANE:(g + 1) * LANE] = (y * sc_ref[:, g * LANE:(g + 1) * LANE]).astype(o_ref.dtype)


def _pool(h, pool_w, pool_scale, B, S, ts=512, hal=128):
    T = B * S
    nt = S // ts
    r = ts // hal
    return pl.pallas_call(
        _pool_kernel,
        grid=(B, nt),
        in_specs=[pl.BlockSpec((ts, POOL_W), lambda b, j: (b * nt + j, COL_POOL // POOL_W)),
                  pl.BlockSpec((hal, POOL_W),
                               lambda b, j: (jnp.maximum((b * nt + j) * r - 1, 0), COL_POOL // POOL_W)),
                  pl.BlockSpec((4, POOL_GW, POOL_GW), lambda b, j: (0, 0, 0)),
                  pl.BlockSpec((1, POOL_W), lambda b, j: (0, 0))],
        out_specs=pl.BlockSpec((ts, POOL_W), lambda b, j: (b * nt + j, 0)),
        out_shape=jax.ShapeDtypeStruct((T, POOL_W), BF16),
        compiler_params=_cparams(("parallel", "arbitrary")),
        name="ms_pool",
    )(h, h, pool_w, pool_scale)


def _conv_kernel(uc_ref, up_ref, w_ref, b_ref, g_ref, bb_ref, o_ref, ext_ref):
    j = pl.program_id(1)
    ts = uc_ref.shape[0]
    pad = ext_ref.shape[0] - ts

    def glu(u):
        u = u.astype(F32)
        return u[:, :CONV_W] * _sigmoid(u[:, CONV_W:])

    hp = glu(up_ref[up_ref.shape[0] - pad:, :])
    ext_ref[0:pad, :] = jnp.where(j > 0, hp, jnp.zeros_like(hp))
    ext_ref[pad:, :] = glu(uc_ref[...])
    acc = jnp.zeros((ts, CONV_W), F32)
    for k in range(CONV_K):
        off = pad - (CONV_K - 1) + k
        acc = acc + ext_ref[off:off + ts, :] * w_ref[k:k + 1, :]
    y = _layer_norm(acc + b_ref[...], g_ref[...], bb_ref[...])
    o_ref[...] = (y * _sigmoid(y)).astype(o_ref.dtype)


def _conv(h, dw_w, dw_b, ln_g, ln_b, B, S, ts=512, hal=128):
    T = B * S
    nt = S // ts
    r = ts // hal
    vec = pl.BlockSpec((1, CONV_W), lambda b, j: (0, 0))
    return pl.pallas_call(
        _conv_kernel,
        grid=(B, nt),
        in_specs=[pl.BlockSpec((ts, 2 * CONV_W), lambda b, j: (b * nt + j, COL_CONV // (2 * CONV_W))),
                  pl.BlockSpec((hal, 2 * CONV_W),
                               lambda b, j: (jnp.maximum((b * nt + j) * r - 1, 0), COL_CONV // (2 * CONV_W))),
                  pl.BlockSpec((32, CONV_W), lambda b, j: (0, 0)),
                  vec, vec, vec],
        out_specs=pl.BlockSpec((ts, CONV_W), lambda b, j: (b * nt + j, 0)),
        out_shape=jax.ShapeDtypeStruct((T, CONV_W), BF16),
        scratch_shapes=[pltpu.VMEM((ts + 32, CONV_W), F32)],
        compiler_params=_cparams(("parallel", "arbitrary")),
        name="conf_conv",
    )(h, h, dw_w, dw_b, ln_g, ln_b)


def _merge_kernel(xb_ref, x_ref, yf_ref, yp_ref, yc_ref, ys_ref, wg_ref, pf_ref, pp_ref, pc_ref, ps_ref,
                  wo_ref, g_ref, b_ref, x1_ref, x1t_ref):
    xb = xb_ref[...]
    merged = None
    for br, (y_ref, p_ref) in enumerate(((yf_ref, pf_ref), (yp_ref, pp_ref), (yc_ref, pc_ref), (ys_ref, ps_ref))):
        gate = _sigmoid(jnp.dot(xb, wg_ref[:, br * D_MODEL:(br + 1) * D_MODEL], preferred_element_type=F32))
        term = gate * jnp.dot(y_ref[...], p_ref[...], preferred_element_type=F32)
        merged = term if merged is None else merged + term
    mix = jnp.dot(merged.astype(BF16), wo_ref[...], preferred_element_type=F32)
    x1 = _layer_norm(ALPHA * x_ref[...] + mix, g_ref[...], b_ref[...])
    x1_ref[...] = x1
    x1t_ref[...] = x1.T.astype(BF16)


def _merge(xb, x, yf, yp, yc, ys, wg, pf, pp, pc, ps, wo, g, b, tm=256):
    T = xb.shape[0]
    const = lambda i: (0, 0)
    tok = lambda w: pl.BlockSpec((tm, w), lambda i: (i, 0))
    wspec = lambda a: pl.BlockSpec(a.shape, const)
    return pl.pallas_call(
        _merge_kernel,
        grid=(T // tm,),
        in_specs=[tok(D_MODEL), tok(D_MODEL), tok(512), tok(512), tok(512), tok(512),
                  wspec(wg), wspec(pf), wspec(pp), wspec(pc), wspec(ps), wspec(wo), wspec(g), wspec(b)],
        out_specs=[tok(D_MODEL), pl.BlockSpec((D_MODEL, tm), lambda i: (0, i))],
        out_shape=[jax.ShapeDtypeStruct((T, D_MODEL), F32), jax.ShapeDtypeStruct((D_MODEL, T), BF16)],
        compiler_params=_cparams(("parallel",)),
        name="merge_ln1",
    )(xb, x, yf, yp, yc, ys, wg, pf, pp, pc, ps, wo, g, b)


def _wf_kernel(k_ref, wq_ref, o_ref):
    o_ref[...] = jnp.dot(k_ref[0], wq_ref[...], preferred_element_type=F32).astype(o_ref.dtype)


def _peer_score_weights(wq_t, keys):
    nblk = wq_t.shape[0] // N_KEYS
    return pl.pallas_call(
        _wf_kernel,
        grid=(nblk,),
        in_specs=[pl.BlockSpec((1, N_KEYS, N_KEYS), lambda j: (j % 2, 0, 0)),
                  pl.BlockSpec((N_KEYS, D_MODEL), lambda j: (j, 0))],
        out_specs=pl.BlockSpec((N_KEYS, D_MODEL), lambda j: (j, 0)),
        out_shape=jax.ShapeDtypeStruct(wq_t.shape, BF16),
        compiler_params=_cparams(("parallel",)),
        name="peer_wf",
    )(keys, wq_t)


N_RANK = PEER_TOPK + 1


def _n_cand():
    return [(r, c) for r in range(N_RANK) for c in range(N_RANK) if (r + 1) * (c + 1) <= N_RANK]


def _gelu(z):
    return 0.5 * z * (1.0 + lax.erf(z * (1.0 / math.sqrt(2.0))))


def _extract_top(work_ref, out_ref, n_slab, n_out):
    def round_(r, carry):
        m = work_ref[0]
        for s in range(1, n_slab):
            m = jnp.maximum(m, work_ref[s])
        out_ref[r] = m
        for s in range(n_slab):
            w = work_ref[s]
            work_ref[s] = jnp.where(w == m, NEG_BIG, w)
        return carry
    lax.fori_loop(0, n_out, round_, 0)


def _peer_kernel(xt_ref, wf_ref, u_ref, vt_ref, x1_ref, g_ref, b_ref, x2_ref, x2b_ref,
                 th_scr, e1_scr, s2_scr, e2_scr, y_scr, work_scr, top_scr, cand_scr, csel_scr):
    k = pl.program_id(1)
    nk = pl.num_programs(1)
    tm = xt_ref.shape[1]
    nh = PEER_HEADS
    xt = xt_ref[...]

    @pl.when(k == 0)
    def _select():
        y_scr[...] = jnp.zeros_like(y_scr)
        st = jnp.dot(wf_ref[...], xt, preferred_element_type=F32)
        rows = N_KEYS * nh
        tops = []
        for half in range(2):
            work_scr[...] = st[half * rows:(half + 1) * rows].reshape(N_KEYS, nh, tm)
            _extract_top(work_scr, top_scr.at[half], N_KEYS, N_RANK)
        cands = _n_cand()
        for ci, (r, c) in enumerate(cands):
            cand_scr[ci] = top_scr[0, r] + top_scr[1, c]
        _extract_top(cand_scr, csel_scr, len(cands), N_RANK)
        m0 = csel_scr[0]
        zsum = jnp.zeros_like(m0)
        for r in range(PEER_TOPK):
            zsum = zsum + jnp.exp(csel_scr[r] - m0)
        tau = 0.5 * (csel_scr[PEER_TOPK - 1] + csel_scr[PEER_TOPK])
        a0 = top_scr[0, 0]
        b0 = top_scr[1, 0]
        s1 = st[0:rows].reshape(N_KEYS, nh, tm)
        th_scr[...] = (tau[None] - s1).reshape(rows, tm)
        e1_scr[...] = (jnp.exp(s1 - a0[None]) / zsum[None]).reshape(rows, tm)
        s2 = st[2 * rows:3 * rows]
        s2_scr[...] = s2
        for hh in range(nh):
            sl = slice(hh * N_KEYS, (hh + 1) * N_KEYS)
            e2_scr[sl, :] = jnp.exp(s2[sl] - b0[hh:hh + 1, :])

    eb = u_ref.shape[0]
    ht = jnp.dot(u_ref[...], xt, preferred_element_type=F32)
    wts = []
    for ii in range(eb // N_KEYS):
        base = pl.multiple_of((k * (eb // N_KEYS) + ii) * nh, nh)
        th_i = th_scr[pl.ds(base, nh), :]
        e1_i = e1_scr[pl.ds(base, nh), :]
        gsum = jnp.zeros((N_KEYS, tm), F32)
        for hh in range(nh):
            sl = slice(hh * N_KEYS, (hh + 1) * N_KEYS)
            gsum = gsum + jnp.where(s2_scr[sl, :] >= th_i[hh:hh + 1, :], e2_scr[sl, :], 0.0) * e1_i[hh:hh + 1, :]
        wts.append((_gelu(ht[ii * N_KEYS:(ii + 1) * N_KEYS]) * gsum).astype(BF16))
    wt = jnp.concatenate(wts, axis=0)
    y_scr[...] += jnp.dot(vt_ref[...], wt, preferred_element_type=F32)

    @pl.when(k == nk - 1)
    def _finish():
        z = ALPHA * x1_ref[...] + y_scr[...].T
        x2 = _layer_norm(z, g_ref[...], b_ref[...])
        x2_ref[...] = x2
        x2b_ref[...] = x2.astype(BF16)


def _peer(x1t, wf, u, vt, x1, g, b, tm=512, eb=256):
    T = x1.shape[0]
    ne = u.shape[0]
    ncand = len(_n_cand())
    rows = N_KEYS * PEER_HEADS
    const = lambda t, k: (0, 0)
    return pl.pallas_call(
        _peer_kernel,
        grid=(T // tm, ne // eb),
        in_specs=[pl.BlockSpec((D_MODEL, tm), lambda t, k: (0, t)),
                  pl.BlockSpec(wf.shape, const),
                  pl.BlockSpec((eb, D_MODEL), lambda t, k: (k, 0)),
                  pl.BlockSpec((D_MODEL, eb), lambda t, k: (0, k)),
                  pl.BlockSpec((tm, D_MODEL), lambda t, k: (t, 0)),
                  pl.BlockSpec((1, D_MODEL), const),
                  pl.BlockSpec((1, D_MODEL), const)],
        out_specs=[pl.BlockSpec((tm, D_MODEL), lambda t, k: (t, 0)),
                   pl.BlockSpec((tm, D_MODEL), lambda t, k: (t, 0))],
        out_shape=[jax.ShapeDtypeStruct((T, D_MODEL), F32), jax.ShapeDtypeStruct((T, D_MODEL), BF16)],
        scratch_shapes=[pltpu.VMEM((rows, tm), F32), pltpu.VMEM((rows, tm), F32),
                        pltpu.VMEM((rows, tm), F32), pltpu.VMEM((rows, tm), F32),
                        pltpu.VMEM((D_MODEL, tm), F32),
                        pltpu.VMEM((N_KEYS, PEER_HEADS, tm), F32),
                        pltpu.VMEM((2, N_RANK, PEER_HEADS, tm), F32),
                        pltpu.VMEM((ncand, PEER_HEADS, tm), F32),
                        pltpu.VMEM((N_RANK, PEER_HEADS, tm), F32)],
        compiler_params=_cparams(("parallel", "arbitrary")),
        name="peer_ln2",
    )(x1t, wf, u, vt, x1, g, b)


def _prep_w_in(w_in):
    sizes = (512, 512, 512, 8, 512, 128, 128, 512, 1024, 4096)
    offs = np.cumsum((0,) + sizes)
    q_f, k_f, v_f, f_l, q_s, k_s, v_s, x_pool, x_conv, gl = (w_in[:, offs[i]:offs[i + 1]] for i in range(10))
    scale = HEAD_DIM ** -0.5
    q_s = q_s.reshape(D_MODEL, 2, 4, HEAD_DIM).transpose(0, 2, 1, 3).reshape(D_MODEL, 512)
    zeros = lambda n: jnp.zeros((D_MODEL, n), w_in.dtype)
    w_h = jnp.concatenate([q_f * scale, k_f, v_f, q_s * scale, x_pool, k_s, v_s, f_l, zeros(120), zeros(128), x_conv],
                          axis=1)
    return w_h.astype(BF16), gl.astype(BF16)


def _prep_peer(wq, k1, k2):
    wq_t = wq.T.astype(BF16)
    keys = jnp.stack([k1, k2]).astype(BF16)
    wf = _peer_score_weights(wq_t, keys)
    wf = wf.reshape(PEER_HEADS, 2, N_KEYS, D_MODEL)
    kh = lambda half: wf[:, half].transpose(1, 0, 2).reshape(N_KEYS * PEER_HEADS, D_MODEL)
    hk = wf[:, 1].reshape(PEER_HEADS * N_KEYS, D_MODEL)
    return jnp.concatenate([kh(0), kh(1), hk], axis=0)


def _pad_lanes(v, n):
    return jnp.zeros((1, n), F32).at[0, :v.shape[0]].set(v.astype(F32))


def kernel(x, w_in, b_f, swa_sinks, pool_w, pool_scale, dw_w, dw_b, conv_ln_g, conv_ln_b, p_fox, p_swa, p_pool,
           p_conv, w_out, ln1_g, ln1_b, peer_wq, peer_k1, peer_k2, peer_u, peer_v, ln2_g, ln2_b):
    B, S, D = x.shape
    T = B * S
    xf = x.reshape(T, D)
    xb = xf.astype(BF16)
    row = lambda v: v.reshape(1, -1).astype(F32)
    for l in range(DEPTH):
        w_h, w_gate = _prep_w_in(w_in[l])
        h = _inproj(xb, w_h)
        aq, ak = _decay(h, _pad_lanes(b_f[l], LANE), B, S)
        y_fox = _fox(h, aq, ak, B, S)
        y_swa = _swa(h, swa_sinks[l].astype(F32), B, S)
        y_pool = _pool(h, pool_w[l].astype(BF16), row(pool_scale[l]), B, S)
        dw = jnp.zeros((32, CONV_W), F32).at[:CONV_K].set(dw_w[l])
        y_conv = _conv(h, dw, row(dw_b[l]), row(conv_ln_g[l]), row(conv_ln_b[l]), B, S)
        ps = p_swa[l].reshape(2, 4, HEAD_DIM, D).transpose(1, 0, 2, 3).reshape(512, D)
        x1, x1t = _merge(xb, xf, y_fox, y_pool, y_conv, y_swa, w_gate, p_fox[l].astype(BF16),
                         p_pool[l].astype(BF16), p_conv[l].astype(BF16), ps.astype(BF16),
                         w_out[l].astype(BF16), row(ln1_g[l]), row(ln1_b[l]))
        wf = _prep_peer(peer_wq[l], peer_k1[l], peer_k2[l])
        xf, xb = _peer(x1t, wf, peer_u[l].astype(BF16), peer_v[l].T.astype(BF16), x1,
                       row(ln2_g[l]), row(ln2_b[l]))
    return xf.reshape(B, S, D)
```

```python
import functools
import math

import numpy as np
import jax
import jax.numpy as jnp
from jax import lax
from jax.experimental import pallas as pl
from jax.experimental.pallas import tpu as pltpu

F32 = jnp.float32
BF16 = jnp.bfloat16

D_MODEL = 1024
DEPTH = 2
FOX_HEADS = 8
HEAD_DIM = 64
SWA_HEADS = 8
SWA_KV = 2
WINDOW = 128
POOL_WINDOWS = (2, 4, 8, 16)
POOL_GW = 128
POOL_W = 512
CONV_W = 512
CONV_K = 31
N_BRANCH = 4
PEER_HEADS = 8
N_KEYS = 128
N_EXPERTS = N_KEYS * N_KEYS
PEER_TOPK = 16
LN_EPS = 1e-5
ALPHA = (2 * DEPTH) ** 0.25
NEG_BIG = -1e30

COL_QF, COL_KF, COL_VF, COL_QS, COL_POOL, COL_KS, COL_VS, COL_FL, COL_CONV = (
    0, 512, 1024, 1536, 2048, 2560, 2688, 2816, 3072)
H_COLS = 4096
LANE = 128
AUG_A = 6
AUG_B = 12

VMEM_LIMIT = 56 * 1024 * 1024


def _cparams(sem, flags=None):
    return pltpu.CompilerParams(dimension_semantics=sem, vmem_limit_bytes=VMEM_LIMIT, flags=flags)


def _layer_norm(z, g, b):
    mu = jnp.mean(z, axis=-1, keepdims=True)
    zc = z - mu
    var = jnp.mean(zc * zc, axis=-1, keepdims=True)
    return zc * lax.rsqrt(var + LN_EPS) * g + b


def _sigmoid(z):
    return 1.0 / (1.0 + jnp.exp(-z))


def _inproj_kernel(x_ref, w_ref, o_ref, *, n_chunk):
    x = x_ref[...]
    for c in range(0, o_ref.shape[1], n_chunk):
        o_ref[:, c:c + n_chunk] = jnp.dot(
            x, w_ref[:, c:c + n_chunk], preferred_element_type=F32).astype(o_ref.dtype)


def _inproj(xb, w, tm=512):
    T, K = xb.shape
    N = w.shape[1]
    return pl.pallas_call(
        functools.partial(_inproj_kernel, n_chunk=1024),
        grid=(T // tm,),
        in_specs=[pl.BlockSpec((tm, K), lambda i: (i, 0)),
                  pl.BlockSpec((K, N), lambda i: (0, 0))],
        out_specs=pl.BlockSpec((tm, N), lambda i: (i, 0)),
        out_shape=jax.ShapeDtypeStruct((T, N), BF16),
        compiler_params=_cparams(("parallel",)),
        name="inproj",
    )(xb, w)


def _split3(v):
    hi = v.astype(BF16)
    r1 = v - hi.astype(F32)
    mid = r1.astype(BF16)
    r2 = r1 - mid.astype(F32)
    return hi, mid, r2.astype(BF16)


def _decay_kernel(fl_ref, bf_ref, selq_ref, selk_ref, cq_ref, ck_ref, aq_ref, ak_ref, carry_ref):
    ts = fl_ref.shape[0]

    @pl.when(pl.program_id(1) == 0)
    def _():
        carry_ref[...] = jnp.zeros_like(carry_ref)

    z = fl_ref[...].astype(F32) + bf_ref[...]
    ls = jnp.minimum(z, 0.0) - jnp.log1p(jnp.exp(-jnp.abs(z)))
    row = lax.broadcasted_iota(jnp.int32, (ts, ts), 0)
    col = lax.broadcasted_iota(jnp.int32, (ts, ts), 1)
    tri = jnp.where(col <= row, 1.0, 0.0).astype(BF16)
    parts = jnp.concatenate(_split3(ls), axis=1)
    cs = jnp.dot(tri, parts, preferred_element_type=F32)
    c = cs[:, :LANE] + cs[:, LANE:2 * LANE] + cs[:, 2 * LANE:] + carry_ref[0:1, :]
    carry_ref[...] = jnp.broadcast_to(c[ts - 1:ts, :], carry_ref.shape)
    cparts = jnp.concatenate(_split3(c), axis=1)
    aq_ref[...] = (jnp.dot(cparts, selq_ref[...], preferred_element_type=F32) + cq_ref[...]).astype(BF16)
    ak_ref[...] = (jnp.dot(cparts, selk_ref[...], preferred_element_type=F32) + ck_ref[...]).astype(BF16)


def _decay_consts():
    selq = np.zeros((3 * LANE, 4 * LANE), np.float32)
    selk = np.zeros((3 * LANE, 4 * LANE), np.float32)
    cq = np.zeros((1, 4 * LANE), np.float32)
    ck = np.zeros((1, 4 * LANE), np.float32)
    for p in range(4):
        for part in range(3):
            for hh in range(2):
                base = p * LANE + hh * AUG_A
                selq[part * LANE + 2 * p + hh, base + part] = 1.0
                cq[0, base + 3 + part] = 1.0
                ck[0, base + part] = 1.0
                selk[part * LANE + 2 * p + hh, base + 3 + part] = -1.0
    return (jnp.asarray(selq, BF16), jnp.asarray(selk, BF16), jnp.asarray(cq), jnp.asarray(ck))


def _decay(h, bf_pad, B, S, ts=512):
    T = B * S
    nt = S // ts
    selq, selk, cq, ck = _decay_consts()
    const = lambda b, j: (0, 0)
    return pl.pallas_call(
        _decay_kernel,
        grid=(B, nt),
        in_specs=[pl.BlockSpec((ts, LANE), lambda b, j: (b * nt + j, COL_FL // LANE)),
                  pl.BlockSpec((1, LANE), const),
                  pl.BlockSpec((3 * LANE, 4 * LANE), const),
                  pl.BlockSpec((3 * LANE, 4 * LANE), const),
                  pl.BlockSpec((1, 4 * LANE), const),
                  pl.BlockSpec((1, 4 * LANE), const)],
        out_specs=[pl.BlockSpec((ts, 4 * LANE), lambda b, j: (b * nt + j, 0)),
                   pl.BlockSpec((ts, 4 * LANE), lambda b, j: (b * nt + j, 0))],
        out_shape=[jax.ShapeDtypeStruct((T, 4 * LANE), BF16)] * 2,
        scratch_shapes=[pltpu.VMEM((8, LANE), F32)],
        compiler_params=_cparams(("parallel", "arbitrary")),
        name="fox_decay",
    )(h, bf_pad, selq, selk, cq, ck)


def _fox_kernel(q_ref, aq_ref, k_ref, ak_ref, v_ref, o_ref, m_scr, acc_scr):
    tq = q_ref.shape[0]
    qi = pl.program_id(2)
    lane2 = lax.broadcasted_iota(jnp.int32, (1, 2 * LANE), 1)
    head_mask = (
        (lane2 < HEAD_DIM) | ((lane2 >= LANE) & (lane2 < LANE + AUG_A)),
        ((lane2 >= HEAD_DIM) & (lane2 < LANE)) | ((lane2 >= LANE + AUG_A) & (lane2 < LANE + AUG_B)),
    )
    qf = jnp.concatenate([q_ref[...], aq_ref[...]], axis=1)
    qs = [jnp.where(mk, qf, jnp.zeros_like(qf)) for mk in head_mask]
    ones_col = jnp.where(lax.broadcasted_iota(jnp.int32, (tq, LANE), 1) == 0, 1.0, 0.0).astype(BF16)
    row = lax.broadcasted_iota(jnp.int32, (tq, tq), 0)
    col = lax.broadcasted_iota(jnp.int32, (tq, tq), 1)

    m_scr[...] = jnp.full(m_scr.shape, NEG_BIG, F32)
    acc_scr[...] = jnp.zeros(acc_scr.shape, F32)

    def step(j, masked):
        off = pl.multiple_of(j * tq, tq)
        kf = jnp.concatenate([k_ref[pl.ds(off, tq), :], ak_ref[pl.ds(off, tq), :]], axis=1)
        vf = jnp.concatenate([v_ref[pl.ds(off, tq), :], ones_col], axis=1)
        for x in range(2):
            s = lax.dot_general(qs[x], kf, (((1,), (1,)), ((), ())), preferred_element_type=F32)
            if masked:
                s = jnp.where(col <= row, s, NEG_BIG)
            m_prev = m_scr[x]
            m_new = jnp.maximum(m_prev, jnp.max(s, axis=1, keepdims=True))
            alpha = jnp.exp(m_prev - m_new)
            p = jnp.exp(s - jnp.concatenate([m_new] * (tq // LANE), axis=1))
            acc_scr[x] = (acc_scr[x] * jnp.concatenate([alpha, alpha], axis=1)
                          + jnp.dot(p.astype(BF16), vf, preferred_element_type=F32))
            m_scr[x] = m_new

    def body(j, carry):
        step(j, False)
        return carry

    lax.fori_loop(0, qi, body, 0)
    step(qi, True)

    outs = []
    for x in range(2):
        acc = acc_scr[x]
        outs.append(acc[:, :LANE] / acc[:, LANE:LANE + 1])
    lane = lax.broadcasted_iota(jnp.int32, (tq, LANE), 1)
    o_ref[...] = jnp.where(lane < HEAD_DIM, outs[0], outs[1]).astype(o_ref.dtype)


def _fox(h, aq, ak, B, S, tq=512):
    T = B * S
    nq = S // tq
    return pl.pallas_call(
        _fox_kernel,
        grid=(B, 4, nq),
        in_specs=[pl.BlockSpec((tq, LANE), lambda b, p, i: (b * nq + i, COL_QF // LANE + p)),
                  pl.BlockSpec((tq, LANE), lambda b, p, i: (b * nq + i, p)),
                  pl.BlockSpec((S, LANE), lambda b, p, i: (b, COL_KF // LANE + p)),
                  pl.BlockSpec((S, LANE), lambda b, p, i: (b, p)),
                  pl.BlockSpec((S, LANE), lambda b, p, i: (b, COL_VF // LANE + p))],
        out_specs=pl.BlockSpec((tq, LANE), lambda b, p, i: (b * nq + i, p)),
        out_shape=jax.ShapeDtypeStruct((T, 4 * LANE), BF16),
        scratch_shapes=[pltpu.VMEM((2, tq, LANE), F32), pltpu.VMEM((2, tq, 2 * LANE), F32)],
        compiler_params=_cparams(("parallel", "parallel", "arbitrary")),
        name="fox_attn",
    )(h, aq, h, ak, h)


def _swa_kernel(sink_ref, q_ref, kc_ref, kp_ref, vc_ref, vp_ref, o_ref):
    n = pl.program_id(1)
    blk = q_ref.shape[0]
    kb = jnp.concatenate([kp_ref[...], kc_ref[...]], axis=0)
    vb = jnp.concatenate([vp_ref[...], vc_ref[...]], axis=0)
    qi = lax.broadcasted_iota(jnp.int32, (blk, 2 * blk), 0)
    kj = lax.broadcasted_iota(jnp.int32, (blk, 2 * blk), 1)
    dist = qi + blk - kj
    valid = (dist >= 0) & (dist < WINDOW) & ((kj >= blk) | (n > 0))
    distf = dist.astype(F32)
    lane = lax.broadcasted_iota(jnp.int32, (blk, LANE), 1)
    lo = lane < HEAD_DIM
    for m in range(4):
        qm = q_ref[:, m * LANE:(m + 1) * LANE]
        outs = []
        for half in range(2):
            hd = m + 4 * half
            slope = 2.0 ** (-8.0 * (hd + 1) / SWA_HEADS)
            qh = jnp.where(lo if half == 0 else jnp.logical_not(lo), qm, jnp.zeros_like(qm))
            s = lax.dot_general(qh, kb, (((1,), (1,)), ((), ())), preferred_element_type=F32)
            s = jnp.where(valid, s - slope * distf, NEG_BIG)
            sink = sink_ref[hd]
            mx = jnp.maximum(jnp.max(s, axis=1, keepdims=True), sink)
            e = jnp.exp(s - mx)
            den = jnp.sum(e, axis=1, keepdims=True) + jnp.exp(sink - mx)
            p = (e / den).astype(BF16)
            outs.append(jnp.dot(p, vb, preferred_element_type=F32))
        o_ref[:, m * LANE:(m + 1) * LANE] = jnp.where(lo, outs[0], outs[1]).astype(o_ref.dtype)


def _swa(h, sinks, B, S, blk=128):
    T = B * S
    nb = S // blk
    cur = lambda c: (lambda b, n: (b * nb + n, c))
    prev = lambda c: (lambda b, n: (b * nb + jnp.maximum(n - 1, 0), c))
    return pl.pallas_call(
        _swa_kernel,
        grid=(B, nb),
        in_specs=[pl.BlockSpec(memory_space=pltpu.SMEM),
                  pl.BlockSpec((blk, 4 * LANE), cur(COL_QS // (4 * LANE))),
                  pl.BlockSpec((blk, LANE), cur(COL_KS // LANE)),
                  pl.BlockSpec((blk, LANE), prev(COL_KS // LANE)),
                  pl.BlockSpec((blk, LANE), cur(COL_VS // LANE)),
                  pl.BlockSpec((blk, LANE), prev(COL_VS // LANE))],
        out_specs=pl.BlockSpec((blk, 4 * LANE), lambda b, n: (b * nb + n, 0)),
        out_shape=jax.ShapeDtypeStruct((T, 4 * LANE), BF16),
        compiler_params=_cparams(("parallel", "arbitrary")),
        name="swa_attn",
    )(sinks, h, h, h, h, h)


def _pool_kernel(xc_ref, xp_ref, w_ref, sc_ref, o_ref):
    j = pl.program_id(1)
    ts = xc_ref.shape[0]
    hal = xp_ref.shape[0]
    r = lax.broadcasted_iota(jnp.int32, (ts, ts + hal), 0)
    c = lax.broadcasted_iota(jnp.int32, (ts, ts + hal), 1) - hal
    t_glob = (lax.broadcasted_iota(jnp.int32, (ts, LANE), 0) + j * ts + 1).astype(F32)
    has_prev = j > 0
    for g, w in enumerate(POOL_WINDOWS):
        xg = xc_ref[:, g * LANE:(g + 1) * LANE]
        xp = xp_ref[:, g * LANE:(g + 1) * LANE]
        xp = jnp.where(has_prev, xp, jnp.zeros_like(xp))
        ext = jnp.concatenate([xp, xg], axis=0)
        band = jnp.where((c <= r) & (c > r - w), 1.0, 0.0).astype(BF16)
        win = jnp.dot(band, ext, preferred_element_type=F32)
        cnt = jnp.minimum(t_glob, float(w))
        pooled = win / cnt - xg.astype(F32)
        y = jnp.dot(pooled.astype(BF16), w_ref[g], preferred_element_type=F32)
        o_ref[:, g * LANE:(g + 1) * LANE] = (y * sc_ref[:, g * LANE:(g + 1) * LANE]).astype(o_ref.dtype)


def _pool(h, pool_w, pool_scale, B, S, ts=512, hal=128):
    T = B * S
    nt = S // ts
    r = ts // hal
    return pl.pallas_call(
        _pool_kernel,
        grid=(B, nt),
        in_specs=[pl.BlockSpec((ts, POOL_W), lambda b, j: (b * nt + j, COL_POOL // POOL_W)),
                  pl.BlockSpec((hal, POOL_W),
                               lambda b, j: (jnp.maximum((b * nt + j) * r - 1, 0), COL_POOL // POOL_W)),
                  pl.BlockSpec((4, POOL_GW, POOL_GW), lambda b, j: (0, 0, 0)),
                  pl.BlockSpec((1, POOL_W), lambda b, j: (0, 0))],
        out_specs=pl.BlockSpec((ts, POOL_W), lambda b, j: (b * nt + j, 0)),
        out_shape=jax.ShapeDtypeStruct((T, POOL_W), BF16),
        compiler_params=_cparams(("parallel", "arbitrary")),
        name="ms_pool",
    )(h, h, pool_w, pool_scale)


def _conv_kernel(uc_ref, up_ref, w_ref, b_ref, g_ref, bb_ref, o_ref, ext_ref):
    j = pl.program_id(1)
    ts = uc_ref.shape[0]
    pad = ext_ref.shape[0] - ts

    def glu(u):
        u = u.astype(F32)
        return u[:, :CONV_W] * _sigmoid(u[:, CONV_W:])

    hp = glu(up_ref[up_ref.shape[0] - pad:, :])
    ext_ref[0:pad, :] = jnp.where(j > 0, hp, jnp.zeros_like(hp))
    ext_ref[pad:, :] = glu(uc_ref[...])
    acc = jnp.zeros((ts, CONV_W), F32)
    for k in range(CONV_K):
        off = pad - (CONV_K - 1) + k
        acc = acc + ext_ref[off:off + ts, :] * w_ref[k:k + 1, :]
    y = _layer_norm(acc + b_ref[...], g_ref[...], bb_ref[...])
    o_ref[...] = (y * _sigmoid(y)).astype(o_ref.dtype)


def _conv(h, dw_w, dw_b, ln_g, ln_b, B, S, ts=512, hal=128):
    T = B * S
    nt = S // ts
    r = ts // hal
    vec = pl.BlockSpec((1, CONV_W), lambda b, j: (0, 0))
    return pl.pallas_call(
        _conv_kernel,
        grid=(B, nt),
        in_specs=[pl.BlockSpec((ts, 2 * CONV_W), lambda b, j: (b * nt + j, COL_CONV // (2 * CONV_W))),
                  pl.BlockSpec((hal, 2 * CONV_W),
                               lambda b, j: (jnp.maximum((b * nt + j) * r - 1, 0), COL_CONV // (2 * CONV_W))),
                  pl.BlockSpec((32, CONV_W), lambda b, j: (0, 0)),
                  vec, vec, vec],
        out_specs=pl.BlockSpec((ts, CONV_W), lambda b, j: (b * nt + j, 0)),
        out_shape=jax.ShapeDtypeStruct((T, CONV_W), BF16),
        scratch_shapes=[pltpu.VMEM((ts + 32, CONV_W), F32)],
        compiler_params=_cparams(("parallel", "arbitrary")),
        name="conf_conv",
    )(h, h, dw_w, dw_b, ln_g, ln_b)


def _merge_kernel(xb_ref, x_ref, yf_ref, yp_ref, yc_ref, ys_ref, wg_ref, pf_ref, pp_ref, pc_ref, ps_ref,
                  wo_ref, g_ref, b_ref, x1_ref, x1t_ref):
    xb = xb_ref[...]
    merged = None
    for br, (y_ref, p_ref) in enumerate(((yf_ref, pf_ref), (yp_ref, pp_ref), (yc_ref, pc_ref), (ys_ref, ps_ref))):
        gate = _sigmoid(jnp.dot(xb, wg_ref[:, br * D_MODEL:(br + 1) * D_MODEL], preferred_element_type=F32))
        term = gate * jnp.dot(y_ref[...], p_ref[...], preferred_element_type=F32)
        merged = term if merged is None else merged + term
    mix = jnp.dot(merged.astype(BF16), wo_ref[...], preferred_element_type=F32)
    x1 = _layer_norm(ALPHA * x_ref[...] + mix, g_ref[...], b_ref[...])
    x1_ref[...] = x1
    x1t_ref[...] = x1.T.astype(BF16)


def _merge(xb, x, yf, yp, yc, ys, wg, pf, pp, pc, ps, wo, g, b, tm=256):
    T = xb.shape[0]
    const = lambda i: (0, 0)
    tok = lambda w: pl.BlockSpec((tm, w), lambda i: (i, 0))
    wspec = lambda a: pl.BlockSpec(a.shape, const)
    return pl.pallas_call(
        _merge_kernel,
        grid=(T // tm,),
        in_specs=[tok(D_MODEL), tok(D_MODEL), tok(512), tok(512), tok(512), tok(512),
                  wspec(wg), wspec(pf), wspec(pp), wspec(pc), wspec(ps), wspec(wo), wspec(g), wspec(b)],
        out_specs=[tok(D_MODEL), pl.BlockSpec((D_MODEL, tm), lambda i: (0, i))],
        out_shape=[jax.ShapeDtypeStruct((T, D_MODEL), F32), jax.ShapeDtypeStruct((D_MODEL, T), BF16)],
        compiler_params=_cparams(("parallel",)),
        name="merge_ln1",
    )(xb, x, yf, yp, yc, ys, wg, pf, pp, pc, ps, wo, g, b)


def _wf_kernel(k_ref, wq_ref, o_ref):
    o_ref[...] = jnp.dot(k_ref[0], wq_ref[...], preferred_element_type=F32).astype(o_ref.dtype)


def _peer_score_weights(wq_t, keys):
    nblk = wq_t.shape[0] // N_KEYS
    return pl.pallas_call(
        _wf_kernel,
        grid=(nblk,),
        in_specs=[pl.BlockSpec((1, N_KEYS, N_KEYS), lambda j: (j % 2, 0, 0)),
                  pl.BlockSpec((N_KEYS, D_MODEL), lambda j: (j, 0))],
        out_specs=pl.BlockSpec((N_KEYS, D_MODEL), lambda j: (j, 0)),
        out_shape=jax.ShapeDtypeStruct(wq_t.shape, BF16),
        compiler_params=_cparams(("parallel",)),
        name="peer_wf",
    )(keys, wq_t)


N_RANK = PEER_TOPK + 1


def _n_cand():
    return [(r, c) for r in range(N_RANK) for c in range(N_RANK) if (r + 1) * (c + 1) <= N_RANK]


def _gelu(z):
    return 0.5 * z * (1.0 + lax.erf(z * (1.0 / math.sqrt(2.0))))


def _extract_top(work_ref, out_ref, n_slab, n_out):
    def round_(r, prev):
        m = jnp.full(prev.shape, NEG_BIG, F32)
        for s in range(n_slab):
            w = work_ref[s]
            m = jnp.maximum(m, jnp.where(w < prev, w, NEG_BIG))
        out_ref[r] = m
        return m
    lax.fori_loop(0, n_out, round_, jnp.full(work_ref.shape[1:], -NEG_BIG, F32))


def _dup_bf16(v):
    u = pltpu.bitcast(v.astype(BF16).astype(F32), jnp.uint32)
    return u | (u >> 16)


def _row_bf16(slab, hh):
    return pltpu.bitcast(jnp.broadcast_to(slab[hh:hh + 1, :], (N_KEYS // 2, LANE)), BF16)


def _peer_kernel(xt_ref, wf_ref, u0_ref, u_ref, vt_ref, x1_ref, g_ref, b_ref, x2_ref, x2b_ref,
                 n_scr, e1_scr, rank_scr, e2_scr, s2_scr, y_scr, wt0_scr, wt1_scr, ht0_scr, ht1_scr,
                 work_scr, top_scr, cand_scr, csel_scr, thr_scr):
    g = pl.program_id(1)
    ng = pl.num_programs(1)
    tm = xt_ref.shape[1]
    eb = u_ref.shape[0] // 2
    nh = PEER_HEADS
    rows = N_KEYS * nh

    @pl.when(g == 0)
    def _select():
        y_scr[...] = jnp.zeros_like(y_scr)
        ht0_scr[...] = jnp.dot(u0_ref[...], xt_ref[...], preferred_element_type=F32)
        st = jnp.dot(wf_ref[...], xt_ref[...], preferred_element_type=F32)
        for half in range(2):
            work_scr[...] = st[half * rows:(half + 1) * rows].reshape(N_KEYS, nh, tm)
            _extract_top(work_scr, top_scr.at[half], N_KEYS, N_RANK)
        cands = _n_cand()
        for ci, (r, c) in enumerate(cands):
            cand_scr[ci] = top_scr[0, r] + top_scr[1, c]
        _extract_top(cand_scr, csel_scr, len(cands), N_RANK)
        m0 = csel_scr[0]
        zsum = jnp.zeros_like(m0)
        for r in range(PEER_TOPK):
            zsum = zsum + jnp.exp(csel_scr[r] - m0)
        tau = 0.5 * (csel_scr[PEER_TOPK - 1] + csel_scr[PEER_TOPK])
        a0 = top_scr[0, 0]
        for c in range(PEER_TOPK):
            thr_scr[c] = tau - top_scr[1, c]
        work_scr[...] = st[0:rows].reshape(N_KEYS, nh, tm)

        def key_body(i, carry):
            s = work_scr[i]
            cnt = jnp.zeros_like(s)
            for c in range(PEER_TOPK):
                cnt = cnt + jnp.where(s > thr_scr[c], 1.0, 0.0)
            r0 = pl.multiple_of(i * nh, nh)
            n_scr[pl.ds(r0, nh), :] = _dup_bf16(cnt)
            e1_scr[pl.ds(r0, nh), :] = _dup_bf16(jnp.exp(s - a0) / zsum)
            return carry
        lax.fori_loop(0, N_KEYS, key_body, 0)

        s2_scr[...] = st[2 * rows:3 * rows]
        grp = 16
        for hh in range(nh):
            brow = [top_scr[1, c][hh:hh + 1, :] for c in range(N_RANK)]
            mid = [0.5 * (brow[c] + brow[c + 1]) for c in range(PEER_TOPK)]

            def rank_body(jg, carry):
                r0 = pl.multiple_of(hh * N_KEYS + jg * grp, grp)
                s = s2_scr[pl.ds(r0, grp), :]
                cnt = jnp.zeros_like(s)
                for c in range(PEER_TOPK):
                    cnt = cnt + jnp.where(s < mid[c], 1.0, 0.0)
                rank_scr[pl.ds(r0, grp), :] = cnt.astype(BF16)
                e2_scr[pl.ds(r0, grp), :] = jnp.exp(s - brow[0]).astype(BF16)
                return carry
            lax.fori_loop(0, N_KEYS // grp, rank_body, 0)

    nsub = eb // N_KEYS

    def gate_block(sub, chunks):
        for ii in range(nsub):
            base = pl.multiple_of(((g * 2 + sub) * nsub + ii) * nh, nh)
            for c in chunks:
                ls = slice(c * LANE, (c + 1) * LANE)
                n_i = n_scr[pl.ds(base, nh), ls]
                e_i = e1_scr[pl.ds(base, nh), ls]
                acc = None
                for hh in range(nh):
                    rs = slice(hh * N_KEYS, (hh + 1) * N_KEYS)
                    term = jnp.where(rank_scr[rs, ls] < _row_bf16(n_i, hh), e2_scr[rs, ls],
                                     jnp.zeros((), BF16)) * _row_bf16(e_i, hh)
                    acc = term if acc is None else acc + term
                act = _gelu(ht_scr[sub][ii * N_KEYS:(ii + 1) * N_KEYS, ls]).astype(BF16)
                wt_scr[sub][ii * N_KEYS:(ii + 1) * N_KEYS, ls] = act * acc

    ht_scr = (ht0_scr, ht1_scr)
    wt_scr = (wt0_scr, wt1_scr)
    half = tm // 2
    cpl = half // LANE
    for sub in range(2):
        for hf in range(2):
            ln = slice(hf * half, (hf + 1) * half)
            ht_scr[1 - sub][:, ln] = jnp.dot(u_ref[sub * eb:(sub + 1) * eb, :], xt_ref[:, ln],
                                             preferred_element_type=F32)
            gate_block(sub, range(hf * cpl, (hf + 1) * cpl))
            y_scr[:, ln] += jnp.dot(vt_ref[:, sub * eb:(sub + 1) * eb], wt_scr[sub][:, ln],
                                    preferred_element_type=F32)

    @pl.when(g == ng - 1)
    def _finish():
        z = ALPHA * x1_ref[...] + y_scr[...].T
        x2 = _layer_norm(z, g_ref[...], b_ref[...])
        x2_ref[...] = x2
        x2b_ref[...] = x2.astype(BF16)


def _peer(x1t, wf, u, vt, x1, g, b, tm=512, eb=256):
    T = x1.shape[0]
    ne = u.shape[0]
    ncand = len(_n_cand())
    rows = N_KEYS * PEER_HEADS
    const = lambda t, k: (0, 0)
    slab = lambda n: pltpu.VMEM((n, PEER_HEADS, tm), F32)
    u_roll = jnp.roll(u, -eb, axis=0)
    return pl.pallas_call(
        _peer_kernel,
        grid=(T // tm, ne // (2 * eb)),
        in_specs=[pl.BlockSpec((D_MODEL, tm), lambda t, k: (0, t)),
                  pl.BlockSpec(wf.shape, const),
                  pl.BlockSpec((eb, D_MODEL), const),
                  pl.BlockSpec((2 * eb, D_MODEL), lambda t, k: (k, 0)),
                  pl.BlockSpec((D_MODEL, 2 * eb), lambda t, k: (0, k)),
                  pl.BlockSpec((tm, D_MODEL), lambda t, k: (t, 0)),
                  pl.BlockSpec((1, D_MODEL), const),
                  pl.BlockSpec((1, D_MODEL), const)],
        out_specs=[pl.BlockSpec((tm, D_MODEL), lambda t, k: (t, 0)),
                   pl.BlockSpec((tm, D_MODEL), lambda t, k: (t, 0))],
        out_shape=[jax.ShapeDtypeStruct((T, D_MODEL), F32), jax.ShapeDtypeStruct((T, D_MODEL), BF16)],
        scratch_shapes=[pltpu.VMEM((rows, tm), jnp.uint32), pltpu.VMEM((rows, tm), jnp.uint32),
                        pltpu.VMEM((rows, tm), BF16), pltpu.VMEM((rows, tm), BF16),
                        pltpu.VMEM((rows, tm), F32),
                        pltpu.VMEM((D_MODEL, tm), F32),
                        pltpu.VMEM((eb, tm), BF16), pltpu.VMEM((eb, tm), BF16),
                        pltpu.VMEM((eb, tm), F32), pltpu.VMEM((eb, tm), F32),
                        slab(N_KEYS), pltpu.VMEM((2, N_RANK, PEER_HEADS, tm), F32),
                        slab(ncand), slab(N_RANK), slab(N_RANK)],
        compiler_params=_cparams(("parallel", "arbitrary")),
        name="peer_ln2",
    )(x1t, wf, u, u_roll, vt, x1, g, b)


def _prep_w_in(w_in):
    sizes = (512, 512, 512, 8, 512, 128, 128, 512, 1024, 4096)
    offs = np.cumsum((0,) + sizes)
    q_f, k_f, v_f, f_l, q_s, k_s, v_s, x_pool, x_conv, gl = (w_in[:, offs[i]:offs[i + 1]] for i in range(10))
    scale = HEAD_DIM ** -0.5
    q_s = q_s.reshape(D_MODEL, 2, 4, HEAD_DIM).transpose(0, 2, 1, 3).reshape(D_MODEL, 512)
    zeros = lambda n: jnp.zeros((D_MODEL, n), w_in.dtype)
    w_h = jnp.concatenate([q_f * scale, k_f, v_f, q_s * scale, x_pool, k_s, v_s, f_l, zeros(120), zeros(128), x_conv],
                          axis=1)
    return w_h.astype(BF16), gl.astype(BF16)


def _prep_peer(wq, k1, k2):
    wq_t = wq.T.astype(BF16)
    keys = jnp.stack([k1, k2]).astype(BF16)
    wf = _peer_score_weights(wq_t, keys)
    wf = wf.reshape(PEER_HEADS, 2, N_KEYS, D_MODEL)
    kh = lambda half: wf[:, half].transpose(1, 0, 2).reshape(N_KEYS * PEER_HEADS, D_MODEL)
    hk = wf[:, 1].reshape(PEER_HEADS * N_KEYS, D_MODEL)
    return jnp.concatenate([kh(0), kh(1), hk], axis=0)


def _pad_lanes(v, n):
    return jnp.zeros((1, n), F32).at[0, :v.shape[0]].set(v.astype(F32))


def kernel(x, w_in, b_f, swa_sinks, pool_w, pool_scale, dw_w, dw_b, conv_ln_g, conv_ln_b, p_fox, p_swa, p_pool,
           p_conv, w_out, ln1_g, ln1_b, peer_wq, peer_k1, peer_k2, peer_u, peer_v, ln2_g, ln2_b):
    B, S, D = x.shape
    T = B * S
    xf = x.reshape(T, D)
    xb = xf.astype(BF16)
    row = lambda v: v.reshape(1, -1).astype(F32)
    for l in range(DEPTH):
        w_h, w_gate = _prep_w_in(w_in[l])
        h = _inproj(xb, w_h)
        aq, ak = _decay(h, _pad_lanes(b_f[l], LANE), B, S)
        y_fox = _fox(h, aq, ak, B, S)
        y_swa = _swa(h, swa_sinks[l].astype(F32), B, S)
        y_pool = _pool(h, pool_w[l].astype(BF16), row(pool_scale[l]), B, S)
        dw = jnp.zeros((32, CONV_W), F32).at[:CONV_K].set(dw_w[l])
        y_conv = _conv(h, dw, row(dw_b[l]), row(conv_ln_g[l]), row(conv_ln_b[l]), B, S)
        ps = p_swa[l].reshape(2, 4, HEAD_DIM, D).transpose(1, 0, 2, 3).reshape(512, D)
        x1, x1t = _merge(xb, xf, y_fox, y_pool, y_conv, y_swa, w_gate, p_fox[l].astype(BF16),
                         p_pool[l].astype(BF16), p_conv[l].astype(BF16), ps.astype(BF16),
                         w_out[l].astype(BF16), row(ln1_g[l]), row(ln1_b[l]))
        wf = _prep_peer(peer_wq[l], peer_k1[l], peer_k2[l])
        xf, xb = _peer(x1t, wf, peer_u[l].astype(BF16), peer_v[l].T.astype(BF16), x1,
                       row(ln2_g[l]), row(ln2_b[l]))
    return xf.reshape(B, S, D)
```

```python
import functools
import math

import numpy as np
import jax
import jax.numpy as jnp
from jax import lax
from jax.experimental import pallas as pl
from jax.experimental.pallas import tpu as pltpu

F32 = jnp.float32
BF16 = jnp.bfloat16

D_MODEL = 1024
DEPTH = 2
FOX_HEADS = 8
HEAD_DIM = 64
SWA_HEADS = 8
SWA_KV = 2
WINDOW = 128
POOL_WINDOWS = (2, 4, 8, 16)
POOL_GW = 128
POOL_W = 512
CONV_W = 512
CONV_K = 31
N_BRANCH = 4
PEER_HEADS = 8
N_KEYS = 128
N_EXPERTS = N_KEYS * N_KEYS
PEER_TOPK = 16
LN_EPS = 1e-5
ALPHA = (2 * DEPTH) ** 0.25
NEG_BIG = -1e30

COL_QF, COL_KF, COL_VF, COL_QS, COL_POOL, COL_KS, COL_VS, COL_FL, COL_CONV = (
    0, 512, 1024, 1536, 2048, 2560, 2688, 2816, 3072)
H_COLS = 4096
LANE = 128
AUG_A = 6
AUG_B = 12

VMEM_LIMIT = 56 * 1024 * 1024


def _cparams(sem, flags=None):
    return pltpu.CompilerParams(dimension_semantics=sem, vmem_limit_bytes=VMEM_LIMIT, flags=flags)


def _layer_norm(z, g, b):
    mu = jnp.mean(z, axis=-1, keepdims=True)
    zc = z - mu
    var = jnp.mean(zc * zc, axis=-1, keepdims=True)
    return zc * lax.rsqrt(var + LN_EPS) * g + b


def _sigmoid(z):
    return 1.0 / (1.0 + jnp.exp(-z))


def _inproj_kernel(x_ref, w_ref, o_ref, *, n_chunk):
    x = x_ref[...]
    for c in range(0, o_ref.shape[1], n_chunk):
        o_ref[:, c:c + n_chunk] = jnp.dot(
            x, w_ref[:, c:c + n_chunk], preferred_element_type=F32).astype(o_ref.dtype)


def _inproj(xb, w, tm=512):
    T, K = xb.shape
    N = w.shape[1]
    return pl.pallas_call(
        functools.partial(_inproj_kernel, n_chunk=1024),
        grid=(T // tm,),
        in_specs=[pl.BlockSpec((tm, K), lambda i: (i, 0)),
                  pl.BlockSpec((K, N), lambda i: (0, 0))],
        out_specs=pl.BlockSpec((tm, N), lambda i: (i, 0)),
        out_shape=jax.ShapeDtypeStruct((T, N), BF16),
        compiler_params=_cparams(("parallel",)),
        name="inproj",
    )(xb, w)


def _split3(v):
    hi = v.astype(BF16)
    r1 = v - hi.astype(F32)
    mid = r1.astype(BF16)
    r2 = r1 - mid.astype(F32)
    return hi, mid, r2.astype(BF16)


def _decay_kernel(fl_ref, bf_ref, selq_ref, selk_ref, cq_ref, ck_ref, aq_ref, ak_ref, carry_ref):
    ts = fl_ref.shape[0]

    @pl.when(pl.program_id(1) == 0)
    def _():
        carry_ref[...] = jnp.zeros_like(carry_ref)

    z = fl_ref[...].astype(F32) + bf_ref[...]
    ls = jnp.minimum(z, 0.0) - jnp.log1p(jnp.exp(-jnp.abs(z)))
    row = lax.broadcasted_iota(jnp.int32, (ts, ts), 0)
    col = lax.broadcasted_iota(jnp.int32, (ts, ts), 1)
    tri = jnp.where(col <= row, 1.0, 0.0).astype(BF16)
    parts = jnp.concatenate(_split3(ls), axis=1)
    cs = jnp.dot(tri, parts, preferred_element_type=F32)
    c = cs[:, :LANE] + cs[:, LANE:2 * LANE] + cs[:, 2 * LANE:] + carry_ref[0:1, :]
    carry_ref[...] = jnp.broadcast_to(c[ts - 1:ts, :], carry_ref.shape)
    cparts = jnp.concatenate(_split3(c), axis=1)
    aq_ref[...] = (jnp.dot(cparts, selq_ref[...], preferred_element_type=F32) + cq_ref[...]).astype(BF16)
    ak_ref[...] = (jnp.dot(cparts, selk_ref[...], preferred_element_type=F32) + ck_ref[...]).astype(BF16)


def _decay_consts():
    selq = np.zeros((3 * LANE, 4 * LANE), np.float32)
    selk = np.zeros((3 * LANE, 4 * LANE), np.float32)
    cq = np.zeros((1, 4 * LANE), np.float32)
    ck = np.zeros((1, 4 * LANE), np.float32)
    for p in range(4):
        for part in range(3):
            for hh in range(2):
                base = p * LANE + hh * AUG_A
                selq[part * LANE + 2 * p + hh, base + part] = 1.0
                cq[0, base + 3 + part] = 1.0
                ck[0, base + part] = 1.0
                selk[part * LANE + 2 * p + hh, base + 3 + part] = -1.0
    return (jnp.asarray(selq, BF16), jnp.asarray(selk, BF16), jnp.asarray(cq), jnp.asarray(ck))


def _decay(h, bf_pad, B, S, ts=512):
    T = B * S
    nt = S // ts
    selq, selk, cq, ck = _decay_consts()
    const = lambda b, j: (0, 0)
    return pl.pallas_call(
        _decay_kernel,
        grid=(B, nt),
        in_specs=[pl.BlockSpec((ts, LANE), lambda b, j: (b * nt + j, COL_FL // LANE)),
                  pl.BlockSpec((1, LANE), const),
                  pl.BlockSpec((3 * LANE, 4 * LANE), const),
                  pl.BlockSpec((3 * LANE, 4 * LANE), const),
                  pl.BlockSpec((1, 4 * LANE), const),
                  pl.BlockSpec((1, 4 * LANE), const)],
        out_specs=[pl.BlockSpec((ts, 4 * LANE), lambda b, j: (b * nt + j, 0)),
                   pl.BlockSpec((ts, 4 * LANE), lambda b, j: (b * nt + j, 0))],
        out_shape=[jax.ShapeDtypeStruct((T, 4 * LANE), BF16)] * 2,
        scratch_shapes=[pltpu.VMEM((8, LANE), F32)],
        compiler_params=_cparams(("parallel", "arbitrary")),
        name="fox_decay",
    )(h, bf_pad, selq, selk, cq, ck)


def _fox_kernel(q_ref, aq_ref, k_ref, ak_ref, v_ref, o_ref, m_scr, acc_scr):
    tq = q_ref.shape[0]
    qi = pl.program_id(2)
    lane2 = lax.broadcasted_iota(jnp.int32, (1, 2 * LANE), 1)
    head_mask = (
        (lane2 < HEAD_DIM) | ((lane2 >= LANE) & (lane2 < LANE + AUG_A)),
        ((lane2 >= HEAD_DIM) & (lane2 < LANE)) | ((lane2 >= LANE + AUG_A) & (lane2 < LANE + AUG_B)),
    )
    qf = jnp.concatenate([q_ref[...], aq_ref[...]], axis=1)
    qs = [jnp.where(mk, qf, jnp.zeros_like(qf)) for mk in head_mask]
    ones_col = jnp.where(lax.broadcasted_iota(jnp.int32, (tq, LANE), 1) == 0, 1.0, 0.0).astype(BF16)
    row = lax.broadcasted_iota(jnp.int32, (tq, tq), 0)
    col = lax.broadcasted_iota(jnp.int32, (tq, tq), 1)

    m_scr[...] = jnp.full(m_scr.shape, NEG_BIG, F32)
    acc_scr[...] = jnp.zeros(acc_scr.shape, F32)

    def step(j, masked):
        off = pl.multiple_of(j * tq, tq)
        kf = jnp.concatenate([k_ref[pl.ds(off, tq), :], ak_ref[pl.ds(off, tq), :]], axis=1)
        vf = jnp.concatenate([v_ref[pl.ds(off, tq), :], ones_col], axis=1)
        for x in range(2):
            s = lax.dot_general(qs[x], kf, (((1,), (1,)), ((), ())), preferred_element_type=F32)
            if masked:
                s = jnp.where(col <= row, s, NEG_BIG)
            m_prev = m_scr[x]
            m_new = jnp.maximum(m_prev, jnp.max(s, axis=1, keepdims=True))
            alpha = jnp.exp(m_prev - m_new)
            p = jnp.exp(s - jnp.concatenate([m_new] * (tq // LANE), axis=1))
            acc_scr[x] = (acc_scr[x] * jnp.concatenate([alpha, alpha], axis=1)
                          + jnp.dot(p.astype(BF16), vf, preferred_element_type=F32))
            m_scr[x] = m_new

    def body(j, carry):
        step(j, False)
        return carry

    lax.fori_loop(0, qi, body, 0)
    step(qi, True)

    outs = []
    for x in range(2):
        acc = acc_scr[x]
        outs.append(acc[:, :LANE] / acc[:, LANE:LANE + 1])
    lane = lax.broadcasted_iota(jnp.int32, (tq, LANE), 1)
    o_ref[...] = jnp.where(lane < HEAD_DIM, outs[0], outs[1]).astype(o_ref.dtype)


def _fox(h, aq, ak, B, S, tq=512):
    T = B * S
    nq = S // tq
    return pl.pallas_call(
        _fox_kernel,
        grid=(B, 4, nq),
        in_specs=[pl.BlockSpec((tq, LANE), lambda b, p, i: (b * nq + i, COL_QF // LANE + p)),
                  pl.BlockSpec((tq, LANE), lambda b, p, i: (b * nq + i, p)),
                  pl.BlockSpec((S, LANE), lambda b, p, i: (b, COL_KF // LANE + p)),
                  pl.BlockSpec((S, LANE), lambda b, p, i: (b, p)),
                  pl.BlockSpec((S, LANE), lambda b, p, i: (b, COL_VF // LANE + p))],
        out_specs=pl.BlockSpec((tq, LANE), lambda b, p, i: (b * nq + i, p)),
        out_shape=jax.ShapeDtypeStruct((T, 4 * LANE), BF16),
        scratch_shapes=[pltpu.VMEM((2, tq, LANE), F32), pltpu.VMEM((2, tq, 2 * LANE), F32)],
        compiler_params=_cparams(("parallel", "parallel", "arbitrary")),
        name="fox_attn",
    )(h, aq, h, ak, h)


def _swa_kernel(sink_ref, q_ref, kc_ref, kp_ref, vc_ref, vp_ref, o_ref):
    n = pl.program_id(1)
    blk = q_ref.shape[0]
    kb = jnp.concatenate([kp_ref[...], kc_ref[...]], axis=0)
    vb = jnp.concatenate([vp_ref[...], vc_ref[...]], axis=0)
    qi = lax.broadcasted_iota(jnp.int32, (blk, 2 * blk), 0)
    kj = lax.broadcasted_iota(jnp.int32, (blk, 2 * blk), 1)
    dist = qi + blk - kj
    valid = (dist >= 0) & (dist < WINDOW) & ((kj >= blk) | (n > 0))
    distf = dist.astype(F32)
    lane = lax.broadcasted_iota(jnp.int32, (blk, LANE), 1)
    lo = lane < HEAD_DIM
    for m in range(4):
        qm = q_ref[:, m * LANE:(m + 1) * LANE]
        outs = []
        for half in range(2):
            hd = m + 4 * half
            slope = 2.0 ** (-8.0 * (hd + 1) / SWA_HEADS)
            qh = jnp.where(lo if half == 0 else jnp.logical_not(lo), qm, jnp.zeros_like(qm))
            s = lax.dot_general(qh, kb, (((1,), (1,)), ((), ())), preferred_element_type=F32)
            s = jnp.where(valid, s - slope * distf, NEG_BIG)
            sink = sink_ref[hd]
            mx = jnp.maximum(jnp.max(s, axis=1, keepdims=True), sink)
            e = jnp.exp(s - mx)
            den = jnp.sum(e, axis=1, keepdims=True) + jnp.exp(sink - mx)
            p = (e / den).astype(BF16)
            outs.append(jnp.dot(p, vb, preferred_element_type=F32))
        o_ref[:, m * LANE:(m + 1) * LANE] = jnp.where(lo, outs[0], outs[1]).astype(o_ref.dtype)


def _swa(h, sinks, B, S, blk=128):
    T = B * S
    nb = S // blk
    cur = lambda c: (lambda b, n: (b * nb + n, c))
    prev = lambda c: (lambda b, n: (b * nb + jnp.maximum(n - 1, 0), c))
    return pl.pallas_call(
        _swa_kernel,
        grid=(B, nb),
        in_specs=[pl.BlockSpec(memory_space=pltpu.SMEM),
                  pl.BlockSpec((blk, 4 * LANE), cur(COL_QS // (4 * LANE))),
                  pl.BlockSpec((blk, LANE), cur(COL_KS // LANE)),
                  pl.BlockSpec((blk, LANE), prev(COL_KS // LANE)),
                  pl.BlockSpec((blk, LANE), cur(COL_VS // LANE)),
                  pl.BlockSpec((blk, LANE), prev(COL_VS // LANE))],
        out_specs=pl.BlockSpec((blk, 4 * LANE), lambda b, n: (b * nb + n, 0)),
        out_shape=jax.ShapeDtypeStruct((T, 4 * LANE), BF16),
        compiler_params=_cparams(("parallel", "arbitrary")),
        name="swa_attn",
    )(sinks, h, h, h, h, h)


def _pool_kernel(xc_ref, xp_ref, w_ref, sc_ref, o_ref):
    j = pl.program_id(1)
    ts = xc_ref.shape[0]
    hal = xp_ref.shape[0]
    r = lax.broadcasted_iota(jnp.int32, (ts, ts + hal), 0)
    c = lax.broadcasted_iota(jnp.int32, (ts, ts + hal), 1) - hal
    t_glob = (lax.broadcasted_iota(jnp.int32, (ts, LANE), 0) + j * ts + 1).astype(F32)
    has_prev = j > 0
    for g, w in enumerate(POOL_WINDOWS):
        xg = xc_ref[:, g * LANE:(g + 1) * LANE]
        xp = xp_ref[:, g * LANE:(g + 1) * LANE]
        xp = jnp.where(has_prev, xp, jnp.zeros_like(xp))
        ext = jnp.concatenate([xp, xg], axis=0)
        band = jnp.where((c <= r) & (c > r - w), 1.0, 0.0).astype(BF16)
        win = jnp.dot(band, ext, preferred_element_type=F32)
        cnt = jnp.minimum(t_glob, float(w))
        pooled = win / cnt - xg.astype(F32)
        y = jnp.dot(pooled.astype(BF16), w_ref[g], preferred_element_type=F32)
        o_ref[:, g * LANE:(g + 1) * LANE] = (y * sc_ref[:, g * LANE:(g + 1) * LANE]).astype(o_ref.dtype)


def _pool(h, pool_w, pool_scale, B, S, ts=512, hal=128):
    T = B * S
    nt = S // ts
    r = ts // hal
    return pl.pallas_call(
        _pool_kernel,
        grid=(B, nt),
        in_specs=[pl.BlockSpec((ts, POOL_W), lambda b, j: (b * nt + j, COL_POOL // POOL_W)),
                  pl.BlockSpec((hal, POOL_W),
                               lambda b, j: (jnp.maximum((b * nt + j) * r - 1, 0), COL_POOL // POOL_W)),
                  pl.BlockSpec((4, POOL_GW, POOL_GW), lambda b, j: (0, 0, 0)),
                  pl.BlockSpec((1, POOL_W), lambda b, j: (0, 0))],
        out_specs=pl.BlockSpec((ts, POOL_W), lambda b, j: (b * nt + j, 0)),
        out_shape=jax.ShapeDtypeStruct((T, POOL_W), BF16),
        compiler_params=_cparams(("parallel", "arbitrary")),
        name="ms_pool",
    )(h, h, pool_w, pool_scale)


def _conv_kernel(uc_ref, up_ref, w_ref, b_ref, g_ref, bb_ref, o_ref, ext_ref):
    j = pl.program_id(1)
    ts = uc_ref.shape[0]
    pad = ext_ref.shape[0] - ts

    def glu(u):
        u = u.astype(F32)
        return u[:, :CONV_W] * _sigmoid(u[:, CONV_W:])

    hp = glu(up_ref[up_ref.shape[0] - pad:, :])
    ext_ref[0:pad, :] = jnp.where(j > 0, hp, jnp.zeros_like(hp))
    ext_ref[pad:, :] = glu(uc_ref[...])
    acc = jnp.zeros((ts, CONV_W), F32)
    for k in range(CONV_K):
        off = pad - (CONV_K - 1) + k
        acc = acc + ext_ref[off:off + ts, :] * w_ref[k:k + 1, :]
    y = _layer_norm(acc + b_ref[...], g_ref[...], bb_ref[...])
    o_ref[...] = (y * _sigmoid(y)).astype(o_ref.dtype)


def _conv(h, dw_w, dw_b, ln_g, ln_b, B, S, ts=512, hal=128):
    T = B * S
    nt = S // ts
    r = ts // hal
    vec = pl.BlockSpec((1, CONV_W), lambda b, j: (0, 0))
    return pl.pallas_call(
        _conv_kernel,
        grid=(B, nt),
        in_specs=[pl.BlockSpec((ts, 2 * CONV_W), lambda b, j: (b * nt + j, COL_CONV // (2 * CONV_W))),
                  pl.BlockSpec((hal, 2 * CONV_W),
                               lambda b, j: (jnp.maximum((b * nt + j) * r - 1, 0), COL_CONV // (2 * CONV_W))),
                  pl.BlockSpec((32, CONV_W), lambda b, j: (0, 0)),
                  vec, vec, vec],
        out_specs=pl.BlockSpec((ts, CONV_W), lambda b, j: (b * nt + j, 0)),
        out_shape=jax.ShapeDtypeStruct((T, CONV_W), BF16),
        scratch_shapes=[pltpu.VMEM((ts + 32, CONV_W), F32)],
        compiler_params=_cparams(("parallel", "arbitrary")),
        name="conf_conv",
    )(h, h, dw_w, dw_b, ln_g, ln_b)


def _merge_kernel(xb_ref, x_ref, yf_ref, yp_ref, yc_ref, ys_ref, wg_ref, pf_ref, pp_ref, pc_ref, ps_ref,
                  wo_ref, g_ref, b_ref, x1_ref, x1t_ref):
    xb = xb_ref[...]
    merged = None
    for br, (y_ref, p_ref) in enumerate(((yf_ref, pf_ref), (yp_ref, pp_ref), (yc_ref, pc_ref), (ys_ref, ps_ref))):
        gate = _sigmoid(jnp.dot(xb, wg_ref[:, br * D_MODEL:(br + 1) * D_MODEL], preferred_element_type=F32))
        term = gate * jnp.dot(y_ref[...], p_ref[...], preferred_element_type=F32)
        merged = term if merged is None else merged + term
    mix = jnp.dot(merged.astype(BF16), wo_ref[...], preferred_element_type=F32)
    x1 = _layer_norm(ALPHA * x_ref[...] + mix, g_ref[...], b_ref[...])
    x1_ref[...] = x1
    x1t_ref[...] = x1.T.astype(BF16)


def _merge(xb, x, yf, yp, yc, ys, wg, pf, pp, pc, ps, wo, g, b, tm=256):
    T = xb.shape[0]
    const = lambda i: (0, 0)
    tok = lambda w: pl.BlockSpec((tm, w), lambda i: (i, 0))
    wspec = lambda a: pl.BlockSpec(a.shape, const)
    return pl.pallas_call(
        _merge_kernel,
        grid=(T // tm,),
        in_specs=[tok(D_MODEL), tok(D_MODEL), tok(512), tok(512), tok(512), tok(512),
                  wspec(wg), wspec(pf), wspec(pp), wspec(pc), wspec(ps), wspec(wo), wspec(g), wspec(b)],
        out_specs=[tok(D_MODEL), pl.BlockSpec((D_MODEL, tm), lambda i: (0, i))],
        out_shape=[jax.ShapeDtypeStruct((T, D_MODEL), F32), jax.ShapeDtypeStruct((D_MODEL, T), BF16)],
        compiler_params=_cparams(("parallel",)),
        name="merge_ln1",
    )(xb, x, yf, yp, yc, ys, wg, pf, pp, pc, ps, wo, g, b)


def _wf_kernel(k_ref, wq_ref, o_ref):
    o_ref[...] = jnp.dot(k_ref[0], wq_ref[...], preferred_element_type=F32).astype(o_ref.dtype)


def _peer_score_weights(wq_t, keys):
    nblk = wq_t.shape[0] // N_KEYS
    return pl.pallas_call(
        _wf_kernel,
        grid=(nblk,),
        in_specs=[pl.BlockSpec((1, N_KEYS, N_KEYS), lambda j: (j % 2, 0, 0)),
                  pl.BlockSpec((N_KEYS, D_MODEL), lambda j: (j, 0))],
        out_specs=pl.BlockSpec((N_KEYS, D_MODEL), lambda j: (j, 0)),
        out_shape=jax.ShapeDtypeStruct(wq_t.shape, BF16),
        compiler_params=_cparams(("parallel",)),
        name="peer_wf",
    )(keys, wq_t)


N_RANK = PEER_TOPK + 1


def _n_cand():
    return [(r, c) for r in range(N_RANK) for c in range(N_RANK) if (r + 1) * (c + 1) <= N_RANK]


def _gelu(z):
    return 0.5 * z * (1.0 + lax.erf(z * (1.0 / math.sqrt(2.0))))


def _extract_top(work_ref, out_ref, n_slab, n_out):
    def round_(r, prev):
        m = jnp.full(prev.shape, NEG_BIG, F32)
        for s in range(n_slab):
            w = work_ref[s]
            m = jnp.maximum(m, jnp.where(w < prev, w, NEG_BIG))
        out_ref[r] = m
        return m
    lax.fori_loop(0, n_out, round_, jnp.full(work_ref.shape[1:], -NEG_BIG, F32))


def _dup_bf16(v):
    u = pltpu.bitcast(v.astype(BF16).astype(F32), jnp.uint32)
    return u | (u >> 16)


def _row_bf16(slab, hh):
    return pltpu.bitcast(jnp.broadcast_to(slab[hh:hh + 1, :], (N_KEYS // 2, LANE)), BF16)


def _peer_kernel(xt_ref, wf_ref, u0_ref, u_ref, vt_ref, x1_ref, g_ref, b_ref, x2_ref, x2b_ref,
                 n_scr, e1_scr, rank_scr, e2_scr, s2_scr, y_scr, wt0_scr, wt1_scr, ht0_scr, ht1_scr,
                 work_scr, top_scr, cand_scr, csel_scr, thr_scr):
    g = pl.program_id(1)
    ng = pl.num_programs(1)
    tm = xt_ref.shape[1]
    eb = u_ref.shape[0] // 2
    nh = PEER_HEADS
    rows = N_KEYS * nh

    @pl.when(g == 0)
    def _select():
        y_scr[...] = jnp.zeros_like(y_scr)
        ht0_scr[...] = jnp.dot(u0_ref[...], xt_ref[...], preferred_element_type=F32)
        st = jnp.dot(wf_ref[...], xt_ref[...], preferred_element_type=F32)
        for half in range(2):
            work_scr[...] = st[half * rows:(half + 1) * rows].reshape(N_KEYS, nh, tm)
            _extract_top(work_scr, top_scr.at[half], N_KEYS, N_RANK)
        cands = _n_cand()
        for ci, (r, c) in enumerate(cands):
            cand_scr[ci] = top_scr[0, r] + top_scr[1, c]
        _extract_top(cand_scr, csel_scr, len(cands), N_RANK)
        m0 = csel_scr[0]
        zsum = jnp.zeros_like(m0)
        for r in range(PEER_TOPK):
            zsum = zsum + jnp.exp(csel_scr[r] - m0)
        tau = 0.5 * (csel_scr[PEER_TOPK - 1] + csel_scr[PEER_TOPK])
        a0 = top_scr[0, 0]
        for c in range(PEER_TOPK):
            thr_scr[c] = tau - top_scr[1, c]
        work_scr[...] = st[0:rows].reshape(N_KEYS, nh, tm)

        def key_body(i, carry):
            s = work_scr[i]
            cnt = jnp.zeros_like(s)
            for c in range(PEER_TOPK):
                cnt = cnt + jnp.where(s > thr_scr[c], 1.0, 0.0)
            r0 = pl.multiple_of(i * nh, nh)
            n_scr[pl.ds(r0, nh), :] = _dup_bf16(cnt)
            e1_scr[pl.ds(r0, nh), :] = _dup_bf16(jnp.exp(s - a0) / zsum)
            return carry
        lax.fori_loop(0, N_KEYS, key_body, 0)

        s2_scr[...] = st[2 * rows:3 * rows]
        grp = 16
        for hh in range(nh):
            brow = [top_scr[1, c][hh:hh + 1, :] for c in range(N_RANK)]
            mid = [0.5 * (brow[c] + brow[c + 1]) for c in range(PEER_TOPK)]

            def rank_body(jg, carry):
                r0 = pl.multiple_of(hh * N_KEYS + jg * grp, grp)
                s = s2_scr[pl.ds(r0, grp), :]
                cnt = jnp.zeros_like(s)
                for c in range(PEER_TOPK):
                    cnt = cnt + jnp.where(s < mid[c], 1.0, 0.0)
                rank_scr[pl.ds(r0, grp), :] = cnt.astype(BF16)
                e2_scr[pl.ds(r0, grp), :] = jnp.exp(s - brow[0]).astype(BF16)
                return carry
            lax.fori_loop(0, N_KEYS // grp, rank_body, 0)

    nsub = eb // N_KEYS

    def gate_block(sub, chunks):
        for ii in range(nsub):
            base = pl.multiple_of(((g * 2 + sub) * nsub + ii) * nh, nh)
            for c in chunks:
                ls = slice(c * LANE, (c + 1) * LANE)
                n_i = n_scr[pl.ds(base, nh), ls]
                e_i = e1_scr[pl.ds(base, nh), ls]
                acc = None
                for hh in range(nh):
                    rs = slice(hh * N_KEYS, (hh + 1) * N_KEYS)
                    term = jnp.where(rank_scr[rs, ls] < _row_bf16(n_i, hh), e2_scr[rs, ls],
                                     jnp.zeros((), BF16)) * _row_bf16(e_i, hh)
                    acc = term if acc is None else acc + term
                act = _gelu(ht_scr[sub][ii * N_KEYS:(ii + 1) * N_KEYS, ls]).astype(BF16)
                wt_scr[sub][ii * N_KEYS:(ii + 1) * N_KEYS, ls] = act * acc

    ht_scr = (ht0_scr, ht1_scr)
    wt_scr = (wt0_scr, wt1_scr)
    half = tm // 2
    cpl = half // LANE
    for sub in range(2):
        for hf in range(2):
            ln = slice(hf * half, (hf + 1) * half)
            ht_scr[1 - sub][:, ln] = jnp.dot(u_ref[sub * eb:(sub + 1) * eb, :], xt_ref[:, ln],
                                             preferred_element_type=F32)
            gate_block(sub, range(hf * cpl, (hf + 1) * cpl))
            y_scr[:, ln] += jnp.dot(vt_ref[:, sub * eb:(sub + 1) * eb], wt_scr[sub][:, ln],
                                    preferred_element_type=F32)

    @pl.when(g == ng - 1)
    def _finish():
        z = ALPHA * x1_ref[...] + y_scr[...].T
        x2 = _layer_norm(z, g_ref[...], b_ref[...])
        x2_ref[...] = x2
        x2b_ref[...] = x2.astype(BF16)


def _peer(x1t, wf, u, v, x1, g, b, tm=512, eb=256):
    T = x1.shape[0]
    ne = u.shape[0]
    ncand = len(_n_cand())
    rows = N_KEYS * PEER_HEADS
    const = lambda t, k: (0, 0)
    slab = lambda n: pltpu.VMEM((n, PEER_HEADS, tm), F32)
    u_roll = jnp.roll(u, -eb, axis=0)
    vt = v.reshape(ne // (2 * eb), 2 * eb, D_MODEL).transpose(0, 2, 1)
    return pl.pallas_call(
        _peer_kernel,
        grid=(T // tm, ne // (2 * eb)),
        in_specs=[pl.BlockSpec((D_MODEL, tm), lambda t, k: (0, t)),
                  pl.BlockSpec(wf.shape, const),
                  pl.BlockSpec((eb, D_MODEL), const),
                  pl.BlockSpec((2 * eb, D_MODEL), lambda t, k: (k, 0)),
                  pl.BlockSpec((None, D_MODEL, 2 * eb), lambda t, k: (k, 0, 0)),
                  pl.BlockSpec((tm, D_MODEL), lambda t, k: (t, 0)),
                  pl.BlockSpec((1, D_MODEL), const),
                  pl.BlockSpec((1, D_MODEL), const)],
        out_specs=[pl.BlockSpec((tm, D_MODEL), lambda t, k: (t, 0)),
                   pl.BlockSpec((tm, D_MODEL), lambda t, k: (t, 0))],
        out_shape=[jax.ShapeDtypeStruct((T, D_MODEL), F32), jax.ShapeDtypeStruct((T, D_MODEL), BF16)],
        scratch_shapes=[pltpu.VMEM((rows, tm), jnp.uint32), pltpu.VMEM((rows, tm), jnp.uint32),
                        pltpu.VMEM((rows, tm), BF16), pltpu.VMEM((rows, tm), BF16),
                        pltpu.VMEM((rows, tm), F32),
                        pltpu.VMEM((D_MODEL, tm), F32),
                        pltpu.VMEM((eb, tm), BF16), pltpu.VMEM((eb, tm), BF16),
                        pltpu.VMEM((eb, tm), F32), pltpu.VMEM((eb, tm), F32),
                        slab(N_KEYS), pltpu.VMEM((2, N_RANK, PEER_HEADS, tm), F32),
                        slab(ncand), slab(N_RANK), slab(N_RANK)],
        compiler_params=_cparams(("parallel", "arbitrary")),
        name="peer_ln2",
    )(x1t, wf, u, u_roll, vt, x1, g, b)


def _prep_w_in(w_in):
    sizes = (512, 512, 512, 8, 512, 128, 128, 512, 1024, 4096)
    offs = np.cumsum((0,) + sizes)
    q_f, k_f, v_f, f_l, q_s, k_s, v_s, x_pool, x_conv, gl = (w_in[:, offs[i]:offs[i + 1]] for i in range(10))
    scale = HEAD_DIM ** -0.5
    q_s = q_s.reshape(D_MODEL, 2, 4, HEAD_DIM).transpose(0, 2, 1, 3).reshape(D_MODEL, 512)
    zeros = lambda n: jnp.zeros((D_MODEL, n), w_in.dtype)
    w_h = jnp.concatenate([q_f * scale, k_f, v_f, q_s * scale, x_pool, k_s, v_s, f_l, zeros(120), zeros(128), x_conv],
                          axis=1)
    return w_h.astype(BF16), gl.astype(BF16)


def _prep_peer(wq, k1, k2):
    wq_t = wq.T.astype(BF16)
    keys = jnp.stack([k1, k2]).astype(BF16)
    wf = _peer_score_weights(wq_t, keys)
    wf = wf.reshape(PEER_HEADS, 2, N_KEYS, D_MODEL)
    kh = lambda half: wf[:, half].transpose(1, 0, 2).reshape(N_KEYS * PEER_HEADS, D_MODEL)
    hk = wf[:, 1].reshape(PEER_HEADS * N_KEYS, D_MODEL)
    return jnp.concatenate([kh(0), kh(1), hk], axis=0)


def _pad_lanes(v, n):
    return jnp.zeros((1, n), F32).at[0, :v.shape[0]].set(v.astype(F32))


def kernel(x, w_in, b_f, swa_sinks, pool_w, pool_scale, dw_w, dw_b, conv_ln_g, conv_ln_b, p_fox, p_swa, p_pool,
           p_conv, w_out, ln1_g, ln1_b, peer_wq, peer_k1, peer_k2, peer_u, peer_v, ln2_g, ln2_b):
    B, S, D = x.shape
    T = B * S
    xf = x.reshape(T, D)
    xb = xf.astype(BF16)
    row = lambda v: v.reshape(1, -1).astype(F32)
    for l in range(DEPTH):
        w_h, w_gate = _prep_w_in(w_in[l])
        h = _inproj(xb, w_h)
        aq, ak = _decay(h, _pad_lanes(b_f[l], LANE), B, S)
        y_fox = _fox(h, aq, ak, B, S)
        y_swa = _swa(h, swa_sinks[l].astype(F32), B, S)
        y_pool = _pool(h, pool_w[l].astype(BF16), row(pool_scale[l]), B, S)
        dw = jnp.zeros((32, CONV_W), F32).at[:CONV_K].set(dw_w[l])
        y_conv = _conv(h, dw, row(dw_b[l]), row(conv_ln_g[l]), row(conv_ln_b[l]), B, S)
        ps = p_swa[l].reshape(2, 4, HEAD_DIM, D).transpose(1, 0, 2, 3).reshape(512, D)
        x1, x1t = _merge(xb, xf, y_fox, y_pool, y_conv, y_swa, w_gate, p_fox[l].astype(BF16),
                         p_pool[l].astype(BF16), p_conv[l].astype(BF16), ps.astype(BF16),
                         w_out[l].astype(BF16), row(ln1_g[l]), row(ln1_b[l]))
        wf = _prep_peer(peer_wq[l], peer_k1[l], peer_k2[l])
        xf, xb = _peer(x1t, wf, peer_u[l].astype(BF16), peer_v[l].astype(BF16), x1,
                       row(ln2_g[l]), row(ln2_b[l]))
    return xf.reshape(B, S, D)
```

```python
import functools
import math

import numpy as np
import jax
import jax.numpy as jnp
from jax import lax
from jax.experimental import pallas as pl
from jax.experimental.pallas import tpu as pltpu

F32 = jnp.float32
BF16 = jnp.bfloat16

D_MODEL = 1024
DEPTH = 2
FOX_HEADS = 8
HEAD_DIM = 64
SWA_HEADS = 8
SWA_KV = 2
WINDOW = 128
POOL_WINDOWS = (2, 4, 8, 16)
POOL_GW = 128
POOL_W = 512
CONV_W = 512
CONV_K = 31
N_BRANCH = 4
PEER_HEADS = 8
N_KEYS = 128
N_EXPERTS = N_KEYS * N_KEYS
PEER_TOPK = 16
LN_EPS = 1e-5
ALPHA = (2 * DEPTH) ** 0.25
NEG_BIG = -1e30

COL_QF, COL_KF, COL_VF, COL_QS, COL_POOL, COL_KS, COL_VS, COL_FL, COL_CONV = (
    0, 512, 1024, 1536, 2048, 2560, 2688, 2816, 3072)
H_COLS = 4096
LANE = 128
AUG_A = 6
AUG_B = 12

VMEM_LIMIT = 56 * 1024 * 1024


def _cparams(sem):
    return pltpu.CompilerParams(dimension_semantics=sem, vmem_limit_bytes=VMEM_LIMIT)


def _layer_norm(z, g, b):
    mu = jnp.mean(z, axis=-1, keepdims=True)
    zc = z - mu
    var = jnp.mean(zc * zc, axis=-1, keepdims=True)
    return zc * lax.rsqrt(var + LN_EPS) * g + b


def _sigmoid(z):
    return 1.0 / (1.0 + jnp.exp(-z))


def _inproj_kernel(x_ref, w_ref, o_ref, *, n_chunk):
    x = x_ref[...]
    for c in range(0, o_ref.shape[1], n_chunk):
        o_ref[:, c:c + n_chunk] = jnp.dot(
            x, w_ref[:, c:c + n_chunk], preferred_element_type=F32).astype(o_ref.dtype)


def _inproj(xb, w, tm=512):
    T, K = xb.shape
    N = w.shape[1]
    return pl.pallas_call(
        functools.partial(_inproj_kernel, n_chunk=1024),
        grid=(T // tm,),
        in_specs=[pl.BlockSpec((tm, K), lambda i: (i, 0)),
                  pl.BlockSpec((K, N), lambda i: (0, 0))],
        out_specs=pl.BlockSpec((tm, N), lambda i: (i, 0)),
        out_shape=jax.ShapeDtypeStruct((T, N), BF16),
        compiler_params=_cparams(("parallel",)),
        name="inproj",
    )(xb, w)


def _split3(v):
    hi = v.astype(BF16)
    r1 = v - hi.astype(F32)
    mid = r1.astype(BF16)
    r2 = r1 - mid.astype(F32)
    return hi, mid, r2.astype(BF16)


def _decay_kernel(fl_ref, bf_ref, selq_ref, selk_ref, cq_ref, ck_ref, aq_ref, ak_ref, carry_ref):
    ts = fl_ref.shape[0]

    @pl.when(pl.program_id(1) == 0)
    def _():
        carry_ref[...] = jnp.zeros_like(carry_ref)

    z = fl_ref[...].astype(F32) + bf_ref[...]
    ls = jnp.minimum(z, 0.0) - jnp.log1p(jnp.exp(-jnp.abs(z)))
    row = lax.broadcasted_iota(jnp.int32, (ts, ts), 0)
    col = lax.broadcasted_iota(jnp.int32, (ts, ts), 1)
    tri = jnp.where(col <= row, 1.0, 0.0).astype(BF16)
    parts = jnp.concatenate(_split3(ls), axis=1)
    cs = jnp.dot(tri, parts, preferred_element_type=F32)
    c = cs[:, :LANE] + cs[:, LANE:2 * LANE] + cs[:, 2 * LANE:] + carry_ref[0:1, :]
    carry_ref[...] = jnp.broadcast_to(c[ts - 1:ts, :], carry_ref.shape)
    cparts = jnp.concatenate(_split3(c), axis=1)
    aq_ref[...] = (jnp.dot(cparts, selq_ref[...], preferred_element_type=F32) + cq_ref[...]).astype(BF16)
    ak_ref[...] = (jnp.dot(cparts, selk_ref[...], preferred_element_type=F32) + ck_ref[...]).astype(BF16)


def _decay_consts():
    selq = np.zeros((3 * LANE, 4 * LANE), np.float32)
    selk = np.zeros((3 * LANE, 4 * LANE), np.float32)
    cq = np.zeros((1, 4 * LANE), np.float32)
    ck = np.zeros((1, 4 * LANE), np.float32)
    for p in range(4):
        for part in range(3):
            for hh in range(2):
                base = p * LANE + hh * AUG_A
                selq[part * LANE + 2 * p + hh, base + part] = 1.0
                cq[0, base + 3 + part] = 1.0
                ck[0, base + part] = 1.0
                selk[part * LANE + 2 * p + hh, base + 3 + part] = -1.0
    return (jnp.asarray(selq, BF16), jnp.asarray(selk, BF16), jnp.asarray(cq), jnp.asarray(ck))


def _decay(h, bf_pad, B, S, ts=512):
    T = B * S
    nt = S // ts
    selq, selk, cq, ck = _decay_consts()
    const = lambda b, j: (0, 0)
    return pl.pallas_call(
        _decay_kernel,
        grid=(B, nt),
        in_specs=[pl.BlockSpec((ts, LANE), lambda b, j: (b * nt + j, COL_FL // LANE)),
                  pl.BlockSpec((1, LANE), const),
                  pl.BlockSpec((3 * LANE, 4 * LANE), const),
                  pl.BlockSpec((3 * LANE, 4 * LANE), const),
                  pl.BlockSpec((1, 4 * LANE), const),
                  pl.BlockSpec((1, 4 * LANE), const)],
        out_specs=[pl.BlockSpec((ts, 4 * LANE), lambda b, j: (b * nt + j, 0)),
                   pl.BlockSpec((ts, 4 * LANE), lambda b, j: (b * nt + j, 0))],
        out_shape=[jax.ShapeDtypeStruct((T, 4 * LANE), BF16)] * 2,
        scratch_shapes=[pltpu.VMEM((8, LANE), F32)],
        compiler_params=_cparams(("parallel", "arbitrary")),
        name="fox_decay",
    )(h, bf_pad, selq, selk, cq, ck)


def _fox_kernel(q_ref, aq_ref, k_ref, ak_ref, v_ref, o_ref, m_scr, acc_scr):
    tq = q_ref.shape[0]
    qi = pl.program_id(2)
    lane2 = lax.broadcasted_iota(jnp.int32, (1, 2 * LANE), 1)
    head_mask = (
        (lane2 < HEAD_DIM) | ((lane2 >= LANE) & (lane2 < LANE + AUG_A)),
        ((lane2 >= HEAD_DIM) & (lane2 < LANE)) | ((lane2 >= LANE + AUG_A) & (lane2 < LANE + AUG_B)),
    )
    qf = jnp.concatenate([q_ref[...], aq_ref[...]], axis=1)
    qs = [jnp.where(mk, qf, jnp.zeros_like(qf)) for mk in head_mask]
    ones_col = jnp.where(lax.broadcasted_iota(jnp.int32, (tq, LANE), 1) == 0, 1.0, 0.0).astype(BF16)
    row = lax.broadcasted_iota(jnp.int32, (tq, tq), 0)
    col = lax.broadcasted_iota(jnp.int32, (tq, tq), 1)

    m_scr[...] = jnp.full(m_scr.shape, NEG_BIG, F32)
    acc_scr[...] = jnp.zeros(acc_scr.shape, F32)

    def step(j, masked):
        off = pl.multiple_of(j * tq, tq)
        kf = jnp.concatenate([k_ref[pl.ds(off, tq), :], ak_ref[pl.ds(off, tq), :]], axis=1)
        vf = jnp.concatenate([v_ref[pl.ds(off, tq), :], ones_col], axis=1)
        for x in range(2):
            s = lax.dot_general(qs[x], kf, (((1,), (1,)), ((), ())), preferred_element_type=F32)
            if masked:
                s = jnp.where(col <= row, s, NEG_BIG)
            m_prev = m_scr[x]
            m_new = jnp.maximum(m_prev, jnp.max(s, axis=1, keepdims=True))
            alpha = jnp.exp(m_prev - m_new)
            p = jnp.exp(s - jnp.concatenate([m_new] * (tq // LANE), axis=1))
            acc_scr[x] = (acc_scr[x] * jnp.concatenate([alpha, alpha], axis=1)
                          + jnp.dot(p.astype(BF16), vf, preferred_element_type=F32))
            m_scr[x] = m_new

    def body(j, carry):
        step(j, False)
        return carry

    lax.fori_loop(0, qi, body, 0)
    step(qi, True)

    outs = []
    for x in range(2):
        acc = acc_scr[x]
        outs.append(acc[:, :LANE] / acc[:, LANE:LANE + 1])
    lane = lax.broadcasted_iota(jnp.int32, (tq, LANE), 1)
    o_ref[...] = jnp.where(lane < HEAD_DIM, outs[0], outs[1]).astype(o_ref.dtype)


def _fox(h, aq, ak, B, S, tq=512):
    T = B * S
    nq = S // tq
    return pl.pallas_call(
        _fox_kernel,
        grid=(B, 4, nq),
        in_specs=[pl.BlockSpec((tq, LANE), lambda b, p, i: (b * nq + i, COL_QF // LANE + p)),
                  pl.BlockSpec((tq, LANE), lambda b, p, i: (b * nq + i, p)),
                  pl.BlockSpec((S, LANE), lambda b, p, i: (b, COL_KF // LANE + p)),
                  pl.BlockSpec((S, LANE), lambda b, p, i: (b, p)),
                  pl.BlockSpec((S, LANE), lambda b, p, i: (b, COL_VF // LANE + p))],
        out_specs=pl.BlockSpec((tq, LANE), lambda b, p, i: (b * nq + i, p)),
        out_shape=jax.ShapeDtypeStruct((T, 4 * LANE), BF16),
        scratch_shapes=[pltpu.VMEM((2, tq, LANE), F32), pltpu.VMEM((2, tq, 2 * LANE), F32)],
        compiler_params=_cparams(("parallel", "parallel", "arbitrary")),
        name="fox_attn",
    )(h, aq, h, ak, h)


def _swa_kernel(sink_ref, q_ref, kc_ref, kp_ref, vc_ref, vp_ref, o_ref):
    n = pl.program_id(1)
    blk = q_ref.shape[0]
    kb = jnp.concatenate([kp_ref[...], kc_ref[...]], axis=0)
    vb = jnp.concatenate([vp_ref[...], vc_ref[...]], axis=0)
    qi = lax.broadcasted_iota(jnp.int32, (blk, 2 * blk), 0)
    kj = lax.broadcasted_iota(jnp.int32, (blk, 2 * blk), 1)
    dist = qi + blk - kj
    valid = (dist >= 0) & (dist < WINDOW) & ((kj >= blk) | (n > 0))
    distf = dist.astype(F32)
    lane = lax.broadcasted_iota(jnp.int32, (blk, LANE), 1)
    lo = lane < HEAD_DIM
    for m in range(4):
        qm = q_ref[:, m * LANE:(m + 1) * LANE]
        outs = []
        for half in range(2):
            hd = m + 4 * half
            slope = 2.0 ** (-8.0 * (hd + 1) / SWA_HEADS)
            qh = jnp.where(lo if half == 0 else jnp.logical_not(lo), qm, jnp.zeros_like(qm))
            s = lax.dot_general(qh, kb, (((1,), (1,)), ((), ())), preferred_element_type=F32)
            s = jnp.where(valid, s - slope * distf, NEG_BIG)
            sink = sink_ref[hd]
            mx = jnp.maximum(jnp.max(s, axis=1, keepdims=True), sink)
            e = jnp.exp(s - mx)
            den = jnp.sum(e, axis=1, keepdims=True) + jnp.exp(sink - mx)
            p = (e / den).astype(BF16)
            outs.append(jnp.dot(p, vb, preferred_element_type=F32))
        o_ref[:, m * LANE:(m + 1) * LANE] = jnp.where(lo, outs[0], outs[1]).astype(o_ref.dtype)


def _swa(h, sinks, B, S, blk=128):
    T = B * S
    nb = S // blk
    cur = lambda c: (lambda b, n: (b * nb + n, c))
    prev = lambda c: (lambda b, n: (b * nb + jnp.maximum(n - 1, 0), c))
    return pl.pallas_call(
        _swa_kernel,
        grid=(B, nb),
        in_specs=[pl.BlockSpec(memory_space=pltpu.SMEM),
                  pl.BlockSpec((blk, 4 * LANE), cur(COL_QS // (4 * LANE))),
                  pl.BlockSpec((blk, LANE), cur(COL_KS // LANE)),
                  pl.BlockSpec((blk, LANE), prev(COL_KS // LANE)),
                  pl.BlockSpec((blk, LANE), cur(COL_VS // LANE)),
                  pl.BlockSpec((blk, LANE), prev(COL_VS // LANE))],
        out_specs=pl.BlockSpec((blk, 4 * LANE), lambda b, n: (b * nb + n, 0)),
        out_shape=jax.ShapeDtypeStruct((T, 4 * LANE), BF16),
        compiler_params=_cparams(("parallel", "arbitrary")),
        name="swa_attn",
    )(sinks, h, h, h, h, h)


def _pool_kernel(xc_ref, xp_ref, w_ref, sc_ref, o_ref):
    j = pl.program_id(1)
    ts = xc_ref.shape[0]
    hal = xp_ref.shape[0]
    r = lax.broadcasted_iota(jnp.int32, (ts, ts + hal), 0)
    c = lax.broadcasted_iota(jnp.int32, (ts, ts + hal), 1) - hal
    t_glob = (lax.broadcasted_iota(jnp.int32, (ts, LANE), 0) + j * ts + 1).astype(F32)
    has_prev = j > 0
    for g, w in enumerate(POOL_WINDOWS):
        xg = xc_ref[:, g * LANE:(g + 1) * LANE]
        xp = xp_ref[:, g * LANE:(g + 1) * LANE]
        xp = jnp.where(has_prev, xp, jnp.zeros_like(xp))
        ext = jnp.concatenate([xp, xg], axis=0)
        band = jnp.where((c <= r) & (c > r - w), 1.0, 0.0).astype(BF16)
        win = jnp.dot(band, ext, preferred_element_type=F32)
        cnt = jnp.minimum(t_glob, float(w))
        pooled = win / cnt - xg.astype(F32)
        y = jnp.dot(pooled.astype(BF16), w_ref[g], preferred_element_type=F32)
        o_ref[:, g * LANE:(g + 1) * LANE] = (y * sc_ref[:, g * LANE:(g + 1) * LANE]).astype(o_ref.dtype)


def _pool(h, pool_w, pool_scale, B, S, ts=512, hal=128):
    T = B * S
    nt = S // ts
    r = ts // hal
    return pl.pallas_call(
        _pool_kernel,
        grid=(B, nt),
        in_specs=[pl.BlockSpec((ts, POOL_W), lambda b, j: (b * nt + j, COL_POOL // POOL_W)),
                  pl.BlockSpec((hal, POOL_W),
                               lambda b, j: (jnp.maximum((b * nt + j) * r - 1, 0), COL_POOL // POOL_W)),
                  pl.BlockSpec((4, POOL_GW, POOL_GW), lambda b, j: (0, 0, 0)),
                  pl.BlockSpec((1, POOL_W), lambda b, j: (0, 0))],
        out_specs=pl.BlockSpec((ts, POOL_W), lambda b, j: (b * nt + j, 0)),
        out_shape=jax.ShapeDtypeStruct((T, POOL_W), BF16),
        compiler_params=_cparams(("parallel", "arbitrary")),
        name="ms_pool",
    )(h, h, pool_w, pool_scale)


def _conv_kernel(uc_ref, up_ref, w_ref, b_ref, g_ref, bb_ref, o_ref, ext_ref):
    j = pl.program_id(1)
    ts = uc_ref.shape[0]
    pad = ext_ref.shape[0] - ts

    def glu(u):
        u = u.astype(F32)
        return u[:, :CONV_W] * _sigmoid(u[:, CONV_W:])

    hp = glu(up_ref[up_ref.shape[0] - pad:, :])
    ext_ref[0:pad, :] = jnp.where(j > 0, hp, jnp.zeros_like(hp))
    ext_ref[pad:, :] = glu(uc_ref[...])
    acc = jnp.zeros((ts, CONV_W), F32)
    for k in range(CONV_K):
        off = pad - (CONV_K - 1) + k
        acc = acc + ext_ref[off:off + ts, :] * w_ref[k:k + 1, :]
    y = _layer_norm(acc + b_ref[...], g_ref[...], bb_ref[...])
    o_ref[...] = (y * _sigmoid(y)).astype(o_ref.dtype)


def _conv(h, dw_w, dw_b, ln_g, ln_b, B, S, ts=512, hal=128):
    T = B * S
    nt = S // ts
    r = ts // hal
    vec = pl.BlockSpec((1, CONV_W), lambda b, j: (0, 0))
    return pl.pallas_call(
        _conv_kernel,
        grid=(B, nt),
        in_specs=[pl.BlockSpec((ts, 2 * CONV_W), lambda b, j: (b * nt + j, COL_CONV // (2 * CONV_W))),
                  pl.BlockSpec((hal, 2 * CONV_W),
                               lambda b, j: (jnp.maximum((b * nt + j) * r - 1, 0), COL_CONV // (2 * CONV_W))),
                  pl.BlockSpec((32, CONV_W), lambda b, j: (0, 0)),
                  vec, vec, vec],
        out_specs=pl.BlockSpec((ts, CONV_W), lambda b, j: (b * nt + j, 0)),
        out_shape=jax.ShapeDtypeStruct((T, CONV_W), BF16),
        scratch_shapes=[pltpu.VMEM((ts + 32, CONV_W), F32)],
        compiler_params=_cparams(("parallel", "arbitrary")),
        name="conf_conv",
    )(h, h, dw_w, dw_b, ln_g, ln_b)


def _merge_kernel(xb_ref, x_ref, yf_ref, yp_ref, yc_ref, ys_ref, wg_ref, pf_ref, pp_ref, pc_ref, ps_ref,
                  wo_ref, g_ref, b_ref, x1_ref, x1t_ref):
    xb = xb_ref[...]
    merged = None
    for br, (y_ref, p_ref) in enumerate(((yf_ref, pf_ref), (yp_ref, pp_ref), (yc_ref, pc_ref), (ys_ref, ps_ref))):
        gate = _sigmoid(jnp.dot(xb, wg_ref[:, br * D_MODEL:(br + 1) * D_MODEL], preferred_element_type=F32))
        term = gate * jnp.dot(y_ref[...], p_ref[...], preferred_element_type=F32)
        merged = term if merged is None else merged + term
    mix = jnp.dot(merged.astype(BF16), wo_ref[...], preferred_element_type=F32)
    x1 = _layer_norm(ALPHA * x_ref[...] + mix, g_ref[...], b_ref[...])
    x1_ref[...] = x1
    x1t_ref[...] = x1.T.astype(BF16)


def _merge(xb, x, yf, yp, yc, ys, wg, pf, pp, pc, ps, wo, g, b, tm=256):
    T = xb.shape[0]
    const = lambda i: (0, 0)
    tok = lambda w: pl.BlockSpec((tm, w), lambda i: (i, 0))
    wspec = lambda a: pl.BlockSpec(a.shape, const)
    return pl.pallas_call(
        _merge_kernel,
        grid=(T // tm,),
        in_specs=[tok(D_MODEL), tok(D_MODEL), tok(512), tok(512), tok(512), tok(512),
                  wspec(wg), wspec(pf), wspec(pp), wspec(pc), wspec(ps), wspec(wo), wspec(g), wspec(b)],
        out_specs=[tok(D_MODEL), pl.BlockSpec((D_MODEL, tm), lambda i: (0, i))],
        out_shape=[jax.ShapeDtypeStruct((T, D_MODEL), F32), jax.ShapeDtypeStruct((D_MODEL, T), BF16)],
        compiler_params=_cparams(("parallel",)),
        name="merge_ln1",
    )(xb, x, yf, yp, yc, ys, wg, pf, pp, pc, ps, wo, g, b)


def _wf_kernel(k_ref, wq_ref, o_ref):
    o_ref[...] = jnp.dot(k_ref[0], wq_ref[...], preferred_element_type=F32).astype(o_ref.dtype)


def _peer_score_weights(wq_t, keys):
    nblk = wq_t.shape[0] // N_KEYS
    return pl.pallas_call(
        _wf_kernel,
        grid=(nblk,),
        in_specs=[pl.BlockSpec((1, N_KEYS, N_KEYS), lambda j: (j % 2, 0, 0)),
                  pl.BlockSpec((N_KEYS, D_MODEL), lambda j: (j, 0))],
        out_specs=pl.BlockSpec((N_KEYS, D_MODEL), lambda j: (j, 0)),
        out_shape=jax.ShapeDtypeStruct(wq_t.shape, BF16),
        compiler_params=_cparams(("parallel",)),
        name="peer_wf",
    )(keys, wq_t)


N_RANK = PEER_TOPK + 1


def _n_cand():
    return [(r, c) for r in range(N_RANK) for c in range(N_RANK) if (r + 1) * (c + 1) <= N_RANK]


def _gelu(z):
    return 0.5 * z * (1.0 + lax.erf(z * (1.0 / math.sqrt(2.0))))


def _extract_top(work_ref, out_ref, n_slab, n_out):
    def round_(r, prev):
        m = jnp.full(prev.shape, NEG_BIG, F32)
        for s in range(n_slab):
            w = work_ref[s]
            m = jnp.maximum(m, jnp.where(w < prev, w, NEG_BIG))
        out_ref[r] = m
        return m
    lax.fori_loop(0, n_out, round_, jnp.full(work_ref.shape[1:], -NEG_BIG, F32))


def _dup_bf16(v):
    u = pltpu.bitcast(v.astype(BF16).astype(F32), jnp.uint32)
    return u | (u >> 16)


def _row_bf16(slab, hh):
    return pltpu.bitcast(jnp.broadcast_to(slab[hh:hh + 1, :], (N_KEYS // 2, LANE)), BF16)


def _peer_kernel(xt_ref, wf_ref, u0_ref, u_ref, vt_ref, x1_ref, g_ref, b_ref, x2_ref, x2b_ref,
                 n_scr, e1_scr, rank_scr, e2_scr, s2_scr, y_scr, wt0_scr, wt1_scr, ht0_scr, ht1_scr,
                 work_scr, top_scr, cand_scr, csel_scr, thr_scr):
    g = pl.program_id(1)
    ng = pl.num_programs(1)
    tm = xt_ref.shape[1]
    eb = u_ref.shape[0] // 2
    nh = PEER_HEADS
    rows = N_KEYS * nh

    @pl.when(g == 0)
    def _select():
        y_scr[...] = jnp.zeros_like(y_scr)
        ht0_scr[...] = jnp.dot(u0_ref[...], xt_ref[...], preferred_element_type=F32)
        st = jnp.dot(wf_ref[...], xt_ref[...], preferred_element_type=F32)
        for half in range(2):
            work_scr[...] = st[half * rows:(half + 1) * rows].reshape(N_KEYS, nh, tm)
            _extract_top(work_scr, top_scr.at[half], N_KEYS, N_RANK)
        cands = _n_cand()
        for ci, (r, c) in enumerate(cands):
            cand_scr[ci] = top_scr[0, r] + top_scr[1, c]
        _extract_top(cand_scr, csel_scr, len(cands), N_RANK)
        m0 = csel_scr[0]
        zsum = jnp.zeros_like(m0)
        for r in range(PEER_TOPK):
            zsum = zsum + jnp.exp(csel_scr[r] - m0)
        tau = 0.5 * (csel_scr[PEER_TOPK - 1] + csel_scr[PEER_TOPK])
        a0 = top_scr[0, 0]
        for c in range(PEER_TOPK):
            thr_scr[c] = tau - top_scr[1, c]
        work_scr[...] = st[0:rows].reshape(N_KEYS, nh, tm)

        def key_body(i, carry):
            s = work_scr[i]
            cnt = jnp.zeros_like(s)
            for c in range(PEER_TOPK):
                cnt = cnt + jnp.where(s > thr_scr[c], 1.0, 0.0)
            r0 = pl.multiple_of(i * nh, nh)
            n_scr[pl.ds(r0, nh), :] = _dup_bf16(cnt)
            e1_scr[pl.ds(r0, nh), :] = _dup_bf16(jnp.exp(s - a0) / zsum)
            return carry
        lax.fori_loop(0, N_KEYS, key_body, 0)

        s2_scr[...] = st[2 * rows:3 * rows]
        grp = 16
        for hh in range(nh):
            brow = [top_scr[1, c][hh:hh + 1, :] for c in range(N_RANK)]
            mid = [0.5 * (brow[c] + brow[c + 1]) for c in range(PEER_TOPK)]

            def rank_body(jg, carry):
                r0 = pl.multiple_of(hh * N_KEYS + jg * grp, grp)
                s = s2_scr[pl.ds(r0, grp), :]
                cnt = jnp.zeros_like(s)
                for c in range(PEER_TOPK):
                    cnt = cnt + jnp.where(s < mid[c], 1.0, 0.0)
                rank_scr[pl.ds(r0, grp), :] = cnt.astype(BF16)
                e2_scr[pl.ds(r0, grp), :] = jnp.exp(s - brow[0]).astype(BF16)
                return carry
            lax.fori_loop(0, N_KEYS // grp, rank_body, 0)

    nsub = eb // N_KEYS

    def gate_block(sub, chunks):
        for ii in range(nsub):
            base = pl.multiple_of(((g * 2 + sub) * nsub + ii) * nh, nh)
            for c in chunks:
                ls = slice(c * LANE, (c + 1) * LANE)
                n_i = n_scr[pl.ds(base, nh), ls]
                e_i = e1_scr[pl.ds(base, nh), ls]
                acc = None
                for hh in range(nh):
                    rs = slice(hh * N_KEYS, (hh + 1) * N_KEYS)
                    term = jnp.where(rank_scr[rs, ls] < _row_bf16(n_i, hh), e2_scr[rs, ls],
                                     jnp.zeros((), BF16)) * _row_bf16(e_i, hh)
                    acc = term if acc is None else acc + term
                act = _gelu(ht_scr[sub][ii * N_KEYS:(ii + 1) * N_KEYS, ls]).astype(BF16)
                wt_scr[sub][ii * N_KEYS:(ii + 1) * N_KEYS, ls] = act * acc

    ht_scr = (ht0_scr, ht1_scr)
    wt_scr = (wt0_scr, wt1_scr)
    half = tm // 2
    cpl = half // LANE
    for sub in range(2):
        for hf in range(2):
            ln = slice(hf * half, (hf + 1) * half)
            ht_scr[1 - sub][:, ln] = jnp.dot(u_ref[sub * eb:(sub + 1) * eb, :], xt_ref[:, ln],
                                             preferred_element_type=F32)
            gate_block(sub, range(hf * cpl, (hf + 1) * cpl))
            y_scr[:, ln] += jnp.dot(vt_ref[:, sub * eb:(sub + 1) * eb], wt_scr[sub][:, ln],
                                    preferred_element_type=F32)

    @pl.when(g == ng - 1)
    def _finish():
        z = ALPHA * x1_ref[...] + y_scr[...].T
        x2 = _layer_norm(z, g_ref[...], b_ref[...])
        x2_ref[...] = x2
        x2b_ref[...] = x2.astype(BF16)


def _peer(x1t, wf, u, v, x1, g, b, tm=512, eb=512):
    T = x1.shape[0]
    ne = u.shape[0]
    ncand = len(_n_cand())
    rows = N_KEYS * PEER_HEADS
    const = lambda t, k: (0, 0)
    slab = lambda n: pltpu.VMEM((n, PEER_HEADS, tm), F32)
    u_roll = jnp.roll(u, -eb, axis=0)
    vt = v.reshape(ne // (2 * eb), 2 * eb, D_MODEL).transpose(0, 2, 1)
    return pl.pallas_call(
        _peer_kernel,
        grid=(T // tm, ne // (2 * eb)),
        in_specs=[pl.BlockSpec((D_MODEL, tm), lambda t, k: (0, t)),
                  pl.BlockSpec(wf.shape, const),
                  pl.BlockSpec((eb, D_MODEL), const),
                  pl.BlockSpec((2 * eb, D_MODEL), lambda t, k: (k, 0)),
                  pl.BlockSpec((None, D_MODEL, 2 * eb), lambda t, k: (k, 0, 0)),
                  pl.BlockSpec((tm, D_MODEL), lambda t, k: (t, 0)),
                  pl.BlockSpec((1, D_MODEL), const),
                  pl.BlockSpec((1, D_MODEL), const)],
        out_specs=[pl.BlockSpec((tm, D_MODEL), lambda t, k: (t, 0)),
                   pl.BlockSpec((tm, D_MODEL), lambda t, k: (t, 0))],
        out_shape=[jax.ShapeDtypeStruct((T, D_MODEL), F32), jax.ShapeDtypeStruct((T, D_MODEL), BF16)],
        scratch_shapes=[pltpu.VMEM((rows, tm), jnp.uint32), pltpu.VMEM((rows, tm), jnp.uint32),
                        pltpu.VMEM((rows, tm), BF16), pltpu.VMEM((rows, tm), BF16),
                        pltpu.VMEM((rows, tm), F32),
                        pltpu.VMEM((D_MODEL, tm), F32),
                        pltpu.VMEM((eb, tm), BF16), pltpu.VMEM((eb, tm), BF16),
                        pltpu.VMEM((eb, tm), F32), pltpu.VMEM((eb, tm), F32),
                        slab(N_KEYS), pltpu.VMEM((2, N_RANK, PEER_HEADS, tm), F32),
                        slab(ncand), slab(N_RANK), slab(N_RANK)],
        compiler_params=_cparams(("parallel", "arbitrary")),
        name="peer_ln2",
    )(x1t, wf, u, u_roll, vt, x1, g, b)


def _prep_w_in(w_in):
    sizes = (512, 512, 512, 8, 512, 128, 128, 512, 1024, 4096)
    offs = np.cumsum((0,) + sizes)
    q_f, k_f, v_f, f_l, q_s, k_s, v_s, x_pool, x_conv, gl = (w_in[:, offs[i]:offs[i + 1]] for i in range(10))
    scale = HEAD_DIM ** -0.5
    q_s = q_s.reshape(D_MODEL, 2, 4, HEAD_DIM).transpose(0, 2, 1, 3).reshape(D_MODEL, 512)
    zeros = lambda n: jnp.zeros((D_MODEL, n), w_in.dtype)
    w_h = jnp.concatenate([q_f * scale, k_f, v_f, q_s * scale, x_pool, k_s, v_s, f_l, zeros(120), zeros(128), x_conv],
                          axis=1)
    return w_h.astype(BF16), gl.astype(BF16)


def _prep_peer(wq, k1, k2):
    wq_t = wq.T.astype(BF16)
    keys = jnp.stack([k1, k2]).astype(BF16)
    wf = _peer_score_weights(wq_t, keys)
    wf = wf.reshape(PEER_HEADS, 2, N_KEYS, D_MODEL)
    kh = lambda half: wf[:, half].transpose(1, 0, 2).reshape(N_KEYS * PEER_HEADS, D_MODEL)
    hk = wf[:, 1].reshape(PEER_HEADS * N_KEYS, D_MODEL)
    return jnp.concatenate([kh(0), kh(1), hk], axis=0)


def _pad_lanes(v, n):
    return jnp.zeros((1, n), F32).at[0, :v.shape[0]].set(v.astype(F32))


def kernel(x, w_in, b_f, swa_sinks, pool_w, pool_scale, dw_w, dw_b, conv_ln_g, conv_ln_b, p_fox, p_swa, p_pool,
           p_conv, w_out, ln1_g, ln1_b, peer_wq, peer_k1, peer_k2, peer_u, peer_v, ln2_g, ln2_b):
    B, S, D = x.shape
    T = B * S
    xf = x.reshape(T, D)
    xb = xf.astype(BF16)
    row = lambda v: v.reshape(1, -1).astype(F32)
    for l in range(DEPTH):
        w_h, w_gate = _prep_w_in(w_in[l])
        h = _inproj(xb, w_h)
        aq, ak = _decay(h, _pad_lanes(b_f[l], LANE), B, S)
        y_fox = _fox(h, aq, ak, B, S)
        y_swa = _swa(h, swa_sinks[l].astype(F32), B, S)
        y_pool = _pool(h, pool_w[l].astype(BF16), row(pool_scale[l]), B, S)
        dw = jnp.zeros((32, CONV_W), F32).at[:CONV_K].set(dw_w[l])
        y_conv = _conv(h, dw, row(dw_b[l]), row(conv_ln_g[l]), row(conv_ln_b[l]), B, S)
        ps = p_swa[l].reshape(2, 4, HEAD_DIM, D).transpose(1, 0, 2, 3).reshape(512, D)
        x1, x1t = _merge(xb, xf, y_fox, y_pool, y_conv, y_swa, w_gate, p_fox[l].astype(BF16),
                         p_pool[l].astype(BF16), p_conv[l].astype(BF16), ps.astype(BF16),
                         w_out[l].astype(BF16), row(ln1_g[l]), row(ln1_b[l]))
        wf = _prep_peer(peer_wq[l], peer_k1[l], peer_k2[l])
        xf, xb = _peer(x1t, wf, peer_u[l].astype(BF16), peer_v[l].astype(BF16), x1,
                       row(ln2_g[l]), row(ln2_b[l]))
    return xf.reshape(B, S, D)
```

```python
import functools
import math

import numpy as np
import jax
import jax.numpy as jnp
from jax import lax
from jax.experimental import pallas as pl
from jax.experimental.pallas import tpu as pltpu

F32 = jnp.float32
BF16 = jnp.bfloat16

D_MODEL = 1024
DEPTH = 2
FOX_HEADS = 8
HEAD_DIM = 64
SWA_HEADS = 8
SWA_KV = 2
WINDOW = 128
POOL_WINDOWS = (2, 4, 8, 16)
POOL_GW = 128
POOL_W = 512
CONV_W = 512
CONV_K = 31
N_BRANCH = 4
PEER_HEADS = 8
N_KEYS = 128
N_EXPERTS = N_KEYS * N_KEYS
PEER_TOPK = 16
LN_EPS = 1e-5
ALPHA = (2 * DEPTH) ** 0.25
NEG_BIG = -1e30

COL_QF, COL_KF, COL_VF, COL_QS, COL_POOL, COL_KS, COL_VS, COL_FL, COL_CONV = (
    0, 512, 1024, 1536, 2048, 2560, 2688, 2816, 3072)
H_COLS = 4096
LANE = 128
AUG_A = 6
AUG_B = 12

VMEM_LIMIT = 56 * 1024 * 1024


def _cparams(sem):
    return pltpu.CompilerParams(dimension_semantics=sem, vmem_limit_bytes=VMEM_LIMIT)


def _layer_norm(z, g, b):
    mu = jnp.mean(z, axis=-1, keepdims=True)
    zc = z - mu
    var = jnp.mean(zc * zc, axis=-1, keepdims=True)
    return zc * lax.rsqrt(var + LN_EPS) * g + b


def _sigmoid(z):
    return 1.0 / (1.0 + jnp.exp(-z))


def _inproj_kernel(x_ref, w_ref, o_ref, *, n_chunk):
    x = x_ref[...]
    for c in range(0, o_ref.shape[1], n_chunk):
        o_ref[:, c:c + n_chunk] = jnp.dot(
            x, w_ref[:, c:c + n_chunk], preferred_element_type=F32).astype(o_ref.dtype)


def _inproj(xb, w, tm=512):
    T, K = xb.shape
    N = w.shape[1]
    return pl.pallas_call(
        functools.partial(_inproj_kernel, n_chunk=1024),
        grid=(T // tm,),
        in_specs=[pl.BlockSpec((tm, K), lambda i: (i, 0)),
                  pl.BlockSpec((K, N), lambda i: (0, 0))],
        out_specs=pl.BlockSpec((tm, N), lambda i: (i, 0)),
        out_shape=jax.ShapeDtypeStruct((T, N), BF16),
        compiler_params=_cparams(("parallel",)),
        name="inproj",
    )(xb, w)


def _split3(v):
    hi = v.astype(BF16)
    r1 = v - hi.astype(F32)
    mid = r1.astype(BF16)
    r2 = r1 - mid.astype(F32)
    return hi, mid, r2.astype(BF16)


def _decay_kernel(fl_ref, bf_ref, selq_ref, selk_ref, cq_ref, ck_ref, aq_ref, ak_ref, carry_ref):
    ts = fl_ref.shape[0]

    @pl.when(pl.program_id(1) == 0)
    def _():
        carry_ref[...] = jnp.zeros_like(carry_ref)

    z = fl_ref[...].astype(F32) + bf_ref[...]
    ls = jnp.minimum(z, 0.0) - jnp.log1p(jnp.exp(-jnp.abs(z)))
    row = lax.broadcasted_iota(jnp.int32, (ts, ts), 0)
    col = lax.broadcasted_iota(jnp.int32, (ts, ts), 1)
    tri = jnp.where(col <= row, 1.0, 0.0).astype(BF16)
    parts = jnp.concatenate(_split3(ls), axis=1)
    cs = jnp.dot(tri, parts, preferred_element_type=F32)
    c = cs[:, :LANE] + cs[:, LANE:2 * LANE] + cs[:, 2 * LANE:] + carry_ref[0:1, :]
    carry_ref[...] = jnp.broadcast_to(c[ts - 1:ts, :], carry_ref.shape)
    cparts = jnp.concatenate(_split3(c), axis=1)
    aq_ref[...] = (jnp.dot(cparts, selq_ref[...], preferred_element_type=F32) + cq_ref[...]).astype(BF16)
    ak_ref[...] = (jnp.dot(cparts, selk_ref[...], preferred_element_type=F32) + ck_ref[...]).astype(BF16)


def _decay_consts():
    selq = np.zeros((3 * LANE, 4 * LANE), np.float32)
    selk = np.zeros((3 * LANE, 4 * LANE), np.float32)
    cq = np.zeros((1, 4 * LANE), np.float32)
    ck = np.zeros((1, 4 * LANE), np.float32)
    for p in range(4):
        for part in range(3):
            for hh in range(2):
                base = p * LANE + hh * AUG_A
                selq[part * LANE + 2 * p + hh, base + part] = 1.0
                cq[0, base + 3 + part] = 1.0
                ck[0, base + part] = 1.0
                selk[part * LANE + 2 * p + hh, base + 3 + part] = -1.0
    return (jnp.asarray(selq, BF16), jnp.asarray(selk, BF16), jnp.asarray(cq), jnp.asarray(ck))


def _decay(h, bf_pad, B, S, ts=512):
    T = B * S
    nt = S // ts
    selq, selk, cq, ck = _decay_consts()
    const = lambda b, j: (0, 0)
    return pl.pallas_call(
        _decay_kernel,
        grid=(B, nt),
        in_specs=[pl.BlockSpec((ts, LANE), lambda b, j: (b * nt + j, COL_FL // LANE)),
                  pl.BlockSpec((1, LANE), const),
                  pl.BlockSpec((3 * LANE, 4 * LANE), const),
                  pl.BlockSpec((3 * LANE, 4 * LANE), const),
                  pl.BlockSpec((1, 4 * LANE), const),
                  pl.BlockSpec((1, 4 * LANE), const)],
        out_specs=[pl.BlockSpec((ts, 4 * LANE), lambda b, j: (b * nt + j, 0)),
                   pl.BlockSpec((ts, 4 * LANE), lambda b, j: (b * nt + j, 0))],
        out_shape=[jax.ShapeDtypeStruct((T, 4 * LANE), BF16)] * 2,
        scratch_shapes=[pltpu.VMEM((8, LANE), F32)],
        compiler_params=_cparams(("parallel", "arbitrary")),
        name="fox_decay",
    )(h, bf_pad, selq, selk, cq, ck)


def _fox_kernel(q_ref, aq_ref, k_ref, ak_ref, v_ref, o_ref, m_scr, acc_scr, s0_scr, s1_scr):
    tq = q_ref.shape[0]
    qi = pl.program_id(2)
    lane2 = lax.broadcasted_iota(jnp.int32, (1, 2 * LANE), 1)
    head_mask = (
        (lane2 < HEAD_DIM) | ((lane2 >= LANE) & (lane2 < LANE + AUG_A)),
        ((lane2 >= HEAD_DIM) & (lane2 < LANE)) | ((lane2 >= LANE + AUG_A) & (lane2 < LANE + AUG_B)),
    )
    qf = jnp.concatenate([q_ref[...], aq_ref[...]], axis=1)
    qs = [jnp.where(mk, qf, jnp.zeros_like(qf)) for mk in head_mask]
    ones_col = jnp.where(lax.broadcasted_iota(jnp.int32, (tq, LANE), 1) == 0, 1.0, 0.0).astype(BF16)
    row = lax.broadcasted_iota(jnp.int32, (tq, tq), 0)
    col = lax.broadcasted_iota(jnp.int32, (tq, tq), 1)

    m_scr[...] = jnp.full(m_scr.shape, NEG_BIG, F32)
    acc_scr[...] = jnp.zeros(acc_scr.shape, F32)

    def logits(j, dst):
        off = pl.multiple_of(j * tq, tq)
        kf = jnp.concatenate([k_ref[pl.ds(off, tq), :], ak_ref[pl.ds(off, tq), :]], axis=1)
        for x in range(2):
            dst[x] = lax.dot_general(qs[x], kf, (((1,), (1,)), ((), ())), preferred_element_type=F32)

    def accumulate(j, src, masked):
        off = pl.multiple_of(j * tq, tq)
        vf = jnp.concatenate([v_ref[pl.ds(off, tq), :], ones_col], axis=1)
        for x in range(2):
            s = src[x]
            if masked:
                s = jnp.where(col <= row, s, NEG_BIG)
            m_prev = m_scr[x]
            m_new = jnp.maximum(m_prev, jnp.max(s, axis=1, keepdims=True))
            alpha = jnp.exp(m_prev - m_new)
            p = jnp.exp(s - jnp.concatenate([m_new] * (tq // LANE), axis=1))
            acc_scr[x] = (acc_scr[x] * jnp.concatenate([alpha, alpha], axis=1)
                          + jnp.dot(p.astype(BF16), vf, preferred_element_type=F32))
            m_scr[x] = m_new

    logits(0, s0_scr)

    def body(jj, carry):
        j = 2 * jj
        logits(j + 1, s1_scr)
        accumulate(j, s0_scr, False)
        logits(j + 2, s0_scr)
        accumulate(j + 1, s1_scr, False)
        return carry

    lax.fori_loop(0, qi // 2, body, 0)

    @pl.when(qi % 2 == 0)
    def _():
        accumulate(qi, s0_scr, True)

    @pl.when(qi % 2 == 1)
    def _():
        logits(qi, s1_scr)
        accumulate(qi - 1, s0_scr, False)
        accumulate(qi, s1_scr, True)

    outs = []
    for x in range(2):
        acc = acc_scr[x]
        outs.append(acc[:, :LANE] / acc[:, LANE:LANE + 1])
    lane = lax.broadcasted_iota(jnp.int32, (tq, LANE), 1)
    o_ref[...] = jnp.where(lane < HEAD_DIM, outs[0], outs[1]).astype(o_ref.dtype)


def _fox(h, aq, ak, B, S, tq=512):
    T = B * S
    nq = S // tq
    return pl.pallas_call(
        _fox_kernel,
        grid=(B, 4, nq),
        in_specs=[pl.BlockSpec((tq, LANE), lambda b, p, i: (b * nq + i, COL_QF // LANE + p)),
                  pl.BlockSpec((tq, LANE), lambda b, p, i: (b * nq + i, p)),
                  pl.BlockSpec((S, LANE), lambda b, p, i: (b, COL_KF // LANE + p)),
                  pl.BlockSpec((S, LANE), lambda b, p, i: (b, p)),
                  pl.BlockSpec((S, LANE), lambda b, p, i: (b, COL_VF // LANE + p))],
        out_specs=pl.BlockSpec((tq, LANE), lambda b, p, i: (b * nq + i, p)),
        out_shape=jax.ShapeDtypeStruct((T, 4 * LANE), BF16),
        scratch_shapes=[pltpu.VMEM((2, tq, LANE), F32), pltpu.VMEM((2, tq, 2 * LANE), F32),
                        pltpu.VMEM((2, tq, tq), F32), pltpu.VMEM((2, tq, tq), F32)],
        compiler_params=_cparams(("parallel", "parallel", "arbitrary")),
        name="fox_attn",
    )(h, aq, h, ak, h)


def _swa_kernel(sink_ref, q_ref, kc_ref, kp_ref, vc_ref, vp_ref, o_ref):
    n = pl.program_id(1)
    blk = q_ref.shape[0]
    kb = jnp.concatenate([kp_ref[...], kc_ref[...]], axis=0)
    vb = jnp.concatenate([vp_ref[...], vc_ref[...]], axis=0)
    qi = lax.broadcasted_iota(jnp.int32, (blk, 2 * blk), 0)
    kj = lax.broadcasted_iota(jnp.int32, (blk, 2 * blk), 1)
    dist = qi + blk - kj
    valid = (dist >= 0) & (dist < WINDOW) & ((kj >= blk) | (n > 0))
    distf = dist.astype(F32)
    lane = lax.broadcasted_iota(jnp.int32, (blk, LANE), 1)
    lo = lane < HEAD_DIM
    for m in range(4):
        qm = q_ref[:, m * LANE:(m + 1) * LANE]
        outs = []
        for half in range(2):
            hd = m + 4 * half
            slope = 2.0 ** (-8.0 * (hd + 1) / SWA_HEADS)
            qh = jnp.where(lo if half == 0 else jnp.logical_not(lo), qm, jnp.zeros_like(qm))
            s = lax.dot_general(qh, kb, (((1,), (1,)), ((), ())), preferred_element_type=F32)
            s = jnp.where(valid, s - slope * distf, NEG_BIG)
            sink = sink_ref[hd]
            mx = jnp.maximum(jnp.max(s, axis=1, keepdims=True), sink)
            e = jnp.exp(s - mx)
            den = jnp.sum(e, axis=1, keepdims=True) + jnp.exp(sink - mx)
            p = (e / den).astype(BF16)
            outs.append(jnp.dot(p, vb, preferred_element_type=F32))
        o_ref[:, m * LANE:(m + 1) * LANE] = jnp.where(lo, outs[0], outs[1]).astype(o_ref.dtype)


def _swa(h, sinks, B, S, blk=128):
    T = B * S
    nb = S // blk
    cur = lambda c: (lambda b, n: (b * nb + n, c))
    prev = lambda c: (lambda b, n: (b * nb + jnp.maximum(n - 1, 0), c))
    return pl.pallas_call(
        _swa_kernel,
        grid=(B, nb),
        in_specs=[pl.BlockSpec(memory_space=pltpu.SMEM),
                  pl.BlockSpec((blk, 4 * LANE), cur(COL_QS // (4 * LANE))),
                  pl.BlockSpec((blk, LANE), cur(COL_KS // LANE)),
                  pl.BlockSpec((blk, LANE), prev(COL_KS // LANE)),
                  pl.BlockSpec((blk, LANE), cur(COL_VS // LANE)),
                  pl.BlockSpec((blk, LANE), prev(COL_VS // LANE))],
        out_specs=pl.BlockSpec((blk, 4 * LANE), lambda b, n: (b * nb + n, 0)),
        out_shape=jax.ShapeDtypeStruct((T, 4 * LANE), BF16),
        compiler_params=_cparams(("parallel", "arbitrary")),
        name="swa_attn",
    )(sinks, h, h, h, h, h)


def _pool_kernel(xc_ref, xp_ref, w_ref, sc_ref, o_ref):
    j = pl.program_id(1)
    ts = xc_ref.shape[0]
    hal = xp_ref.shape[0]
    r = lax.broadcasted_iota(jnp.int32, (ts, ts + hal), 0)
    c = lax.broadcasted_iota(jnp.int32, (ts, ts + hal), 1) - hal
    t_glob = (lax.broadcasted_iota(jnp.int32, (ts, LANE), 0) + j * ts + 1).astype(F32)
    has_prev = j > 0
    for g, w in enumerate(POOL_WINDOWS):
        xg = xc_ref[:, g * LANE:(g + 1) * LANE]
        xp = xp_ref[:, g * LANE:(g + 1) * LANE]
        xp = jnp.where(has_prev, xp, jnp.zeros_like(xp))
        ext = jnp.concatenate([xp, xg], axis=0)
        band = jnp.where((c <= r) & (c > r - w), 1.0, 0.0).astype(BF16)
        win = jnp.dot(band, ext, preferred_element_type=F32)
        cnt = jnp.minimum(t_glob, float(w))
        pooled = win / cnt - xg.astype(F32)
        y = jnp.dot(pooled.astype(BF16), w_ref[g], preferred_element_type=F32)
        o_ref[:, g * LANE:(g + 1) * LANE] = (y * sc_ref[:, g * LANE:(g + 1) * LANE]).astype(o_ref.dtype)


def _pool(h, pool_w, pool_scale, B, S, ts=512, hal=128):
    T = B * S
    nt = S // ts
    r = ts // hal
    return pl.pallas_call(
        _pool_kernel,
        grid=(B, nt),
        in_specs=[pl.BlockSpec((ts, POOL_W), lambda b, j: (b * nt + j, COL_POOL // POOL_W)),
                  pl.BlockSpec((hal, POOL_W),
                               lambda b, j: (jnp.maximum((b * nt + j) * r - 1, 0), COL_POOL // POOL_W)),
                  pl.BlockSpec((4, POOL_GW, POOL_GW), lambda b, j: (0, 0, 0)),
                  pl.BlockSpec((1, POOL_W), lambda b, j: (0, 0))],
        out_specs=pl.BlockSpec((ts, POOL_W), lambda b, j: (b * nt + j, 0)),
        out_shape=jax.ShapeDtypeStruct((T, POOL_W), BF16),
        compiler_params=_cparams(("parallel", "arbitrary")),
        name="ms_pool",
    )(h, h, pool_w, pool_scale)


CONV_PAD = 32
SUBLANES = 8


def _conv_kernel(uc_ref, up_ref, w_ref, b_ref, g_ref, bb_ref, o_ref, ext_ref, sh_ref):
    j = pl.program_id(1)
    ts = uc_ref.shape[0]
    pad = CONV_PAD

    def glu(u):
        u = u.astype(F32)
        return u[:, :CONV_W] * _sigmoid(u[:, CONV_W:])

    hp = glu(up_ref[up_ref.shape[0] - pad:, :])
    ext_ref[0:pad, :] = jnp.where(j > 0, hp, jnp.zeros_like(hp))
    ext_ref[pad:pad + ts, :] = glu(uc_ref[...])
    ext_ref[pad + ts:, :] = jnp.zeros((SUBLANES, CONV_W), F32)
    for r in range(SUBLANES):
        sh_ref[r] = ext_ref[r:r + ts + pad, :]
    acc = jnp.zeros((ts, CONV_W), F32)
    for k in range(CONV_K):
        off = pad - (CONV_K - 1) + k
        r = off % SUBLANES
        acc = acc + sh_ref[r, off - r:off - r + ts, :] * w_ref[k:k + 1, :]
    y = _layer_norm(acc + b_ref[...], g_ref[...], bb_ref[...])
    o_ref[...] = (y * _sigmoid(y)).astype(o_ref.dtype)


def _conv(h, dw_w, dw_b, ln_g, ln_b, B, S, ts=512, hal=128):
    T = B * S
    nt = S // ts
    r = ts // hal
    vec = pl.BlockSpec((1, CONV_W), lambda b, j: (0, 0))
    return pl.pallas_call(
        _conv_kernel,
        grid=(B, nt),
        in_specs=[pl.BlockSpec((ts, 2 * CONV_W), lambda b, j: (b * nt + j, COL_CONV // (2 * CONV_W))),
                  pl.BlockSpec((hal, 2 * CONV_W),
                               lambda b, j: (jnp.maximum((b * nt + j) * r - 1, 0), COL_CONV // (2 * CONV_W))),
                  pl.BlockSpec((32, CONV_W), lambda b, j: (0, 0)),
                  vec, vec, vec],
        out_specs=pl.BlockSpec((ts, CONV_W), lambda b, j: (b * nt + j, 0)),
        out_shape=jax.ShapeDtypeStruct((T, CONV_W), BF16),
        scratch_shapes=[pltpu.VMEM((ts + CONV_PAD + SUBLANES, CONV_W), F32),
                        pltpu.VMEM((SUBLANES, ts + CONV_PAD, CONV_W), F32)],
        compiler_params=_cparams(("parallel", "arbitrary")),
        name="conf_conv",
    )(h, h, dw_w, dw_b, ln_g, ln_b)


def _merge_kernel(xb_ref, x_ref, yf_ref, yp_ref, yc_ref, ys_ref, wg_ref, pf_ref, pp_ref, pc_ref, ps_ref,
                  wo_ref, g_ref, b_ref, x1_ref, x1t_ref):
    xb = xb_ref[...]
    merged = None
    for br, (y_ref, p_ref) in enumerate(((yf_ref, pf_ref), (yp_ref, pp_ref), (yc_ref, pc_ref), (ys_ref, ps_ref))):
        gate = _sigmoid(jnp.dot(xb, wg_ref[:, br * D_MODEL:(br + 1) * D_MODEL], preferred_element_type=F32))
        term = gate * jnp.dot(y_ref[...], p_ref[...], preferred_element_type=F32)
        merged = term if merged is None else merged + term
    mix = jnp.dot(merged.astype(BF16), wo_ref[...], preferred_element_type=F32)
    x1 = _layer_norm(ALPHA * x_ref[...] + mix, g_ref[...], b_ref[...])
    x1_ref[...] = x1
    x1t_ref[...] = x1.T.astype(BF16)


def _merge(xb, x, yf, yp, yc, ys, wg, pf, pp, pc, ps, wo, g, b, tm=256):
    T = xb.shape[0]
    const = lambda i: (0, 0)
    tok = lambda w: pl.BlockSpec((tm, w), lambda i: (i, 0))
    wspec = lambda a: pl.BlockSpec(a.shape, const)
    return pl.pallas_call(
        _merge_kernel,
        grid=(T // tm,),
        in_specs=[tok(D_MODEL), tok(D_MODEL), tok(512), tok(512), tok(512), tok(512),
                  wspec(wg), wspec(pf), wspec(pp), wspec(pc), wspec(ps), wspec(wo), wspec(g), wspec(b)],
        out_specs=[tok(D_MODEL), pl.BlockSpec((D_MODEL, tm), lambda i: (0, i))],
        out_shape=[jax.ShapeDtypeStruct((T, D_MODEL), F32), jax.ShapeDtypeStruct((D_MODEL, T), BF16)],
        compiler_params=_cparams(("parallel",)),
        name="merge_ln1",
    )(xb, x, yf, yp, yc, ys, wg, pf, pp, pc, ps, wo, g, b)


def _wf_kernel(k_ref, wq_ref, o_ref):
    o_ref[...] = jnp.dot(k_ref[0], wq_ref[...], preferred_element_type=F32).astype(o_ref.dtype)


def _peer_score_weights(wq_t, keys):
    nblk = wq_t.shape[0] // N_KEYS
    return pl.pallas_call(
        _wf_kernel,
        grid=(nblk,),
        in_specs=[pl.BlockSpec((1, N_KEYS, N_KEYS), lambda j: (j % 2, 0, 0)),
                  pl.BlockSpec((N_KEYS, D_MODEL), lambda j: (j, 0))],
        out_specs=pl.BlockSpec((N_KEYS, D_MODEL), lambda j: (j, 0)),
        out_shape=jax.ShapeDtypeStruct(wq_t.shape, BF16),
        compiler_params=_cparams(("parallel",)),
        name="peer_wf",
    )(keys, wq_t)


N_RANK = PEER_TOPK + 1


def _n_cand():
    return [(r, c) for r in range(N_RANK) for c in range(N_RANK) if (r + 1) * (c + 1) <= N_RANK]


def _gelu(z):
    return 0.5 * z * (1.0 + lax.erf(z * (1.0 / math.sqrt(2.0))))


def _extract_top(work_ref, out_ref, n_slab, n_out):
    def round_(r, prev):
        m = jnp.full(prev.shape, NEG_BIG, F32)
        for s in range(n_slab):
            w = work_ref[s]
            m = jnp.maximum(m, jnp.where(w < prev, w, NEG_BIG))
        out_ref[r] = m
        return m
    lax.fori_loop(0, n_out, round_, jnp.full(work_ref.shape[1:], -NEG_BIG, F32))


def _dup_bf16(v):
    u = pltpu.bitcast(v.astype(BF16).astype(F32), jnp.uint32)
    return u | (u >> 16)


def _row_bf16(slab, hh):
    return pltpu.bitcast(jnp.broadcast_to(slab[hh:hh + 1, :], (N_KEYS // 2, LANE)), BF16)


def _peer_kernel(xt_ref, wf_ref, u0_ref, u_ref, vt_ref, x1_ref, g_ref, b_ref, x2_ref, x2b_ref,
                 n_scr, e1_scr, rank_scr, e2_scr, s2_scr, y_scr, wt0_scr, wt1_scr, ht0_scr, ht1_scr,
                 work_scr, top_scr, cand_scr, csel_scr, thr_scr):
    g = pl.program_id(1)
    ng = pl.num_programs(1)
    tm = xt_ref.shape[1]
    eb = u_ref.shape[0] // 2
    nh = PEER_HEADS
    rows = N_KEYS * nh

    @pl.when(g == 0)
    def _select():
        y_scr[...] = jnp.zeros_like(y_scr)
        ht0_scr[...] = jnp.dot(u0_ref[...], xt_ref[...], preferred_element_type=F32)
        st = jnp.dot(wf_ref[...], xt_ref[...], preferred_element_type=F32)
        for half in range(2):
            work_scr[...] = st[half * rows:(half + 1) * rows].reshape(N_KEYS, nh, tm)
            _extract_top(work_scr, top_scr.at[half], N_KEYS, N_RANK)
        cands = _n_cand()
        for ci, (r, c) in enumerate(cands):
            cand_scr[ci] = top_scr[0, r] + top_scr[1, c]
        _extract_top(cand_scr, csel_scr, len(cands), N_RANK)
        m0 = csel_scr[0]
        zsum = jnp.zeros_like(m0)
        for r in range(PEER_TOPK):
            zsum = zsum + jnp.exp(csel_scr[r] - m0)
        tau = 0.5 * (csel_scr[PEER_TOPK - 1] + csel_scr[PEER_TOPK])
        a0 = top_scr[0, 0]
        for c in range(PEER_TOPK):
            thr_scr[c] = tau - top_scr[1, c]
        work_scr[...] = st[0:rows].reshape(N_KEYS, nh, tm)

        def key_body(i, carry):
            s = work_scr[i]
            cnt = jnp.zeros_like(s)
            for c in range(PEER_TOPK):
                cnt = cnt + jnp.where(s > thr_scr[c], 1.0, 0.0)
            r0 = pl.multiple_of(i * nh, nh)
            n_scr[pl.ds(r0, nh), :] = _dup_bf16(cnt)
            e1_scr[pl.ds(r0, nh), :] = _dup_bf16(jnp.exp(s - a0) / zsum)
            return carry
        lax.fori_loop(0, N_KEYS, key_body, 0)

        s2_scr[...] = st[2 * rows:3 * rows]
        grp = 16
        for hh in range(nh):
            brow = [top_scr[1, c][hh:hh + 1, :] for c in range(N_RANK)]
            mid = [0.5 * (brow[c] + brow[c + 1]) for c in range(PEER_TOPK)]

            def rank_body(jg, carry):
                r0 = pl.multiple_of(hh * N_KEYS + jg * grp, grp)
                s = s2_scr[pl.ds(r0, grp), :]
                cnt = jnp.zeros_like(s)
                for c in range(PEER_TOPK):
                    cnt = cnt + jnp.where(s < mid[c], 1.0, 0.0)
                rank_scr[pl.ds(r0, grp), :] = cnt.astype(BF16)
                e2_scr[pl.ds(r0, grp), :] = jnp.exp(s - brow[0]).astype(BF16)
                return carry
            lax.fori_loop(0, N_KEYS // grp, rank_body, 0)

    nsub = eb // N_KEYS

    def gate_block(sub, chunks):
        for ii in range(nsub):
            base = pl.multiple_of(((g * 2 + sub) * nsub + ii) * nh, nh)
            for c in chunks:
                ls = slice(c * LANE, (c + 1) * LANE)
                n_i = n_scr[pl.ds(base, nh), ls]
                e_i = e1_scr[pl.ds(base, nh), ls]
                acc = None
                for hh in range(nh):
                    rs = slice(hh * N_KEYS, (hh + 1) * N_KEYS)
                    term = jnp.where(rank_scr[rs, ls] < _row_bf16(n_i, hh), e2_scr[rs, ls],
                                     jnp.zeros((), BF16)) * _row_bf16(e_i, hh)
                    acc = term if acc is None else acc + term
                act = _gelu(ht_scr[sub][ii * N_KEYS:(ii + 1) * N_KEYS, ls]).astype(BF16)
                wt_scr[sub][ii * N_KEYS:(ii + 1) * N_KEYS, ls] = act * acc

    ht_scr = (ht0_scr, ht1_scr)
    wt_scr = (wt0_scr, wt1_scr)
    half = tm // 2
    cpl = half // LANE
    for sub in range(2):
        for hf in range(2):
            ln = slice(hf * half, (hf + 1) * half)
            ht_scr[1 - sub][:, ln] = jnp.dot(u_ref[sub * eb:(sub + 1) * eb, :], xt_ref[:, ln],
                                             preferred_element_type=F32)
            gate_block(sub, range(hf * cpl, (hf + 1) * cpl))
            y_scr[:, ln] += jnp.dot(vt_ref[:, sub * eb:(sub + 1) * eb], wt_scr[sub][:, ln],
                                    preferred_element_type=F32)

    @pl.when(g == ng - 1)
    def _finish():
        z = ALPHA * x1_ref[...] + y_scr[...].T
        x2 = _layer_norm(z, g_ref[...], b_ref[...])
        x2_ref[...] = x2
        x2b_ref[...] = x2.astype(BF16)


def _peer(x1t, wf, u, v, x1, g, b, tm=512, eb=256):
    T = x1.shape[0]
    ne = u.shape[0]
    ncand = len(_n_cand())
    rows = N_KEYS * PEER_HEADS
    const = lambda t, k: (0, 0)
    slab = lambda n: pltpu.VMEM((n, PEER_HEADS, tm), F32)
    u_roll = jnp.roll(u, -eb, axis=0)
    vt = v.reshape(ne // (2 * eb), 2 * eb, D_MODEL).transpose(0, 2, 1)
    return pl.pallas_call(
        _peer_kernel,
        grid=(T // tm, ne // (2 * eb)),
        in_specs=[pl.BlockSpec((D_MODEL, tm), lambda t, k: (0, t)),
                  pl.BlockSpec(wf.shape, const),
                  pl.BlockSpec((eb, D_MODEL), const),
                  pl.BlockSpec((2 * eb, D_MODEL), lambda t, k: (k, 0)),
                  pl.BlockSpec((None, D_MODEL, 2 * eb), lambda t, k: (k, 0, 0)),
                  pl.BlockSpec((tm, D_MODEL), lambda t, k: (t, 0)),
                  pl.BlockSpec((1, D_MODEL), const),
                  pl.BlockSpec((1, D_MODEL), const)],
        out_specs=[pl.BlockSpec((tm, D_MODEL), lambda t, k: (t, 0)),
                   pl.BlockSpec((tm, D_MODEL), lambda t, k: (t, 0))],
        out_shape=[jax.ShapeDtypeStruct((T, D_MODEL), F32), jax.ShapeDtypeStruct((T, D_MODEL), BF16)],
        scratch_shapes=[pltpu.VMEM((rows, tm), jnp.uint32), pltpu.VMEM((rows, tm), jnp.uint32),
                        pltpu.VMEM((rows, tm), BF16), pltpu.VMEM((rows, tm), BF16),
                        pltpu.VMEM((rows, tm), F32),
                        pltpu.VMEM((D_MODEL, tm), F32),
                        pltpu.VMEM((eb, tm), BF16), pltpu.VMEM((eb, tm), BF16),
                        pltpu.VMEM((eb, tm), F32), pltpu.VMEM((eb, tm), F32),
                        slab(N_KEYS), pltpu.VMEM((2, N_RANK, PEER_HEADS, tm), F32),
                        slab(ncand), slab(N_RANK), slab(N_RANK)],
        compiler_params=_cparams(("parallel", "arbitrary")),
        name="peer_ln2",
    )(x1t, wf, u, u_roll, vt, x1, g, b)


def _prep_w_in(w_in):
    sizes = (512, 512, 512, 8, 512, 128, 128, 512, 1024, 4096)
    offs = np.cumsum((0,) + sizes)
    q_f, k_f, v_f, f_l, q_s, k_s, v_s, x_pool, x_conv, gl = (w_in[:, offs[i]:offs[i + 1]] for i in range(10))
    scale = HEAD_DIM ** -0.5
    q_s = q_s.reshape(D_MODEL, 2, 4, HEAD_DIM).transpose(0, 2, 1, 3).reshape(D_MODEL, 512)
    zeros = lambda n: jnp.zeros((D_MODEL, n), w_in.dtype)
    w_h = jnp.concatenate([q_f * scale, k_f, v_f, q_s * scale, x_pool, k_s, v_s, f_l, zeros(120), zeros(128), x_conv],
                          axis=1)
    return w_h.astype(BF16), gl.astype(BF16)


def _prep_peer(wq, k1, k2):
    wq_t = wq.T.astype(BF16)
    keys = jnp.stack([k1, k2]).astype(BF16)
    wf = _peer_score_weights(wq_t, keys)
    wf = wf.reshape(PEER_HEADS, 2, N_KEYS, D_MODEL)
    kh = lambda half: wf[:, half].transpose(1, 0, 2).reshape(N_KEYS * PEER_HEADS, D_MODEL)
    hk = wf[:, 1].reshape(PEER_HEADS * N_KEYS, D_MODEL)
    return jnp.concatenate([kh(0), kh(1), hk], axis=0)


def _pad_lanes(v, n):
    return jnp.zeros((1, n), F32).at[0, :v.shape[0]].set(v.astype(F32))


def kernel(x, w_in, b_f, swa_sinks, pool_w, pool_scale, dw_w, dw_b, conv_ln_g, conv_ln_b, p_fox, p_swa, p_pool,
           p_conv, w_out, ln1_g, ln1_b, peer_wq, peer_k1, peer_k2, peer_u, peer_v, ln2_g, ln2_b):
    B, S, D = x.shape
    T = B * S
    xf = x.reshape(T, D)
    xb = xf.astype(BF16)
    row = lambda v: v.reshape(1, -1).astype(F32)
    for l in range(DEPTH):
        w_h, w_gate = _prep_w_in(w_in[l])
        h = _inproj(xb, w_h)
        aq, ak = _decay(h, _pad_lanes(b_f[l], LANE), B, S)
        y_fox = _fox(h, aq, ak, B, S)
        y_swa = _swa(h, swa_sinks[l].astype(F32), B, S)
        y_pool = _pool(h, pool_w[l].astype(BF16), row(pool_scale[l]), B, S)
        dw = jnp.zeros((32, CONV_W), F32).at[:CONV_K].set(dw_w[l])
        y_conv = _conv(h, dw, row(dw_b[l]), row(conv_ln_g[l]), row(conv_ln_b[l]), B, S)
        ps = p_swa[l].reshape(2, 4, HEAD_DIM, D).transpose(1, 0, 2, 3).reshape(512, D)
        x1, x1t = _merge(xb, xf, y_fox, y_pool, y_conv, y_swa, w_gate, p_fox[l].astype(BF16),
                         p_pool[l].astype(BF16), p_conv[l].astype(BF16), ps.astype(BF16),
                         w_out[l].astype(BF16), row(ln1_g[l]), row(ln1_b[l]))
        wf = _prep_peer(peer_wq[l], peer_k1[l], peer_k2[l])
        xf, xb = _peer(x1t, wf, peer_u[l].astype(BF16), peer_v[l].astype(BF16), x1,
                       row(ln2_g[l]), row(ln2_b[l]))
    return xf.reshape(B, S, D)
```

```python
import functools
import math

import numpy as np
import jax
import jax.numpy as jnp
from jax import lax
from jax.experimental import pallas as pl
from jax.experimental.pallas import tpu as pltpu

F32 = jnp.float32
BF16 = jnp.bfloat16

D_MODEL = 1024
DEPTH = 2
FOX_HEADS = 8
HEAD_DIM = 64
SWA_HEADS = 8
SWA_KV = 2
WINDOW = 128
POOL_WINDOWS = (2, 4, 8, 16)
POOL_GW = 128
POOL_W = 512
CONV_W = 512
CONV_K = 31
N_BRANCH = 4
PEER_HEADS = 8
N_KEYS = 128
N_EXPERTS = N_KEYS * N_KEYS
PEER_TOPK = 16
LN_EPS = 1e-5
ALPHA = (2 * DEPTH) ** 0.25
NEG_BIG = -1e30

COL_QF, COL_KF, COL_VF, COL_QS, COL_POOL, COL_KS, COL_VS, COL_FL, COL_CONV = (
    0, 512, 1024, 1536, 2048, 2560, 2688, 2816, 3072)
H_COLS = 4096
LANE = 128
AUG_A = 6
AUG_B = 12

VMEM_LIMIT = 56 * 1024 * 1024


def _cparams(sem):
    return pltpu.CompilerParams(dimension_semantics=sem, vmem_limit_bytes=VMEM_LIMIT)


def _layer_norm(z, g, b):
    mu = jnp.mean(z, axis=-1, keepdims=True)
    zc = z - mu
    var = jnp.mean(zc * zc, axis=-1, keepdims=True)
    return zc * lax.rsqrt(var + LN_EPS) * g + b


def _sigmoid(z):
    return 1.0 / (1.0 + jnp.exp(-z))


def _inproj_kernel(x_ref, w_ref, o_ref, *, n_chunk):
    x = x_ref[...]
    for c in range(0, o_ref.shape[1], n_chunk):
        o_ref[:, c:c + n_chunk] = jnp.dot(
            x, w_ref[:, c:c + n_chunk], preferred_element_type=F32).astype(o_ref.dtype)


def _inproj(xb, w, tm=512):
    T, K = xb.shape
    N = w.shape[1]
    return pl.pallas_call(
        functools.partial(_inproj_kernel, n_chunk=1024),
        grid=(T // tm,),
        in_specs=[pl.BlockSpec((tm, K), lambda i: (i, 0)),
                  pl.BlockSpec((K, N), lambda i: (0, 0))],
        out_specs=pl.BlockSpec((tm, N), lambda i: (i, 0)),
        out_shape=jax.ShapeDtypeStruct((T, N), BF16),
        compiler_params=_cparams(("parallel",)),
        name="inproj",
    )(xb, w)


def _split3(v):
    hi = v.astype(BF16)
    r1 = v - hi.astype(F32)
    mid = r1.astype(BF16)
    r2 = r1 - mid.astype(F32)
    return hi, mid, r2.astype(BF16)


def _decay_kernel(fl_ref, bf_ref, selq_ref, selk_ref, cq_ref, ck_ref, aq_ref, ak_ref, carry_ref):
    ts = fl_ref.shape[0]

    @pl.when(pl.program_id(1) == 0)
    def _():
        carry_ref[...] = jnp.zeros_like(carry_ref)

    z = fl_ref[...].astype(F32) + bf_ref[...]
    ls = jnp.minimum(z, 0.0) - jnp.log1p(jnp.exp(-jnp.abs(z)))
    row = lax.broadcasted_iota(jnp.int32, (ts, ts), 0)
    col = lax.broadcasted_iota(jnp.int32, (ts, ts), 1)
    tri = jnp.where(col <= row, 1.0, 0.0).astype(BF16)
    parts = jnp.concatenate(_split3(ls), axis=1)
    cs = jnp.dot(tri, parts, preferred_element_type=F32)
    c = cs[:, :LANE] + cs[:, LANE:2 * LANE] + cs[:, 2 * LANE:] + carry_ref[0:1, :]
    carry_ref[...] = jnp.broadcast_to(c[ts - 1:ts, :], carry_ref.shape)
    cparts = jnp.concatenate(_split3(c), axis=1)
    aq_ref[...] = (jnp.dot(cparts, selq_ref[...], preferred_element_type=F32) + cq_ref[...]).astype(BF16)
    ak_ref[...] = (jnp.dot(cparts, selk_ref[...], preferred_element_type=F32) + ck_ref[...]).astype(BF16)


def _decay_consts():
    selq = np.zeros((3 * LANE, 4 * LANE), np.float32)
    selk = np.zeros((3 * LANE, 4 * LANE), np.float32)
    cq = np.zeros((1, 4 * LANE), np.float32)
    ck = np.zeros((1, 4 * LANE), np.float32)
    for p in range(4):
        for part in range(3):
            for hh in range(2):
                base = p * LANE + hh * AUG_A
                selq[part * LANE + 2 * p + hh, base + part] = 1.0
                cq[0, base + 3 + part] = 1.0
                ck[0, base + part] = 1.0
                selk[part * LANE + 2 * p + hh, base + 3 + part] = -1.0
    return (jnp.asarray(selq, BF16), jnp.asarray(selk, BF16), jnp.asarray(cq), jnp.asarray(ck))


def _decay(h, bf_pad, B, S, ts=512):
    T = B * S
    nt = S // ts
    selq, selk, cq, ck = _decay_consts()
    const = lambda b, j: (0, 0)
    return pl.pallas_call(
        _decay_kernel,
        grid=(B, nt),
        in_specs=[pl.BlockSpec((ts, LANE), lambda b, j: (b * nt + j, COL_FL // LANE)),
                  pl.BlockSpec((1, LANE), const),
                  pl.BlockSpec((3 * LANE, 4 * LANE), const),
                  pl.BlockSpec((3 * LANE, 4 * LANE), const),
                  pl.BlockSpec((1, 4 * LANE), const),
                  pl.BlockSpec((1, 4 * LANE), const)],
        out_specs=[pl.BlockSpec((ts, 4 * LANE), lambda b, j: (b * nt + j, 0)),
                   pl.BlockSpec((ts, 4 * LANE), lambda b, j: (b * nt + j, 0))],
        out_shape=[jax.ShapeDtypeStruct((T, 4 * LANE), BF16)] * 2,
        scratch_shapes=[pltpu.VMEM((8, LANE), F32)],
        compiler_params=_cparams(("parallel", "arbitrary")),
        name="fox_decay",
    )(h, bf_pad, selq, selk, cq, ck)


def _fox_kernel(q_ref, aq_ref, k_ref, ak_ref, v_ref, o_ref, m_scr, acc_scr, s0_scr, s1_scr):
    tq = q_ref.shape[0]
    qi = pl.program_id(2)
    lane2 = lax.broadcasted_iota(jnp.int32, (1, 2 * LANE), 1)
    head_mask = (
        (lane2 < HEAD_DIM) | ((lane2 >= LANE) & (lane2 < LANE + AUG_A)),
        ((lane2 >= HEAD_DIM) & (lane2 < LANE)) | ((lane2 >= LANE + AUG_A) & (lane2 < LANE + AUG_B)),
    )
    qf = jnp.concatenate([q_ref[...], aq_ref[...]], axis=1)
    qs = [jnp.where(mk, qf, jnp.zeros_like(qf)) for mk in head_mask]
    ones_col = jnp.where(lax.broadcasted_iota(jnp.int32, (tq, LANE), 1) == 0, 1.0, 0.0).astype(BF16)
    row = lax.broadcasted_iota(jnp.int32, (tq, tq), 0)
    col = lax.broadcasted_iota(jnp.int32, (tq, tq), 1)

    m_scr[...] = jnp.full(m_scr.shape, NEG_BIG, F32)
    acc_scr[...] = jnp.zeros(acc_scr.shape, F32)

    def logits(j, dst):
        off = pl.multiple_of(j * tq, tq)
        kf = jnp.concatenate([k_ref[pl.ds(off, tq), :], ak_ref[pl.ds(off, tq), :]], axis=1)
        for x in range(2):
            dst[x] = lax.dot_general(qs[x], kf, (((1,), (1,)), ((), ())), preferred_element_type=F32)

    def accumulate(j, src, masked):
        off = pl.multiple_of(j * tq, tq)
        vf = jnp.concatenate([v_ref[pl.ds(off, tq), :], ones_col], axis=1)
        for x in range(2):
            s = src[x]
            if masked:
                s = jnp.where(col <= row, s, NEG_BIG)
            m_prev = m_scr[x]
            m_new = jnp.maximum(m_prev, jnp.max(s, axis=1, keepdims=True))
            alpha = jnp.exp(m_prev - m_new)
            p = jnp.exp(s - jnp.concatenate([m_new] * (tq // LANE), axis=1))
            acc_scr[x] = (acc_scr[x] * jnp.concatenate([alpha, alpha], axis=1)
                          + jnp.dot(p.astype(BF16), vf, preferred_element_type=F32))
            m_scr[x] = m_new

    logits(0, s0_scr)

    def body(jj, carry):
        j = 2 * jj
        logits(j + 1, s1_scr)
        accumulate(j, s0_scr, False)
        logits(j + 2, s0_scr)
        accumulate(j + 1, s1_scr, False)
        return carry

    lax.fori_loop(0, qi // 2, body, 0)

    @pl.when(qi % 2 == 0)
    def _():
        accumulate(qi, s0_scr, True)

    @pl.when(qi % 2 == 1)
    def _():
        logits(qi, s1_scr)
        accumulate(qi - 1, s0_scr, False)
        accumulate(qi, s1_scr, True)

    outs = []
    for x in range(2):
        acc = acc_scr[x]
        outs.append(acc[:, :LANE] / acc[:, LANE:LANE + 1])
    lane = lax.broadcasted_iota(jnp.int32, (tq, LANE), 1)
    o_ref[...] = jnp.where(lane < HEAD_DIM, outs[0], outs[1]).astype(o_ref.dtype)


def _fox(h, aq, ak, B, S, tq=512):
    T = B * S
    nq = S // tq
    return pl.pallas_call(
        _fox_kernel,
        grid=(B, 4, nq),
        in_specs=[pl.BlockSpec((tq, LANE), lambda b, p, i: (b * nq + i, COL_QF // LANE + p)),
                  pl.BlockSpec((tq, LANE), lambda b, p, i: (b * nq + i, p)),
                  pl.BlockSpec((S, LANE), lambda b, p, i: (b, COL_KF // LANE + p)),
                  pl.BlockSpec((S, LANE), lambda b, p, i: (b, p)),
                  pl.BlockSpec((S, LANE), lambda b, p, i: (b, COL_VF // LANE + p))],
        out_specs=pl.BlockSpec((tq, LANE), lambda b, p, i: (b * nq + i, p)),
        out_shape=jax.ShapeDtypeStruct((T, 4 * LANE), BF16),
        scratch_shapes=[pltpu.VMEM((2, tq, LANE), F32), pltpu.VMEM((2, tq, 2 * LANE), F32),
                        pltpu.VMEM((2, tq, tq), F32), pltpu.VMEM((2, tq, tq), F32)],
        compiler_params=_cparams(("parallel", "parallel", "arbitrary")),
        name="fox_attn",
    )(h, aq, h, ak, h)


def _swa_kernel(sink_ref, q_ref, kc_ref, kp_ref, vc_ref, vp_ref, o_ref):
    n = pl.program_id(1)
    blk = q_ref.shape[0]
    kb = jnp.concatenate([kp_ref[...], kc_ref[...]], axis=0)
    vb = jnp.concatenate([vp_ref[...], vc_ref[...]], axis=0)
    qi = lax.broadcasted_iota(jnp.int32, (blk, 2 * blk), 0)
    kj = lax.broadcasted_iota(jnp.int32, (blk, 2 * blk), 1)
    dist = qi + blk - kj
    valid = (dist >= 0) & (dist < WINDOW) & ((kj >= blk) | (n > 0))
    distf = dist.astype(F32)
    lane = lax.broadcasted_iota(jnp.int32, (blk, LANE), 1)
    lo = lane < HEAD_DIM
    heads = [(m, half) for m in range(4) for half in range(2)]
    logits = []
    for m, half in heads:
        hd = m + 4 * half
        slope = 2.0 ** (-8.0 * (hd + 1) / SWA_HEADS)
        qm = q_ref[:, m * LANE:(m + 1) * LANE]
        qh = jnp.where(lo if half == 0 else jnp.logical_not(lo), qm, jnp.zeros_like(qm))
        s = lax.dot_general(qh, kb, (((1,), (1,)), ((), ())), preferred_element_type=F32)
        logits.append(jnp.where(valid, s - slope * distf, NEG_BIG))
    probs = []
    for (m, half), s in zip(heads, logits):
        sink = sink_ref[m + 4 * half]
        mx = jnp.maximum(jnp.max(s, axis=1, keepdims=True), sink)
        e = jnp.exp(s - mx)
        den = jnp.sum(e, axis=1, keepdims=True) + jnp.exp(sink - mx)
        probs.append((e / den).astype(BF16))
    outs = [jnp.dot(p, vb, preferred_element_type=F32) for p in probs]
    for m in range(4):
        o_ref[:, m * LANE:(m + 1) * LANE] = jnp.where(lo, outs[2 * m], outs[2 * m + 1]).astype(o_ref.dtype)


def _swa(h, sinks, B, S, blk=128):
    T = B * S
    nb = S // blk
    cur = lambda c: (lambda b, n: (b * nb + n, c))
    prev = lambda c: (lambda b, n: (b * nb + jnp.maximum(n - 1, 0), c))
    return pl.pallas_call(
        _swa_kernel,
        grid=(B, nb),
        in_specs=[pl.BlockSpec(memory_space=pltpu.SMEM),
                  pl.BlockSpec((blk, 4 * LANE), cur(COL_QS // (4 * LANE))),
                  pl.BlockSpec((blk, LANE), cur(COL_KS // LANE)),
                  pl.BlockSpec((blk, LANE), prev(COL_KS // LANE)),
                  pl.BlockSpec((blk, LANE), cur(COL_VS // LANE)),
                  pl.BlockSpec((blk, LANE), prev(COL_VS // LANE))],
        out_specs=pl.BlockSpec((blk, 4 * LANE), lambda b, n: (b * nb + n, 0)),
        out_shape=jax.ShapeDtypeStruct((T, 4 * LANE), BF16),
        compiler_params=_cparams(("parallel", "arbitrary")),
        name="swa_attn",
    )(sinks, h, h, h, h, h)


def _pool_kernel(xc_ref, xp_ref, w_ref, sc_ref, o_ref):
    j = pl.program_id(1)
    ts = xc_ref.shape[0]
    hal = xp_ref.shape[0]
    r = lax.broadcasted_iota(jnp.int32, (ts, ts + hal), 0)
    c = lax.broadcasted_iota(jnp.int32, (ts, ts + hal), 1) - hal
    t_glob = (lax.broadcasted_iota(jnp.int32, (ts, LANE), 0) + j * ts + 1).astype(F32)
    has_prev = j > 0
    for g, w in enumerate(POOL_WINDOWS):
        xg = xc_ref[:, g * LANE:(g + 1) * LANE]
        xp = xp_ref[:, g * LANE:(g + 1) * LANE]
        xp = jnp.where(has_prev, xp, jnp.zeros_like(xp))
        ext = jnp.concatenate([xp, xg], axis=0)
        band = jnp.where((c <= r) & (c > r - w), 1.0, 0.0).astype(BF16)
        win = jnp.dot(band, ext, preferred_element_type=F32)
        cnt = jnp.minimum(t_glob, float(w))
        pooled = win / cnt - xg.astype(F32)
        y = jnp.dot(pooled.astype(BF16), w_ref[g], preferred_element_type=F32)
        o_ref[:, g * LANE:(g + 1) * LANE] = (y * sc_ref[:, g * LANE:(g + 1) * LANE]).astype(o_ref.dtype)


def _pool(h, pool_w, pool_scale, B, S, ts=512, hal=128):
    T = B * S
    nt = S // ts
    r = ts // hal
    return pl.pallas_call(
        _pool_kernel,
        grid=(B, nt),
        in_specs=[pl.BlockSpec((ts, POOL_W), lambda b, j: (b * nt + j, COL_POOL // POOL_W)),
                  pl.BlockSpec((hal, POOL_W),
                               lambda b, j: (jnp.maximum((b * nt + j) * r - 1, 0), COL_POOL // POOL_W)),
                  pl.BlockSpec((4, POOL_GW, POOL_GW), lambda b, j: (0, 0, 0)),
                  pl.BlockSpec((1, POOL_W), lambda b, j: (0, 0))],
        out_specs=pl.BlockSpec((ts, POOL_W), lambda b, j: (b * nt + j, 0)),
        out_shape=jax.ShapeDtypeStruct((T, POOL_W), BF16),
        compiler_params=_cparams(("parallel", "arbitrary")),
        name="ms_pool",
    )(h, h, pool_w, pool_scale)


CONV_PAD = 32
SUBLANES = 8


def _conv_kernel(uc_ref, up_ref, w_ref, b_ref, g_ref, bb_ref, o_ref, ext_ref, sh_ref):
    j = pl.program_id(1)
    ts = uc_ref.shape[0]
    pad = CONV_PAD

    def glu(u):
        u = u.astype(F32)
        return u[:, :CONV_W] * _sigmoid(u[:, CONV_W:])

    hp = glu(up_ref[up_ref.shape[0] - pad:, :])
    ext_ref[0:pad, :] = jnp.where(j > 0, hp, jnp.zeros_like(hp))
    ext_ref[pad:pad + ts, :] = glu(uc_ref[...])
    ext_ref[pad + ts:, :] = jnp.zeros((SUBLANES, CONV_W), F32)
    for r in range(SUBLANES):
        sh_ref[r] = ext_ref[r:r + ts + pad, :]
    acc = jnp.zeros((ts, CONV_W), F32)
    for k in range(CONV_K):
        off = pad - (CONV_K - 1) + k
        r = off % SUBLANES
        acc = acc + sh_ref[r, off - r:off - r + ts, :] * w_ref[k:k + 1, :]
    y = _layer_norm(acc + b_ref[...], g_ref[...], bb_ref[...])
    o_ref[...] = (y * _sigmoid(y)).astype(o_ref.dtype)


def _conv(h, dw_w, dw_b, ln_g, ln_b, B, S, ts=512, hal=128):
    T = B * S
    nt = S // ts
    r = ts // hal
    vec = pl.BlockSpec((1, CONV_W), lambda b, j: (0, 0))
    return pl.pallas_call(
        _conv_kernel,
        grid=(B, nt),
        in_specs=[pl.BlockSpec((ts, 2 * CONV_W), lambda b, j: (b * nt + j, COL_CONV // (2 * CONV_W))),
                  pl.BlockSpec((hal, 2 * CONV_W),
                               lambda b, j: (jnp.maximum((b * nt + j) * r - 1, 0), COL_CONV // (2 * CONV_W))),
                  pl.BlockSpec((32, CONV_W), lambda b, j: (0, 0)),
                  vec, vec, vec],
        out_specs=pl.BlockSpec((ts, CONV_W), lambda b, j: (b * nt + j, 0)),
        out_shape=jax.ShapeDtypeStruct((T, CONV_W), BF16),
        scratch_shapes=[pltpu.VMEM((ts + CONV_PAD + SUBLANES, CONV_W), F32),
                        pltpu.VMEM((SUBLANES, ts + CONV_PAD, CONV_W), F32)],
        compiler_params=_cparams(("parallel", "arbitrary")),
        name="conf_conv",
    )(h, h, dw_w, dw_b, ln_g, ln_b)


def _merge_kernel(xb_ref, x_ref, yf_ref, yp_ref, yc_ref, ys_ref, wg_ref, pf_ref, pp_ref, pc_ref, ps_ref,
                  wo_ref, g_ref, b_ref, x1_ref, x1t_ref):
    xb = xb_ref[...]
    merged = None
    for br, (y_ref, p_ref) in enumerate(((yf_ref, pf_ref), (yp_ref, pp_ref), (yc_ref, pc_ref), (ys_ref, ps_ref))):
        gate = _sigmoid(jnp.dot(xb, wg_ref[:, br * D_MODEL:(br + 1) * D_MODEL], preferred_element_type=F32))
        term = gate * jnp.dot(y_ref[...], p_ref[...], preferred_element_type=F32)
        merged = term if merged is None else merged + term
    mix = jnp.dot(merged.astype(BF16), wo_ref[...], preferred_element_type=F32)
    x1 = _layer_norm(ALPHA * x_ref[...] + mix, g_ref[...], b_ref[...])
    x1_ref[...] = x1
    x1t_ref[...] = x1.T.astype(BF16)


def _merge(xb, x, yf, yp, yc, ys, wg, pf, pp, pc, ps, wo, g, b, tm=256):
    T = xb.shape[0]
    const = lambda i: (0, 0)
    tok = lambda w: pl.BlockSpec((tm, w), lambda i: (i, 0))
    wspec = lambda a: pl.BlockSpec(a.shape, const)
    return pl.pallas_call(
        _merge_kernel,
        grid=(T // tm,),
        in_specs=[tok(D_MODEL), tok(D_MODEL), tok(512), tok(512), tok(512), tok(512),
                  wspec(wg), wspec(pf), wspec(pp), wspec(pc), wspec(ps), wspec(wo), wspec(g), wspec(b)],
        out_specs=[tok(D_MODEL), pl.BlockSpec((D_MODEL, tm), lambda i: (0, i))],
        out_shape=[jax.ShapeDtypeStruct((T, D_MODEL), F32), jax.ShapeDtypeStruct((D_MODEL, T), BF16)],
        compiler_params=_cparams(("parallel",)),
        name="merge_ln1",
    )(xb, x, yf, yp, yc, ys, wg, pf, pp, pc, ps, wo, g, b)


def _wf_kernel(k_ref, wq_ref, o_ref):
    o_ref[...] = jnp.dot(k_ref[0], wq_ref[...], preferred_element_type=F32).astype(o_ref.dtype)


def _peer_score_weights(wq_t, keys):
    nblk = wq_t.shape[0] // N_KEYS
    return pl.pallas_call(
        _wf_kernel,
        grid=(nblk,),
        in_specs=[pl.BlockSpec((1, N_KEYS, N_KEYS), lambda j: (j % 2, 0, 0)),
                  pl.BlockSpec((N_KEYS, D_MODEL), lambda j: (j, 0))],
        out_specs=pl.BlockSpec((N_KEYS, D_MODEL), lambda j: (j, 0)),
        out_shape=jax.ShapeDtypeStruct(wq_t.shape, BF16),
        compiler_params=_cparams(("parallel",)),
        name="peer_wf",
    )(keys, wq_t)


N_RANK = PEER_TOPK + 1


def _n_cand():
    return [(r, c) for r in range(N_RANK) for c in range(N_RANK) if (r + 1) * (c + 1) <= N_RANK]


def _gelu(z):
    return 0.5 * z * (1.0 + lax.erf(z * (1.0 / math.sqrt(2.0))))


def _extract_top(work_ref, out_ref, n_slab, n_out):
    def round_(r, prev):
        m = jnp.full(prev.shape, NEG_BIG, F32)
        for s in range(n_slab):
            w = work_ref[s]
            m = jnp.maximum(m, jnp.where(w < prev, w, NEG_BIG))
        out_ref[r] = m
        return m
    lax.fori_loop(0, n_out, round_, jnp.full(work_ref.shape[1:], -NEG_BIG, F32))


def _dup_bf16(v):
    u = pltpu.bitcast(v.astype(BF16).astype(F32), jnp.uint32)
    return u | (u >> 16)


def _row_bf16(slab, hh):
    return pltpu.bitcast(jnp.broadcast_to(slab[hh:hh + 1, :], (N_KEYS // 2, LANE)), BF16)


def _peer_kernel(xt_ref, wf_ref, u0_ref, u_ref, vt_ref, x1_ref, g_ref, b_ref, x2_ref, x2b_ref,
                 n_scr, e1_scr, rank_scr, e2_scr, s2_scr, y_scr, wt0_scr, wt1_scr, ht0_scr, ht1_scr,
                 work_scr, top_scr, cand_scr, csel_scr, thr_scr):
    g = pl.program_id(1)
    ng = pl.num_programs(1)
    tm = xt_ref.shape[1]
    eb = u_ref.shape[0] // 2
    nh = PEER_HEADS
    rows = N_KEYS * nh

    @pl.when(g == 0)
    def _select():
        y_scr[...] = jnp.zeros_like(y_scr)
        ht0_scr[...] = jnp.dot(u0_ref[...], xt_ref[...], preferred_element_type=F32)
        st = jnp.dot(wf_ref[...], xt_ref[...], preferred_element_type=F32)
        for half in range(2):
            work_scr[...] = st[half * rows:(half + 1) * rows].reshape(N_KEYS, nh, tm)
            _extract_top(work_scr, top_scr.at[half], N_KEYS, N_RANK)
        cands = _n_cand()
        for ci, (r, c) in enumerate(cands):
            cand_scr[ci] = top_scr[0, r] + top_scr[1, c]
        _extract_top(cand_scr, csel_scr, len(cands), N_RANK)
        m0 = csel_scr[0]
        zsum = jnp.zeros_like(m0)
        for r in range(PEER_TOPK):
            zsum = zsum + jnp.exp(csel_scr[r] - m0)
        tau = 0.5 * (csel_scr[PEER_TOPK - 1] + csel_scr[PEER_TOPK])
        a0 = top_scr[0, 0]
        for c in range(PEER_TOPK):
            thr_scr[c] = tau - top_scr[1, c]
        work_scr[...] = st[0:rows].reshape(N_KEYS, nh, tm)

        def key_body(i, carry):
            s = work_scr[i]
            cnt = jnp.zeros_like(s)
            for c in range(PEER_TOPK):
                cnt = cnt + jnp.where(s > thr_scr[c], 1.0, 0.0)
            r0 = pl.multiple_of(i * nh, nh)
            n_scr[pl.ds(r0, nh), :] = _dup_bf16(cnt)
            e1_scr[pl.ds(r0, nh), :] = _dup_bf16(jnp.exp(s - a0) / zsum)
            return carry
        lax.fori_loop(0, N_KEYS, key_body, 0)

        s2_scr[...] = st[2 * rows:3 * rows]
        grp = 16
        for hh in range(nh):
            brow = [top_scr[1, c][hh:hh + 1, :] for c in range(N_RANK)]
            mid = [0.5 * (brow[c] + brow[c + 1]) for c in range(PEER_TOPK)]

            def rank_body(jg, carry):
                r0 = pl.multiple_of(hh * N_KEYS + jg * grp, grp)
                s = s2_scr[pl.ds(r0, grp), :]
                cnt = jnp.zeros_like(s)
                for c in range(PEER_TOPK):
                    cnt = cnt + jnp.where(s < mid[c], 1.0, 0.0)
                rank_scr[pl.ds(r0, grp), :] = cnt.astype(BF16)
                e2_scr[pl.ds(r0, grp), :] = jnp.exp(s - brow[0]).astype(BF16)
                return carry
            lax.fori_loop(0, N_KEYS // grp, rank_body, 0)

    nsub = eb // N_KEYS

    def gate_block(sub, chunks):
        for ii in range(nsub):
            base = pl.multiple_of(((g * 2 + sub) * nsub + ii) * nh, nh)
            for c in chunks:
                ls = slice(c * LANE, (c + 1) * LANE)
                n_i = n_scr[pl.ds(base, nh), ls]
                e_i = e1_scr[pl.ds(base, nh), ls]
                acc = None
                for hh in range(nh):
                    rs = slice(hh * N_KEYS, (hh + 1) * N_KEYS)
                    term = jnp.where(rank_scr[rs, ls] < _row_bf16(n_i, hh), e2_scr[rs, ls],
                                     jnp.zeros((), BF16)) * _row_bf16(e_i, hh)
                    acc = term if acc is None else acc + term
                act = _gelu(ht_scr[sub][ii * N_KEYS:(ii + 1) * N_KEYS, ls]).astype(BF16)
                wt_scr[sub][ii * N_KEYS:(ii + 1) * N_KEYS, ls] = act * acc

    ht_scr = (ht0_scr, ht1_scr)
    wt_scr = (wt0_scr, wt1_scr)
    half = tm // 2
    cpl = half // LANE
    for sub in range(2):
        for hf in range(2):
            ln = slice(hf * half, (hf + 1) * half)
            ht_scr[1 - sub][:, ln] = jnp.dot(u_ref[sub * eb:(sub + 1) * eb, :], xt_ref[:, ln],
                                             preferred_element_type=F32)
            gate_block(sub, range(hf * cpl, (hf + 1) * cpl))
            y_scr[:, ln] += jnp.dot(vt_ref[:, sub * eb:(sub + 1) * eb], wt_scr[sub][:, ln],
                                    preferred_element_type=F32)

    @pl.when(g == ng - 1)
    def _finish():
        z = ALPHA * x1_ref[...] + y_scr[...].T
        x2 = _layer_norm(z, g_ref[...], b_ref[...])
        x2_ref[...] = x2
        x2b_ref[...] = x2.astype(BF16)


def _peer(x1t, wf, u, v, x1, g, b, tm=512, eb=256):
    T = x1.shape[0]
    ne = u.shape[0]
    ncand = len(_n_cand())
    rows = N_KEYS * PEER_HEADS
    const = lambda t, k: (0, 0)
    slab = lambda n: pltpu.VMEM((n, PEER_HEADS, tm), F32)
    u_roll = jnp.roll(u, -eb, axis=0)
    vt = v.reshape(ne // (2 * eb), 2 * eb, D_MODEL).transpose(0, 2, 1)
    return pl.pallas_call(
        _peer_kernel,
        grid=(T // tm, ne // (2 * eb)),
        in_specs=[pl.BlockSpec((D_MODEL, tm), lambda t, k: (0, t)),
                  pl.BlockSpec(wf.shape, const),
                  pl.BlockSpec((eb, D_MODEL), const),
                  pl.BlockSpec((2 * eb, D_MODEL), lambda t, k: (k, 0)),
                  pl.BlockSpec((None, D_MODEL, 2 * eb), lambda t, k: (k, 0, 0)),
                  pl.BlockSpec((tm, D_MODEL), lambda t, k: (t, 0)),
                  pl.BlockSpec((1, D_MODEL), const),
                  pl.BlockSpec((1, D_MODEL), const)],
        out_specs=[pl.BlockSpec((tm, D_MODEL), lambda t, k: (t, 0)),
                   pl.BlockSpec((tm, D_MODEL), lambda t, k: (t, 0))],
        out_shape=[jax.ShapeDtypeStruct((T, D_MODEL), F32), jax.ShapeDtypeStruct((T, D_MODEL), BF16)],
        scratch_shapes=[pltpu.VMEM((rows, tm), jnp.uint32), pltpu.VMEM((rows, tm), jnp.uint32),
                        pltpu.VMEM((rows, tm), BF16), pltpu.VMEM((rows, tm), BF16),
                        pltpu.VMEM((rows, tm), F32),
                        pltpu.VMEM((D_MODEL, tm), F32),
                        pltpu.VMEM((eb, tm), BF16), pltpu.VMEM((eb, tm), BF16),
                        pltpu.VMEM((eb, tm), F32), pltpu.VMEM((eb, tm), F32),
                        slab(N_KEYS), pltpu.VMEM((2, N_RANK, PEER_HEADS, tm), F32),
                        slab(ncand), slab(N_RANK), slab(N_RANK)],
        compiler_params=_cparams(("parallel", "arbitrary")),
        name="peer_ln2",
    )(x1t, wf, u, u_roll, vt, x1, g, b)


def _prep_w_in(w_in):
    sizes = (512, 512, 512, 8, 512, 128, 128, 512, 1024, 4096)
    offs = np.cumsum((0,) + sizes)
    q_f, k_f, v_f, f_l, q_s, k_s, v_s, x_pool, x_conv, gl = (w_in[:, offs[i]:offs[i + 1]] for i in range(10))
    scale = HEAD_DIM ** -0.5
    q_s = q_s.reshape(D_MODEL, 2, 4, HEAD_DIM).transpose(0, 2, 1, 3).reshape(D_MODEL, 512)
    zeros = lambda n: jnp.zeros((D_MODEL, n), w_in.dtype)
    w_h = jnp.concatenate([q_f * scale, k_f, v_f, q_s * scale, x_pool, k_s, v_s, f_l, zeros(120), zeros(128), x_conv],
                          axis=1)
    return w_h.astype(BF16), gl.astype(BF16)


def _prep_peer(wq, k1, k2):
    wq_t = wq.T.astype(BF16)
    keys = jnp.stack([k1, k2]).astype(BF16)
    wf = _peer_score_weights(wq_t, keys)
    wf = wf.reshape(PEER_HEADS, 2, N_KEYS, D_MODEL)
    kh = lambda half: wf[:, half].transpose(1, 0, 2).reshape(N_KEYS * PEER_HEADS, D_MODEL)
    hk = wf[:, 1].reshape(PEER_HEADS * N_KEYS, D_MODEL)
    return jnp.concatenate([kh(0), kh(1), hk], axis=0)


def _pad_lanes(v, n):
    return jnp.zeros((1, n), F32).at[0, :v.shape[0]].set(v.astype(F32))


def kernel(x, w_in, b_f, swa_sinks, pool_w, pool_scale, dw_w, dw_b, conv_ln_g, conv_ln_b, p_fox, p_swa, p_pool,
           p_conv, w_out, ln1_g, ln1_b, peer_wq, peer_k1, peer_k2, peer_u, peer_v, ln2_g, ln2_b):
    B, S, D = x.shape
    T = B * S
    xf = x.reshape(T, D)
    xb = xf.astype(BF16)
    row = lambda v: v.reshape(1, -1).astype(F32)
    for l in range(DEPTH):
        w_h, w_gate = _prep_w_in(w_in[l])
        h = _inproj(xb, w_h)
        aq, ak = _decay(h, _pad_lanes(b_f[l], LANE), B, S)
        y_fox = _fox(h, aq, ak, B, S)
        y_swa = _swa(h, swa_sinks[l].astype(F32), B, S)
        y_pool = _pool(h, pool_w[l].astype(BF16), row(pool_scale[l]), B, S)
        dw = jnp.zeros((32, CONV_W), F32).at[:CONV_K].set(dw_w[l])
        y_conv = _conv(h, dw, row(dw_b[l]), row(conv_ln_g[l]), row(conv_ln_b[l]), B, S)
        ps = p_swa[l].reshape(2, 4, HEAD_DIM, D).transpose(1, 0, 2, 3).reshape(512, D)
        x1, x1t = _merge(xb, xf, y_fox, y_pool, y_conv, y_swa, w_gate, p_fox[l].astype(BF16),
                         p_pool[l].astype(BF16), p_conv[l].astype(BF16), ps.astype(BF16),
                         w_out[l].astype(BF16), row(ln1_g[l]), row(ln1_b[l]))
        wf = _prep_peer(peer_wq[l], peer_k1[l], peer_k2[l])
        xf, xb = _peer(x1t, wf, peer_u[l].astype(BF16), peer_v[l].astype(BF16), x1,
                       row(ln2_g[l]), row(ln2_b[l]))
    return xf.reshape(B, S, D)
```

```python
import functools
import math

import numpy as np
import jax
import jax.numpy as jnp
from jax import lax
from jax.experimental import pallas as pl
from jax.experimental.pallas import tpu as pltpu

F32 = jnp.float32
BF16 = jnp.bfloat16

D_MODEL = 1024
DEPTH = 2
FOX_HEADS = 8
HEAD_DIM = 64
SWA_HEADS = 8
SWA_KV = 2
WINDOW = 128
POOL_WINDOWS = (2, 4, 8, 16)
POOL_GW = 128
POOL_W = 512
CONV_W = 512
CONV_K = 31
N_BRANCH = 4
PEER_HEADS = 8
N_KEYS = 128
N_EXPERTS = N_KEYS * N_KEYS
PEER_TOPK = 16
LN_EPS = 1e-5
ALPHA = (2 * DEPTH) ** 0.25
NEG_BIG = -1e30

COL_QF, COL_KF, COL_VF, COL_QS, COL_POOL, COL_KS, COL_VS, COL_FL, COL_CONV = (
    0, 512, 1024, 1536, 2048, 2560, 2688, 2816, 3072)
H_COLS = 4096
LANE = 128
AUG_A = 6
AUG_B = 12

VMEM_LIMIT = 56 * 1024 * 1024


def _cparams(sem):
    return pltpu.CompilerParams(dimension_semantics=sem, vmem_limit_bytes=VMEM_LIMIT)


def _layer_norm(z, g, b):
    mu = jnp.mean(z, axis=-1, keepdims=True)
    zc = z - mu
    var = jnp.mean(zc * zc, axis=-1, keepdims=True)
    return zc * lax.rsqrt(var + LN_EPS) * g + b


def _sigmoid(z):
    return 1.0 / (1.0 + jnp.exp(-z))


def _inproj_kernel(x_ref, w_ref, o_ref, *, n_chunk):
    x = x_ref[...].astype(BF16)
    for c in range(0, o_ref.shape[1], n_chunk):
        o_ref[:, c:c + n_chunk] = jnp.dot(
            x, w_ref[:, c:c + n_chunk], preferred_element_type=F32).astype(o_ref.dtype)


def _inproj(x, w, tm=512):
    T, K = x.shape
    N = w.shape[1]
    return pl.pallas_call(
        functools.partial(_inproj_kernel, n_chunk=1024),
        grid=(T // tm,),
        in_specs=[pl.BlockSpec((tm, K), lambda i: (i, 0)),
                  pl.BlockSpec((K, N), lambda i: (0, 0))],
        out_specs=pl.BlockSpec((tm, N), lambda i: (i, 0)),
        out_shape=jax.ShapeDtypeStruct((T, N), BF16),
        compiler_params=_cparams(("parallel",)),
        name="inproj",
    )(x, w)


def _split3(v):
    hi = v.astype(BF16)
    r1 = v - hi.astype(F32)
    mid = r1.astype(BF16)
    r2 = r1 - mid.astype(F32)
    return hi, mid, r2.astype(BF16)


def _decay_kernel(fl_ref, bf_ref, selq_ref, selk_ref, cq_ref, ck_ref, aq_ref, ak_ref, carry_ref):
    ts = fl_ref.shape[0]

    @pl.when(pl.program_id(1) == 0)
    def _():
        carry_ref[...] = jnp.zeros_like(carry_ref)

    z = fl_ref[...].astype(F32) + bf_ref[...]
    ls = jnp.minimum(z, 0.0) - jnp.log1p(jnp.exp(-jnp.abs(z)))
    row = lax.broadcasted_iota(jnp.int32, (ts, ts), 0)
    col = lax.broadcasted_iota(jnp.int32, (ts, ts), 1)
    tri = jnp.where(col <= row, 1.0, 0.0).astype(BF16)
    parts = jnp.concatenate(_split3(ls), axis=1)
    cs = jnp.dot(tri, parts, preferred_element_type=F32)
    c = cs[:, :LANE] + cs[:, LANE:2 * LANE] + cs[:, 2 * LANE:] + carry_ref[0:1, :]
    carry_ref[...] = jnp.broadcast_to(c[ts - 1:ts, :], carry_ref.shape)
    cparts = jnp.concatenate(_split3(c), axis=1)
    aq_ref[...] = (jnp.dot(cparts, selq_ref[...], preferred_element_type=F32) + cq_ref[...]).astype(BF16)
    ak_ref[...] = (jnp.dot(cparts, selk_ref[...], preferred_element_type=F32) + ck_ref[...]).astype(BF16)


def _decay_consts():
    selq = np.zeros((3 * LANE, 4 * LANE), np.float32)
    selk = np.zeros((3 * LANE, 4 * LANE), np.float32)
    cq = np.zeros((1, 4 * LANE), np.float32)
    ck = np.zeros((1, 4 * LANE), np.float32)
    for p in range(4):
        for part in range(3):
            for hh in range(2):
                base = p * LANE + hh * AUG_A
                selq[part * LANE + 2 * p + hh, base + part] = 1.0
                cq[0, base + 3 + part] = 1.0
                ck[0, base + part] = 1.0
                selk[part * LANE + 2 * p + hh, base + 3 + part] = -1.0
    return (jnp.asarray(selq, BF16), jnp.asarray(selk, BF16), jnp.asarray(cq), jnp.asarray(ck))


def _decay(h, bf_pad, B, S, ts=512):
    T = B * S
    nt = S // ts
    selq, selk, cq, ck = _decay_consts()
    const = lambda b, j: (0, 0)
    return pl.pallas_call(
        _decay_kernel,
        grid=(B, nt),
        in_specs=[pl.BlockSpec((ts, LANE), lambda b, j: (b * nt + j, COL_FL // LANE)),
                  pl.BlockSpec((1, LANE), const),
                  pl.BlockSpec((3 * LANE, 4 * LANE), const),
                  pl.BlockSpec((3 * LANE, 4 * LANE), const),
                  pl.BlockSpec((1, 4 * LANE), const),
                  pl.BlockSpec((1, 4 * LANE), const)],
        out_specs=[pl.BlockSpec((ts, 4 * LANE), lambda b, j: (b * nt + j, 0)),
                   pl.BlockSpec((ts, 4 * LANE), lambda b, j: (b * nt + j, 0))],
        out_shape=[jax.ShapeDtypeStruct((T, 4 * LANE), BF16)] * 2,
        scratch_shapes=[pltpu.VMEM((8, LANE), F32)],
        compiler_params=_cparams(("parallel", "arbitrary")),
        name="fox_decay",
    )(h, bf_pad, selq, selk, cq, ck)


def _fox_kernel(q_ref, aq_ref, k_ref, ak_ref, v_ref, o_ref, m_scr, acc_scr, s0_scr, s1_scr):
    tq = q_ref.shape[0]
    qi = pl.program_id(2)
    lane2 = lax.broadcasted_iota(jnp.int32, (1, 2 * LANE), 1)
    head_mask = (
        (lane2 < HEAD_DIM) | ((lane2 >= LANE) & (lane2 < LANE + AUG_A)),
        ((lane2 >= HEAD_DIM) & (lane2 < LANE)) | ((lane2 >= LANE + AUG_A) & (lane2 < LANE + AUG_B)),
    )
    qf = jnp.concatenate([q_ref[...], aq_ref[...]], axis=1)
    qs = [jnp.where(mk, qf, jnp.zeros_like(qf)) for mk in head_mask]
    ones_col = jnp.where(lax.broadcasted_iota(jnp.int32, (tq, LANE), 1) == 0, 1.0, 0.0).astype(BF16)
    row = lax.broadcasted_iota(jnp.int32, (tq, tq), 0)
    col = lax.broadcasted_iota(jnp.int32, (tq, tq), 1)

    m_scr[...] = jnp.full(m_scr.shape, NEG_BIG, F32)
    acc_scr[...] = jnp.zeros(acc_scr.shape, F32)

    def logits(j, dst):
        off = pl.multiple_of(j * tq, tq)
        kf = jnp.concatenate([k_ref[pl.ds(off, tq), :], ak_ref[pl.ds(off, tq), :]], axis=1)
        for x in range(2):
            dst[x] = lax.dot_general(qs[x], kf, (((1,), (1,)), ((), ())), preferred_element_type=F32)

    def accumulate(j, src, masked):
        off = pl.multiple_of(j * tq, tq)
        vf = jnp.concatenate([v_ref[pl.ds(off, tq), :], ones_col], axis=1)
        for x in range(2):
            s = src[x]
            if masked:
                s = jnp.where(col <= row, s, NEG_BIG)
            m_prev = m_scr[x]
            m_new = jnp.maximum(m_prev, jnp.max(s, axis=1, keepdims=True))
            alpha = jnp.exp(m_prev - m_new)
            p = jnp.exp(s - jnp.concatenate([m_new] * (tq // LANE), axis=1))
            acc_scr[x] = (acc_scr[x] * jnp.concatenate([alpha, alpha], axis=1)
                          + jnp.dot(p.astype(BF16), vf, preferred_element_type=F32))
            m_scr[x] = m_new

    logits(0, s0_scr)

    def body(jj, carry):
        j = 2 * jj
        logits(j + 1, s1_scr)
        accumulate(j, s0_scr, False)
        logits(j + 2, s0_scr)
        accumulate(j + 1, s1_scr, False)
        return carry

    lax.fori_loop(0, qi // 2, body, 0)

    @pl.when(qi % 2 == 0)
    def _():
        accumulate(qi, s0_scr, True)

    @pl.when(qi % 2 == 1)
    def _():
        logits(qi, s1_scr)
        accumulate(qi - 1, s0_scr, False)
        accumulate(qi, s1_scr, True)

    outs = []
    for x in range(2):
        acc = acc_scr[x]
        outs.append(acc[:, :LANE] / acc[:, LANE:LANE + 1])
    lane = lax.broadcasted_iota(jnp.int32, (tq, LANE), 1)
    o_ref[...] = jnp.where(lane < HEAD_DIM, outs[0], outs[1]).astype(o_ref.dtype)


def _fox(h, aq, ak, B, S, tq=512):
    T = B * S
    nq = S // tq
    return pl.pallas_call(
        _fox_kernel,
        grid=(B, 4, nq),
        in_specs=[pl.BlockSpec((tq, LANE), lambda b, p, i: (b * nq + i, COL_QF // LANE + p)),
                  pl.BlockSpec((tq, LANE), lambda b, p, i: (b * nq + i, p)),
                  pl.BlockSpec((S, LANE), lambda b, p, i: (b, COL_KF // LANE + p)),
                  pl.BlockSpec((S, LANE), lambda b, p, i: (b, p)),
                  pl.BlockSpec((S, LANE), lambda b, p, i: (b, COL_VF // LANE + p))],
        out_specs=pl.BlockSpec((tq, LANE), lambda b, p, i: (b * nq + i, p)),
        out_shape=jax.ShapeDtypeStruct((T, 4 * LANE), BF16),
        scratch_shapes=[pltpu.VMEM((2, tq, LANE), F32), pltpu.VMEM((2, tq, 2 * LANE), F32),
                        pltpu.VMEM((2, tq, tq), F32), pltpu.VMEM((2, tq, tq), F32)],
        compiler_params=_cparams(("parallel", "parallel", "arbitrary")),
        name="fox_attn",
    )(h, aq, h, ak, h)


def _swa_kernel(sink_ref, q_ref, kc_ref, kp_ref, vc_ref, vp_ref, o_ref):
    n = pl.program_id(1)
    blk = q_ref.shape[0]
    kb = jnp.concatenate([kp_ref[...], kc_ref[...]], axis=0)
    vb = jnp.concatenate([vp_ref[...], vc_ref[...]], axis=0)
    qi = lax.broadcasted_iota(jnp.int32, (blk, 2 * blk), 0)
    kj = lax.broadcasted_iota(jnp.int32, (blk, 2 * blk), 1)
    dist = qi + blk - kj
    valid = (dist >= 0) & (dist < WINDOW) & ((kj >= blk) | (n > 0))
    distf = dist.astype(F32)
    lane = lax.broadcasted_iota(jnp.int32, (blk, LANE), 1)
    lo = lane < HEAD_DIM
    heads = [(m, half) for m in range(4) for half in range(2)]
    logits = []
    for m, half in heads:
        hd = m + 4 * half
        slope = 2.0 ** (-8.0 * (hd + 1) / SWA_HEADS)
        qm = q_ref[:, m * LANE:(m + 1) * LANE]
        qh = jnp.where(lo if half == 0 else jnp.logical_not(lo), qm, jnp.zeros_like(qm))
        s = lax.dot_general(qh, kb, (((1,), (1,)), ((), ())), preferred_element_type=F32)
        logits.append(jnp.where(valid, s - slope * distf, NEG_BIG))
    probs = []
    for (m, half), s in zip(heads, logits):
        sink = sink_ref[m + 4 * half]
        mx = jnp.maximum(jnp.max(s, axis=1, keepdims=True), sink)
        e = jnp.exp(s - mx)
        den = jnp.sum(e, axis=1, keepdims=True) + jnp.exp(sink - mx)
        probs.append((e / den).astype(BF16))
    outs = [jnp.dot(p, vb, preferred_element_type=F32) for p in probs]
    for m in range(4):
        o_ref[:, m * LANE:(m + 1) * LANE] = jnp.where(lo, outs[2 * m], outs[2 * m + 1]).astype(o_ref.dtype)


def _swa(h, sinks, B, S, blk=128):
    T = B * S
    nb = S // blk
    cur = lambda c: (lambda b, n: (b * nb + n, c))
    prev = lambda c: (lambda b, n: (b * nb + jnp.maximum(n - 1, 0), c))
    return pl.pallas_call(
        _swa_kernel,
        grid=(B, nb),
        in_specs=[pl.BlockSpec(memory_space=pltpu.SMEM),
                  pl.BlockSpec((blk, 4 * LANE), cur(COL_QS // (4 * LANE))),
                  pl.BlockSpec((blk, LANE), cur(COL_KS // LANE)),
                  pl.BlockSpec((blk, LANE), prev(COL_KS // LANE)),
                  pl.BlockSpec((blk, LANE), cur(COL_VS // LANE)),
                  pl.BlockSpec((blk, LANE), prev(COL_VS // LANE))],
        out_specs=pl.BlockSpec((blk, 4 * LANE), lambda b, n: (b * nb + n, 0)),
        out_shape=jax.ShapeDtypeStruct((T, 4 * LANE), BF16),
        compiler_params=_cparams(("parallel", "arbitrary")),
        name="swa_attn",
    )(sinks, h, h, h, h, h)


def _pool_kernel(xc_ref, xp_ref, w_ref, sc_ref, o_ref):
    j = pl.program_id(1)
    ts = xc_ref.shape[0]
    hal = xp_ref.shape[0]
    r = lax.broadcasted_iota(jnp.int32, (ts, ts + hal), 0)
    c = lax.broadcasted_iota(jnp.int32, (ts, ts + hal), 1) - hal
    t_glob = (lax.broadcasted_iota(jnp.int32, (ts, LANE), 0) + j * ts + 1).astype(F32)
    has_prev = j > 0
    for g, w in enumerate(POOL_WINDOWS):
        xg = xc_ref[:, g * LANE:(g + 1) * LANE]
        xp = xp_ref[:, g * LANE:(g + 1) * LANE]
        xp = jnp.where(has_prev, xp, jnp.zeros_like(xp))
        ext = jnp.concatenate([xp, xg], axis=0)
        band = jnp.where((c <= r) & (c > r - w), 1.0, 0.0).astype(BF16)
        win = jnp.dot(band, ext, preferred_element_type=F32)
        cnt = jnp.minimum(t_glob, float(w))
        pooled = win / cnt - xg.astype(F32)
        y = jnp.dot(pooled.astype(BF16), w_ref[g], preferred_element_type=F32)
        o_ref[:, g * LANE:(g + 1) * LANE] = (y * sc_ref[:, g * LANE:(g + 1) * LANE]).astype(o_ref.dtype)


def _pool(h, pool_w, pool_scale, B, S, ts=512, hal=128):
    T = B * S
    nt = S // ts
    r = ts // hal
    return pl.pallas_call(
        _pool_kernel,
        grid=(B, nt),
        in_specs=[pl.BlockSpec((ts, POOL_W), lambda b, j: (b * nt + j, COL_POOL // POOL_W)),
                  pl.BlockSpec((hal, POOL_W),
                               lambda b, j: (jnp.maximum((b * nt + j) * r - 1, 0), COL_POOL // POOL_W)),
                  pl.BlockSpec((4, POOL_GW, POOL_GW), lambda b, j: (0, 0, 0)),
                  pl.BlockSpec((1, POOL_W), lambda b, j: (0, 0))],
        out_specs=pl.BlockSpec((ts, POOL_W), lambda b, j: (b * nt + j, 0)),
        out_shape=jax.ShapeDtypeStruct((T, POOL_W), BF16),
        compiler_params=_cparams(("parallel", "arbitrary")),
        name="ms_pool",
    )(h, h, pool_w, pool_scale)


CONV_PAD = 32
SUBLANES = 8


def _conv_kernel(uc_ref, up_ref, w_ref, b_ref, g_ref, bb_ref, o_ref, ext_ref, sh_ref):
    j = pl.program_id(1)
    ts = uc_ref.shape[0]
    pad = CONV_PAD

    def glu(u):
        u = u.astype(F32)
        return u[:, :CONV_W] * _sigmoid(u[:, CONV_W:])

    hp = glu(up_ref[up_ref.shape[0] - pad:, :])
    ext_ref[0:pad, :] = jnp.where(j > 0, hp, jnp.zeros_like(hp))
    ext_ref[pad:pad + ts, :] = glu(uc_ref[...])
    ext_ref[pad + ts:, :] = jnp.zeros((SUBLANES, CONV_W), F32)
    for r in range(SUBLANES):
        sh_ref[r] = ext_ref[r:r + ts + pad, :]
    acc = jnp.zeros((ts, CONV_W), F32)
    for k in range(CONV_K):
        off = pad - (CONV_K - 1) + k
        r = off % SUBLANES
        acc = acc + sh_ref[r, off - r:off - r + ts, :] * w_ref[k:k + 1, :]
    y = _layer_norm(acc + b_ref[...], g_ref[...], bb_ref[...])
    o_ref[...] = (y * _sigmoid(y)).astype(o_ref.dtype)


def _conv(h, dw_w, dw_b, ln_g, ln_b, B, S, ts=512, hal=128):
    T = B * S
    nt = S // ts
    r = ts // hal
    vec = pl.BlockSpec((1, CONV_W), lambda b, j: (0, 0))
    return pl.pallas_call(
        _conv_kernel,
        grid=(B, nt),
        in_specs=[pl.BlockSpec((ts, 2 * CONV_W), lambda b, j: (b * nt + j, COL_CONV // (2 * CONV_W))),
                  pl.BlockSpec((hal, 2 * CONV_W),
                               lambda b, j: (jnp.maximum((b * nt + j) * r - 1, 0), COL_CONV // (2 * CONV_W))),
                  pl.BlockSpec((32, CONV_W), lambda b, j: (0, 0)),
                  vec, vec, vec],
        out_specs=pl.BlockSpec((ts, CONV_W), lambda b, j: (b * nt + j, 0)),
        out_shape=jax.ShapeDtypeStruct((T, CONV_W), BF16),
        scratch_shapes=[pltpu.VMEM((ts + CONV_PAD + SUBLANES, CONV_W), F32),
                        pltpu.VMEM((SUBLANES, ts + CONV_PAD, CONV_W), F32)],
        compiler_params=_cparams(("parallel", "arbitrary")),
        name="conf_conv",
    )(h, h, dw_w, dw_b, ln_g, ln_b)


def _merge_kernel(x_ref, yf_ref, yp_ref, yc_ref, ys_ref, wg_ref, pf_ref, pp_ref, pc_ref, ps_ref,
                  wo_ref, g_ref, b_ref, x1_ref, x1t_ref):
    xb = x_ref[...].astype(BF16)
    merged = None
    for br, (y_ref, p_ref) in enumerate(((yf_ref, pf_ref), (yp_ref, pp_ref), (yc_ref, pc_ref), (ys_ref, ps_ref))):
        gate = _sigmoid(jnp.dot(xb, wg_ref[:, br * D_MODEL:(br + 1) * D_MODEL], preferred_element_type=F32))
        term = gate * jnp.dot(y_ref[...], p_ref[...], preferred_element_type=F32)
        merged = term if merged is None else merged + term
    mix = jnp.dot(merged.astype(BF16), wo_ref[...], preferred_element_type=F32)
    x1 = _layer_norm(ALPHA * x_ref[...] + mix, g_ref[...], b_ref[...])
    x1_ref[...] = x1
    x1t_ref[...] = x1.T.astype(BF16)


def _merge(x, yf, yp, yc, ys, wg, pf, pp, pc, ps, wo, g, b, tm=256):
    T = x.shape[0]
    const = lambda i: (0, 0)
    tok = lambda w: pl.BlockSpec((tm, w), lambda i: (i, 0))
    wspec = lambda a: pl.BlockSpec(a.shape, const)
    return pl.pallas_call(
        _merge_kernel,
        grid=(T // tm,),
        in_specs=[tok(D_MODEL), tok(512), tok(512), tok(512), tok(512),
                  wspec(wg), wspec(pf), wspec(pp), wspec(pc), wspec(ps), wspec(wo), wspec(g), wspec(b)],
        out_specs=[tok(D_MODEL), pl.BlockSpec((D_MODEL, tm), lambda i: (0, i))],
        out_shape=[jax.ShapeDtypeStruct((T, D_MODEL), F32), jax.ShapeDtypeStruct((D_MODEL, T), BF16)],
        compiler_params=_cparams(("parallel",)),
        name="merge_ln1",
    )(x, yf, yp, yc, ys, wg, pf, pp, pc, ps, wo, g, b)


def _wf_kernel(k_ref, wq_ref, o_ref):
    o_ref[...] = jnp.dot(k_ref[0], wq_ref[...], preferred_element_type=F32).astype(o_ref.dtype)


def _peer_score_weights(wq_t, keys):
    nblk = wq_t.shape[0] // N_KEYS
    return pl.pallas_call(
        _wf_kernel,
        grid=(nblk,),
        in_specs=[pl.BlockSpec((1, N_KEYS, N_KEYS), lambda j: (j % 2, 0, 0)),
                  pl.BlockSpec((N_KEYS, D_MODEL), lambda j: (j, 0))],
        out_specs=pl.BlockSpec((N_KEYS, D_MODEL), lambda j: (j, 0)),
        out_shape=jax.ShapeDtypeStruct(wq_t.shape, BF16),
        compiler_params=_cparams(("parallel",)),
        name="peer_wf",
    )(keys, wq_t)


N_RANK = PEER_TOPK + 1


def _n_cand():
    return [(r, c) for r in range(N_RANK) for c in range(N_RANK) if (r + 1) * (c + 1) <= N_RANK]


def _gelu(z):
    return 0.5 * z * (1.0 + lax.erf(z * (1.0 / math.sqrt(2.0))))


def _extract_top(work_ref, out_ref, n_slab, n_out):
    def round_(r, prev):
        m = jnp.full(prev.shape, NEG_BIG, F32)
        for s in range(n_slab):
            w = work_ref[s]
            m = jnp.maximum(m, jnp.where(w < prev, w, NEG_BIG))
        out_ref[r] = m
        return m
    lax.fori_loop(0, n_out, round_, jnp.full(work_ref.shape[1:], -NEG_BIG, F32))


def _count_below(s, thr):
    c8 = s > thr[7]
    c4 = s > jnp.where(c8, thr[11], thr[3])
    c2 = s > jnp.where(c8, jnp.where(c4, thr[13], thr[9]), jnp.where(c4, thr[5], thr[1]))
    lo = jnp.where(c4, jnp.where(c2, thr[6], thr[4]), jnp.where(c2, thr[2], thr[0]))
    hi = jnp.where(c4, jnp.where(c2, thr[14], thr[12]), jnp.where(c2, thr[10], thr[8]))
    c1 = s > jnp.where(c8, hi, lo)
    one = lambda cond, w: jnp.where(cond, w, 0.0)
    return one(c8, 8.0) + one(c4, 4.0) + one(c2, 2.0) + one(c1, 1.0) + one(s > thr[15], 1.0)


def _dup_bf16(v):
    u = pltpu.bitcast(v.astype(BF16).astype(F32), jnp.uint32)
    return u | (u >> 16)


def _row_bf16(slab, hh):
    return pltpu.bitcast(jnp.broadcast_to(slab[hh:hh + 1, :], (N_KEYS // 2, LANE)), BF16)


def _peer_kernel(xt_ref, wf_ref, u0_ref, u_ref, vt_ref, x1_ref, g_ref, b_ref, x2_ref,
                 n_scr, e1_scr, rank_scr, e2_scr, s2_scr, y_scr, wt0_scr, wt1_scr, ht0_scr, ht1_scr,
                 work_scr, top_scr, cand_scr, csel_scr, thr_scr):
    g = pl.program_id(1)
    ng = pl.num_programs(1)
    tm = xt_ref.shape[1]
    eb = u_ref.shape[0] // 2
    nh = PEER_HEADS
    rows = N_KEYS * nh

    @pl.when(g == 0)
    def _select():
        y_scr[...] = jnp.zeros_like(y_scr)
        ht0_scr[...] = jnp.dot(u0_ref[...], xt_ref[...], preferred_element_type=F32)
        st = jnp.dot(wf_ref[...], xt_ref[...], preferred_element_type=F32)
        for half in range(2):
            work_scr[...] = st[half * rows:(half + 1) * rows].reshape(N_KEYS, nh, tm)
            _extract_top(work_scr, top_scr.at[half], N_KEYS, N_RANK)
        cands = _n_cand()
        for ci, (r, c) in enumerate(cands):
            cand_scr[ci] = top_scr[0, r] + top_scr[1, c]
        _extract_top(cand_scr, csel_scr, len(cands), N_RANK)
        m0 = csel_scr[0]
        zsum = jnp.zeros_like(m0)
        for r in range(PEER_TOPK):
            zsum = zsum + jnp.exp(csel_scr[r] - m0)
        tau = 0.5 * (csel_scr[PEER_TOPK - 1] + csel_scr[PEER_TOPK])
        a0 = top_scr[0, 0]
        for c in range(PEER_TOPK):
            thr_scr[c] = tau - top_scr[1, c]
        work_scr[...] = st[0:rows].reshape(N_KEYS, nh, tm)

        def key_body(i, carry):
            s = work_scr[i]
            cnt = _count_below(s, [thr_scr[c] for c in range(PEER_TOPK)])
            r0 = pl.multiple_of(i * nh, nh)
            n_scr[pl.ds(r0, nh), :] = _dup_bf16(cnt)
            e1_scr[pl.ds(r0, nh), :] = _dup_bf16(jnp.exp(s - a0) / zsum)
            return carry
        lax.fori_loop(0, N_KEYS, key_body, 0)

        s2_scr[...] = st[2 * rows:3 * rows]
        grp = 16
        for hh in range(nh):
            brow = [top_scr[1, c][hh:hh + 1, :] for c in range(N_RANK)]
            mid = [0.5 * (brow[c] + brow[c + 1]) for c in range(PEER_TOPK)]
            mid_up = mid[::-1]

            def rank_body(jg, carry):
                r0 = pl.multiple_of(hh * N_KEYS + jg * grp, grp)
                s = s2_scr[pl.ds(r0, grp), :]
                cnt = float(PEER_TOPK) - _count_below(s, mid_up)
                rank_scr[pl.ds(r0, grp), :] = cnt.astype(BF16)
                e2_scr[pl.ds(r0, grp), :] = jnp.exp(s - brow[0]).astype(BF16)
                return carry
            lax.fori_loop(0, N_KEYS // grp, rank_body, 0)

    nsub = eb // N_KEYS

    def gate_block(sub, chunks):
        for ii in range(nsub):
            base = pl.multiple_of(((g * 2 + sub) * nsub + ii) * nh, nh)
            for c in chunks:
                ls = slice(c * LANE, (c + 1) * LANE)
                n_i = n_scr[pl.ds(base, nh), ls]
                e_i = e1_scr[pl.ds(base, nh), ls]
                acc = None
                for hh in range(nh):
                    rs = slice(hh * N_KEYS, (hh + 1) * N_KEYS)
                    term = jnp.where(rank_scr[rs, ls] < _row_bf16(n_i, hh), e2_scr[rs, ls],
                                     jnp.zeros((), BF16)) * _row_bf16(e_i, hh)
                    acc = term if acc is None else acc + term
                act = _gelu(ht_scr[sub][ii * N_KEYS:(ii + 1) * N_KEYS, ls]).astype(BF16)
                wt_scr[sub][ii * N_KEYS:(ii + 1) * N_KEYS, ls] = act * acc

    ht_scr = (ht0_scr, ht1_scr)
    wt_scr = (wt0_scr, wt1_scr)
    half = tm // 2
    cpl = half // LANE
    for sub in range(2):
        for hf in range(2):
            ln = slice(hf * half, (hf + 1) * half)
            ht_scr[1 - sub][:, ln] = jnp.dot(u_ref[sub * eb:(sub + 1) * eb, :], xt_ref[:, ln],
                                             preferred_element_type=F32)
            gate_block(sub, range(hf * cpl, (hf + 1) * cpl))
            y_scr[:, ln] += jnp.dot(vt_ref[:, sub * eb:(sub + 1) * eb], wt_scr[sub][:, ln],
                                    preferred_element_type=F32)

    @pl.when(g == ng - 1)
    def _finish():
        z = ALPHA * x1_ref[...] + y_scr[...].T
        x2 = _layer_norm(z, g_ref[...], b_ref[...])
        x2_ref[...] = x2


def _peer(x1t, wf, u, v, x1, g, b, tm=512, eb=256):
    T = x1.shape[0]
    ne = u.shape[0]
    ncand = len(_n_cand())
    rows = N_KEYS * PEER_HEADS
    const = lambda t, k: (0, 0)
    slab = lambda n: pltpu.VMEM((n, PEER_HEADS, tm), F32)
    u_roll = jnp.roll(u, -eb, axis=0)
    vt = v.reshape(ne // (2 * eb), 2 * eb, D_MODEL).transpose(0, 2, 1)
    return pl.pallas_call(
        _peer_kernel,
        grid=(T // tm, ne // (2 * eb)),
        in_specs=[pl.BlockSpec((D_MODEL, tm), lambda t, k: (0, t)),
                  pl.BlockSpec(wf.shape, const),
                  pl.BlockSpec((eb, D_MODEL), const),
                  pl.BlockSpec((2 * eb, D_MODEL), lambda t, k: (k, 0)),
                  pl.BlockSpec((None, D_MODEL, 2 * eb), lambda t, k: (k, 0, 0)),
                  pl.BlockSpec((tm, D_MODEL), lambda t, k: (t, 0)),
                  pl.BlockSpec((1, D_MODEL), const),
                  pl.BlockSpec((1, D_MODEL), const)],
        out_specs=pl.BlockSpec((tm, D_MODEL), lambda t, k: (t, 0)),
        out_shape=jax.ShapeDtypeStruct((T, D_MODEL), F32),
        scratch_shapes=[pltpu.VMEM((rows, tm), jnp.uint32), pltpu.VMEM((rows, tm), jnp.uint32),
                        pltpu.VMEM((rows, tm), BF16), pltpu.VMEM((rows, tm), BF16),
                        pltpu.VMEM((rows, tm), F32),
                        pltpu.VMEM((D_MODEL, tm), F32),
                        pltpu.VMEM((eb, tm), BF16), pltpu.VMEM((eb, tm), BF16),
                        pltpu.VMEM((eb, tm), F32), pltpu.VMEM((eb, tm), F32),
                        slab(N_KEYS), pltpu.VMEM((2, N_RANK, PEER_HEADS, tm), F32),
                        slab(ncand), slab(N_RANK), slab(N_RANK)],
        compiler_params=_cparams(("parallel", "arbitrary")),
        name="peer_ln2",
    )(x1t, wf, u, u_roll, vt, x1, g, b)


def _prep_w_in(w_in):
    sizes = (512, 512, 512, 8, 512, 128, 128, 512, 1024, 4096)
    offs = np.cumsum((0,) + sizes)
    q_f, k_f, v_f, f_l, q_s, k_s, v_s, x_pool, x_conv, gl = (w_in[:, offs[i]:offs[i + 1]] for i in range(10))
    scale = HEAD_DIM ** -0.5
    q_s = q_s.reshape(D_MODEL, 2, 4, HEAD_DIM).transpose(0, 2, 1, 3).reshape(D_MODEL, 512)
    zeros = lambda n: jnp.zeros((D_MODEL, n), w_in.dtype)
    w_h = jnp.concatenate([q_f * scale, k_f, v_f, q_s * scale, x_pool, k_s, v_s, f_l, zeros(120), zeros(128), x_conv],
                          axis=1)
    return w_h.astype(BF16), gl.astype(BF16)


def _prep_peer(wq, k1, k2):
    wq_t = wq.T.astype(BF16)
    keys = jnp.stack([k1, k2]).astype(BF16)
    wf = _peer_score_weights(wq_t, keys)
    wf = wf.reshape(PEER_HEADS, 2, N_KEYS, D_MODEL)
    kh = lambda half: wf[:, half].transpose(1, 0, 2).reshape(N_KEYS * PEER_HEADS, D_MODEL)
    hk = wf[:, 1].reshape(PEER_HEADS * N_KEYS, D_MODEL)
    return jnp.concatenate([kh(0), kh(1), hk], axis=0)


def _pad_lanes(v, n):
    return jnp.zeros((1, n), F32).at[0, :v.shape[0]].set(v.astype(F32))


def kernel(x, w_in, b_f, swa_sinks, pool_w, pool_scale, dw_w, dw_b, conv_ln_g, conv_ln_b, p_fox, p_swa, p_pool,
           p_conv, w_out, ln1_g, ln1_b, peer_wq, peer_k1, peer_k2, peer_u, peer_v, ln2_g, ln2_b):
    B, S, D = x.shape
    T = B * S
    xf = x.reshape(T, D)
    row = lambda v: v.reshape(1, -1).astype(F32)
    for l in range(DEPTH):
        w_h, w_gate = _prep_w_in(w_in[l])
        h = _inproj(xf, w_h)
        aq, ak = _decay(h, _pad_lanes(b_f[l], LANE), B, S)
        y_fox = _fox(h, aq, ak, B, S)
        y_swa = _swa(h, swa_sinks[l].astype(F32), B, S)
        y_pool = _pool(h, pool_w[l].astype(BF16), row(pool_scale[l]), B, S)
        dw = jnp.zeros((32, CONV_W), F32).at[:CONV_K].set(dw_w[l])
        y_conv = _conv(h, dw, row(dw_b[l]), row(conv_ln_g[l]), row(conv_ln_b[l]), B, S)
        ps = p_swa[l].reshape(2, 4, HEAD_DIM, D).transpose(1, 0, 2, 3).reshape(512, D)
        x1, x1t = _merge(xf, y_fox, y_pool, y_conv, y_swa, w_gate, p_fox[l].astype(BF16),
                         p_pool[l].astype(BF16), p_conv[l].astype(BF16), ps.astype(BF16),
                         w_out[l].astype(BF16), row(ln1_g[l]), row(ln1_b[l]))
        wf = _prep_peer(peer_wq[l], peer_k1[l], peer_k2[l])
        xf = _peer(x1t, wf, peer_u[l].astype(BF16), peer_v[l].astype(BF16), x1,
                       row(ln2_g[l]), row(ln2_b[l]))
    return xf.reshape(B, S, D)
```

```python
import functools
import math

import numpy as np
import jax
import jax.numpy as jnp
from jax import lax
from jax.experimental import pallas as pl
from jax.experimental.pallas import tpu as pltpu

F32 = jnp.float32
BF16 = jnp.bfloat16

D_MODEL = 1024
DEPTH = 2
FOX_HEADS = 8
HEAD_DIM = 64
SWA_HEADS = 8
SWA_KV = 2
WINDOW = 128
POOL_WINDOWS = (2, 4, 8, 16)
POOL_GW = 128
POOL_W = 512
CONV_W = 512
CONV_K = 31
N_BRANCH = 4
PEER_HEADS = 8
N_KEYS = 128
N_EXPERTS = N_KEYS * N_KEYS
PEER_TOPK = 16
LN_EPS = 1e-5
ALPHA = (2 * DEPTH) ** 0.25
NEG_BIG = -1e30

COL_QF, COL_KF, COL_VF, COL_QS, COL_POOL, COL_KS, COL_VS, COL_FL, COL_CONV = (
    0, 512, 1024, 1536, 2048, 2560, 2688, 2816, 3072)
H_COLS = 4096
LANE = 128
AUG_A = 6
AUG_B = 12

VMEM_LIMIT = 56 * 1024 * 1024


def _cparams(sem):
    return pltpu.CompilerParams(dimension_semantics=sem, vmem_limit_bytes=VMEM_LIMIT)


def _layer_norm(z, g, b):
    mu = jnp.mean(z, axis=-1, keepdims=True)
    zc = z - mu
    var = jnp.mean(zc * zc, axis=-1, keepdims=True)
    return zc * lax.rsqrt(var + LN_EPS) * g + b


def _sigmoid(z):
    return 1.0 / (1.0 + jnp.exp(-z))


def _inproj_kernel(x_ref, w_ref, o_ref, *, n_chunk):
    x = x_ref[...].astype(BF16)
    for c in range(0, o_ref.shape[1], n_chunk):
        o_ref[:, c:c + n_chunk] = jnp.dot(
            x, w_ref[:, c:c + n_chunk], preferred_element_type=F32).astype(o_ref.dtype)


def _inproj(x, w, tm=512):
    T, K = x.shape
    N = w.shape[1]
    return pl.pallas_call(
        functools.partial(_inproj_kernel, n_chunk=1024),
        grid=(T // tm,),
        in_specs=[pl.BlockSpec((tm, K), lambda i: (i, 0)),
                  pl.BlockSpec((K, N), lambda i: (0, 0))],
        out_specs=pl.BlockSpec((tm, N), lambda i: (i, 0)),
        out_shape=jax.ShapeDtypeStruct((T, N), BF16),
        compiler_params=_cparams(("parallel",)),
        name="inproj",
    )(x, w)


def _split3(v):
    hi = v.astype(BF16)
    r1 = v - hi.astype(F32)
    mid = r1.astype(BF16)
    r2 = r1 - mid.astype(F32)
    return hi, mid, r2.astype(BF16)


def _decay_kernel(fl_ref, bf_ref, selq_ref, selk_ref, cq_ref, ck_ref, aq_ref, ak_ref, carry_ref):
    ts = fl_ref.shape[0]

    @pl.when(pl.program_id(1) == 0)
    def _():
        carry_ref[...] = jnp.zeros_like(carry_ref)

    z = fl_ref[...].astype(F32) + bf_ref[...]
    ls = jnp.minimum(z, 0.0) - jnp.log1p(jnp.exp(-jnp.abs(z)))
    row = lax.broadcasted_iota(jnp.int32, (ts, ts), 0)
    col = lax.broadcasted_iota(jnp.int32, (ts, ts), 1)
    tri = jnp.where(col <= row, 1.0, 0.0).astype(BF16)
    parts = jnp.concatenate(_split3(ls), axis=1)
    cs = jnp.dot(tri, parts, preferred_element_type=F32)
    c = cs[:, :LANE] + cs[:, LANE:2 * LANE] + cs[:, 2 * LANE:] + carry_ref[0:1, :]
    carry_ref[...] = jnp.broadcast_to(c[ts - 1:ts, :], carry_ref.shape)
    cparts = jnp.concatenate(_split3(c), axis=1)
    aq_ref[...] = (jnp.dot(cparts, selq_ref[...], preferred_element_type=F32) + cq_ref[...]).astype(BF16)
    ak_ref[...] = (jnp.dot(cparts, selk_ref[...], preferred_element_type=F32) + ck_ref[...]).astype(BF16)


def _decay_consts():
    selq = np.zeros((3 * LANE, 4 * LANE), np.float32)
    selk = np.zeros((3 * LANE, 4 * LANE), np.float32)
    cq = np.zeros((1, 4 * LANE), np.float32)
    ck = np.zeros((1, 4 * LANE), np.float32)
    for p in range(4):
        for part in range(3):
            for hh in range(2):
                base = p * LANE + hh * AUG_A
                selq[part * LANE + 2 * p + hh, base + part] = 1.0
                cq[0, base + 3 + part] = 1.0
                ck[0, base + part] = 1.0
                selk[part * LANE + 2 * p + hh, base + 3 + part] = -1.0
    return (jnp.asarray(selq, BF16), jnp.asarray(selk, BF16), jnp.asarray(cq), jnp.asarray(ck))


def _decay(h, bf_pad, B, S, ts=512):
    T = B * S
    nt = S // ts
    selq, selk, cq, ck = _decay_consts()
    const = lambda b, j: (0, 0)
    return pl.pallas_call(
        _decay_kernel,
        grid=(B, nt),
        in_specs=[pl.BlockSpec((ts, LANE), lambda b, j: (b * nt + j, COL_FL // LANE)),
                  pl.BlockSpec((1, LANE), const),
                  pl.BlockSpec((3 * LANE, 4 * LANE), const),
                  pl.BlockSpec((3 * LANE, 4 * LANE), const),
                  pl.BlockSpec((1, 4 * LANE), const),
                  pl.BlockSpec((1, 4 * LANE), const)],
        out_specs=[pl.BlockSpec((ts, 4 * LANE), lambda b, j: (b * nt + j, 0)),
                   pl.BlockSpec((ts, 4 * LANE), lambda b, j: (b * nt + j, 0))],
        out_shape=[jax.ShapeDtypeStruct((T, 4 * LANE), BF16)] * 2,
        scratch_shapes=[pltpu.VMEM((8, LANE), F32)],
        compiler_params=_cparams(("parallel", "arbitrary")),
        name="fox_decay",
    )(h, bf_pad, selq, selk, cq, ck)


def _fox_kernel(q_ref, aq_ref, k_ref, ak_ref, v_ref, o_ref, m_scr, acc_scr, s0_scr, s1_scr):
    tq = q_ref.shape[0]
    qi = pl.program_id(2)
    lane2 = lax.broadcasted_iota(jnp.int32, (1, 2 * LANE), 1)
    head_mask = (
        (lane2 < HEAD_DIM) | ((lane2 >= LANE) & (lane2 < LANE + AUG_A)),
        ((lane2 >= HEAD_DIM) & (lane2 < LANE)) | ((lane2 >= LANE + AUG_A) & (lane2 < LANE + AUG_B)),
    )
    qf = jnp.concatenate([q_ref[...], aq_ref[...]], axis=1)
    qs = [jnp.where(mk, qf, jnp.zeros_like(qf)) for mk in head_mask]
    ones_col = jnp.where(lax.broadcasted_iota(jnp.int32, (tq, LANE), 1) == 0, 1.0, 0.0).astype(BF16)
    row = lax.broadcasted_iota(jnp.int32, (tq, tq), 0)
    col = lax.broadcasted_iota(jnp.int32, (tq, tq), 1)

    m_scr[...] = jnp.full(m_scr.shape, NEG_BIG, F32)
    acc_scr[...] = jnp.zeros(acc_scr.shape, F32)

    def logits(j, dst):
        off = pl.multiple_of(j * tq, tq)
        kf = jnp.concatenate([k_ref[pl.ds(off, tq), :], ak_ref[pl.ds(off, tq), :]], axis=1)
        for x in range(2):
            dst[x] = lax.dot_general(qs[x], kf, (((1,), (1,)), ((), ())), preferred_element_type=F32)

    def accumulate(j, src, masked):
        off = pl.multiple_of(j * tq, tq)
        vf = jnp.concatenate([v_ref[pl.ds(off, tq), :], ones_col], axis=1)
        for x in range(2):
            s = src[x]
            if masked:
                s = jnp.where(col <= row, s, NEG_BIG)
            m_prev = m_scr[x]
            m_new = jnp.maximum(m_prev, jnp.max(s, axis=1, keepdims=True))
            alpha = jnp.exp(m_prev - m_new)
            p = jnp.exp(s - jnp.concatenate([m_new] * (tq // LANE), axis=1))
            acc_scr[x] = (acc_scr[x] * jnp.concatenate([alpha, alpha], axis=1)
                          + jnp.dot(p.astype(BF16), vf, preferred_element_type=F32))
            m_scr[x] = m_new

    logits(0, s0_scr)

    def body(jj, carry):
        j = 2 * jj
        logits(j + 1, s1_scr)
        accumulate(j, s0_scr, False)
        logits(j + 2, s0_scr)
        accumulate(j + 1, s1_scr, False)
        return carry

    lax.fori_loop(0, qi // 2, body, 0)

    @pl.when(qi % 2 == 0)
    def _():
        accumulate(qi, s0_scr, True)

    @pl.when(qi % 2 == 1)
    def _():
        logits(qi, s1_scr)
        accumulate(qi - 1, s0_scr, False)
        accumulate(qi, s1_scr, True)

    outs = []
    for x in range(2):
        acc = acc_scr[x]
        outs.append(acc[:, :LANE] / acc[:, LANE:LANE + 1])
    lane = lax.broadcasted_iota(jnp.int32, (tq, LANE), 1)
    o_ref[...] = jnp.where(lane < HEAD_DIM, outs[0], outs[1]).astype(o_ref.dtype)


def _fox(h, aq, ak, B, S, tq=512):
    T = B * S
    nq = S // tq
    return pl.pallas_call(
        _fox_kernel,
        grid=(B, 4, nq),
        in_specs=[pl.BlockSpec((tq, LANE), lambda b, p, i: (b * nq + i, COL_QF // LANE + p)),
                  pl.BlockSpec((tq, LANE), lambda b, p, i: (b * nq + i, p)),
                  pl.BlockSpec((S, LANE), lambda b, p, i: (b, COL_KF // LANE + p)),
                  pl.BlockSpec((S, LANE), lambda b, p, i: (b, p)),
                  pl.BlockSpec((S, LANE), lambda b, p, i: (b, COL_VF // LANE + p))],
        out_specs=pl.BlockSpec((tq, LANE), lambda b, p, i: (b * nq + i, p)),
        out_shape=jax.ShapeDtypeStruct((T, 4 * LANE), BF16),
        scratch_shapes=[pltpu.VMEM((2, tq, LANE), F32), pltpu.VMEM((2, tq, 2 * LANE), F32),
                        pltpu.VMEM((2, tq, tq), F32), pltpu.VMEM((2, tq, tq), F32)],
        compiler_params=_cparams(("parallel", "parallel", "arbitrary")),
        name="fox_attn",
    )(h, aq, h, ak, h)


def _swa_kernel(sink_ref, q_ref, kc_ref, kp_ref, vc_ref, vp_ref, o_ref):
    n = pl.program_id(1)
    blk = q_ref.shape[0]
    kb = jnp.concatenate([kp_ref[...], kc_ref[...]], axis=0)
    vb = jnp.concatenate([vp_ref[...], vc_ref[...]], axis=0)
    qi = lax.broadcasted_iota(jnp.int32, (blk, 2 * blk), 0)
    kj = lax.broadcasted_iota(jnp.int32, (blk, 2 * blk), 1)
    dist = qi + blk - kj
    valid = (dist >= 0) & (dist < WINDOW) & ((kj >= blk) | (n > 0))
    distf = dist.astype(F32)
    lane = lax.broadcasted_iota(jnp.int32, (blk, LANE), 1)
    lo = lane < HEAD_DIM
    heads = [(m, half) for m in range(4) for half in range(2)]
    logits = []
    for m, half in heads:
        hd = m + 4 * half
        slope = 2.0 ** (-8.0 * (hd + 1) / SWA_HEADS)
        qm = q_ref[:, m * LANE:(m + 1) * LANE]
        qh = jnp.where(lo if half == 0 else jnp.logical_not(lo), qm, jnp.zeros_like(qm))
        s = lax.dot_general(qh, kb, (((1,), (1,)), ((), ())), preferred_element_type=F32)
        logits.append(jnp.where(valid, s - slope * distf, NEG_BIG))
    probs = []
    for (m, half), s in zip(heads, logits):
        sink = sink_ref[m + 4 * half]
        mx = jnp.maximum(jnp.max(s, axis=1, keepdims=True), sink)
        e = jnp.exp(s - mx)
        den = jnp.sum(e, axis=1, keepdims=True) + jnp.exp(sink - mx)
        probs.append((e / den).astype(BF16))
    outs = [jnp.dot(p, vb, preferred_element_type=F32) for p in probs]
    for m in range(4):
        o_ref[:, m * LANE:(m + 1) * LANE] = jnp.where(lo, outs[2 * m], outs[2 * m + 1]).astype(o_ref.dtype)


def _swa(h, sinks, B, S, blk=128):
    T = B * S
    nb = S // blk
    cur = lambda c: (lambda b, n: (b * nb + n, c))
    prev = lambda c: (lambda b, n: (b * nb + jnp.maximum(n - 1, 0), c))
    return pl.pallas_call(
        _swa_kernel,
        grid=(B, nb),
        in_specs=[pl.BlockSpec(memory_space=pltpu.SMEM),
                  pl.BlockSpec((blk, 4 * LANE), cur(COL_QS // (4 * LANE))),
                  pl.BlockSpec((blk, LANE), cur(COL_KS // LANE)),
                  pl.BlockSpec((blk, LANE), prev(COL_KS // LANE)),
                  pl.BlockSpec((blk, LANE), cur(COL_VS // LANE)),
                  pl.BlockSpec((blk, LANE), prev(COL_VS // LANE))],
        out_specs=pl.BlockSpec((blk, 4 * LANE), lambda b, n: (b * nb + n, 0)),
        out_shape=jax.ShapeDtypeStruct((T, 4 * LANE), BF16),
        compiler_params=_cparams(("parallel", "arbitrary")),
        name="swa_attn",
    )(sinks, h, h, h, h, h)


def _pool_kernel(xc_ref, xp_ref, w_ref, sc_ref, o_ref):
    j = pl.program_id(1)
    ts = xc_ref.shape[0]
    hal = xp_ref.shape[0]
    r = lax.broadcasted_iota(jnp.int32, (ts, ts + hal), 0)
    c = lax.broadcasted_iota(jnp.int32, (ts, ts + hal), 1) - hal
    t_glob = (lax.broadcasted_iota(jnp.int32, (ts, LANE), 0) + j * ts + 1).astype(F32)
    has_prev = j > 0
    for g, w in enumerate(POOL_WINDOWS):
        xg = xc_ref[:, g * LANE:(g + 1) * LANE]
        xp = xp_ref[:, g * LANE:(g + 1) * LANE]
        xp = jnp.where(has_prev, xp, jnp.zeros_like(xp))
        ext = jnp.concatenate([xp, xg], axis=0)
        band = jnp.where((c <= r) & (c > r - w), 1.0, 0.0).astype(BF16)
        win = jnp.dot(band, ext, preferred_element_type=F32)
        cnt = jnp.minimum(t_glob, float(w))
        pooled = win / cnt - xg.astype(F32)
        y = jnp.dot(pooled.astype(BF16), w_ref[g], preferred_element_type=F32)
        o_ref[:, g * LANE:(g + 1) * LANE] = (y * sc_ref[:, g * LANE:(g + 1) * LANE]).astype(o_ref.dtype)


def _pool(h, pool_w, pool_scale, B, S, ts=512, hal=128):
    T = B * S
    nt = S // ts
    r = ts // hal
    return pl.pallas_call(
        _pool_kernel,
        grid=(B, nt),
        in_specs=[pl.BlockSpec((ts, POOL_W), lambda b, j: (b * nt + j, COL_POOL // POOL_W)),
                  pl.BlockSpec((hal, POOL_W),
                               lambda b, j: (jnp.maximum((b * nt + j) * r - 1, 0), COL_POOL // POOL_W)),
                  pl.BlockSpec((4, POOL_GW, POOL_GW), lambda b, j: (0, 0, 0)),
                  pl.BlockSpec((1, POOL_W), lambda b, j: (0, 0))],
        out_specs=pl.BlockSpec((ts, POOL_W), lambda b, j: (b * nt + j, 0)),
        out_shape=jax.ShapeDtypeStruct((T, POOL_W), BF16),
        compiler_params=_cparams(("parallel", "arbitrary")),
        name="ms_pool",
    )(h, h, pool_w, pool_scale)


CONV_PAD = 32
SUBLANES = 8


def _conv_kernel(uc_ref, up_ref, w_ref, b_ref, g_ref, bb_ref, o_ref, ext_ref, sh_ref):
    j = pl.program_id(1)
    ts = uc_ref.shape[0]
    pad = CONV_PAD

    def glu(u):
        u = u.astype(F32)
        return u[:, :CONV_W] * _sigmoid(u[:, CONV_W:])

    hp = glu(up_ref[up_ref.shape[0] - pad:, :])
    ext_ref[0:pad, :] = jnp.where(j > 0, hp, jnp.zeros_like(hp))
    ext_ref[pad:pad + ts, :] = glu(uc_ref[...])
    ext_ref[pad + ts:, :] = jnp.zeros((SUBLANES, CONV_W), F32)
    for r in range(SUBLANES):
        sh_ref[r] = ext_ref[r:r + ts + pad, :]
    acc = jnp.zeros((ts, CONV_W), F32)
    for k in range(CONV_K):
        off = pad - (CONV_K - 1) + k
        r = off % SUBLANES
        acc = acc + sh_ref[r, off - r:off - r + ts, :] * w_ref[k:k + 1, :]
    y = _layer_norm(acc + b_ref[...], g_ref[...], bb_ref[...])
    o_ref[...] = (y * _sigmoid(y)).astype(o_ref.dtype)


def _conv(h, dw_w, dw_b, ln_g, ln_b, B, S, ts=512, hal=128):
    T = B * S
    nt = S // ts
    r = ts // hal
    vec = pl.BlockSpec((1, CONV_W), lambda b, j: (0, 0))
    return pl.pallas_call(
        _conv_kernel,
        grid=(B, nt),
        in_specs=[pl.BlockSpec((ts, 2 * CONV_W), lambda b, j: (b * nt + j, COL_CONV // (2 * CONV_W))),
                  pl.BlockSpec((hal, 2 * CONV_W),
                               lambda b, j: (jnp.maximum((b * nt + j) * r - 1, 0), COL_CONV // (2 * CONV_W))),
                  pl.BlockSpec((32, CONV_W), lambda b, j: (0, 0)),
                  vec, vec, vec],
        out_specs=pl.BlockSpec((ts, CONV_W), lambda b, j: (b * nt + j, 0)),
        out_shape=jax.ShapeDtypeStruct((T, CONV_W), BF16),
        scratch_shapes=[pltpu.VMEM((ts + CONV_PAD + SUBLANES, CONV_W), F32),
                        pltpu.VMEM((SUBLANES, ts + CONV_PAD, CONV_W), F32)],
        compiler_params=_cparams(("parallel", "arbitrary")),
        name="conf_conv",
    )(h, h, dw_w, dw_b, ln_g, ln_b)


def _merge_kernel(x_ref, yf_ref, yp_ref, yc_ref, ys_ref, wg_ref, pf_ref, pp_ref, pc_ref, ps_ref,
                  wo_ref, g_ref, b_ref, x1_ref, x1t_ref):
    xb = x_ref[...].astype(BF16)
    merged = None
    for br, (y_ref, p_ref) in enumerate(((yf_ref, pf_ref), (yp_ref, pp_ref), (yc_ref, pc_ref), (ys_ref, ps_ref))):
        gate = _sigmoid(jnp.dot(xb, wg_ref[:, br * D_MODEL:(br + 1) * D_MODEL], preferred_element_type=F32))
        term = gate * jnp.dot(y_ref[...], p_ref[...], preferred_element_type=F32)
        merged = term if merged is None else merged + term
    mix = jnp.dot(merged.astype(BF16), wo_ref[...], preferred_element_type=F32)
    x1 = _layer_norm(ALPHA * x_ref[...] + mix, g_ref[...], b_ref[...])
    x1_ref[...] = x1
    x1t_ref[...] = x1.T.astype(BF16)


def _merge(x, yf, yp, yc, ys, wg, pf, pp, pc, ps, wo, g, b, tm=256):
    T = x.shape[0]
    const = lambda i: (0, 0)
    tok = lambda w: pl.BlockSpec((tm, w), lambda i: (i, 0))
    wspec = lambda a: pl.BlockSpec(a.shape, const)
    return pl.pallas_call(
        _merge_kernel,
        grid=(T // tm,),
        in_specs=[tok(D_MODEL), tok(512), tok(512), tok(512), tok(512),
                  wspec(wg), wspec(pf), wspec(pp), wspec(pc), wspec(ps), wspec(wo), wspec(g), wspec(b)],
        out_specs=[tok(D_MODEL), pl.BlockSpec((D_MODEL, tm), lambda i: (0, i))],
        out_shape=[jax.ShapeDtypeStruct((T, D_MODEL), F32), jax.ShapeDtypeStruct((D_MODEL, T), BF16)],
        compiler_params=_cparams(("parallel",)),
        name="merge_ln1",
    )(x, yf, yp, yc, ys, wg, pf, pp, pc, ps, wo, g, b)


def _wf_kernel(k_ref, wq_ref, o_ref):
    o_ref[...] = jnp.dot(k_ref[0], wq_ref[...], preferred_element_type=F32).astype(o_ref.dtype)


def _peer_score_weights(wq_t, keys):
    nblk = wq_t.shape[0] // N_KEYS
    return pl.pallas_call(
        _wf_kernel,
        grid=(nblk,),
        in_specs=[pl.BlockSpec((1, N_KEYS, N_KEYS), lambda j: (j % 2, 0, 0)),
                  pl.BlockSpec((N_KEYS, D_MODEL), lambda j: (j, 0))],
        out_specs=pl.BlockSpec((N_KEYS, D_MODEL), lambda j: (j, 0)),
        out_shape=jax.ShapeDtypeStruct(wq_t.shape, BF16),
        compiler_params=_cparams(("parallel",)),
        name="peer_wf",
    )(keys, wq_t)


N_RANK = PEER_TOPK + 1


def _n_cand():
    return [(r, c) for r in range(N_RANK) for c in range(N_RANK) if (r + 1) * (c + 1) <= N_RANK]


def _gelu(z):
    return 0.5 * z * (1.0 + lax.erf(z * (1.0 / math.sqrt(2.0))))


def _extract_top(work_ref, out_ref, n_slab, n_out):
    nh, tm = work_ref.shape[1:]
    unroll = 8 if n_slab % 8 == 0 else 4
    assert n_slab % unroll == 0
    for c in range(tm // LANE):
        ls = slice(c * LANE, (c + 1) * LANE)

        def insert(it, best):
            best = list(best)
            for k in range(unroll):
                w = work_ref[it * unroll + k, :, ls]
                for r in range(n_out):
                    best[r], w = jnp.maximum(best[r], w), jnp.minimum(best[r], w)
            return tuple(best)

        best = lax.fori_loop(0, n_slab // unroll, insert,
                             tuple(jnp.full((nh, LANE), NEG_BIG, F32) for _ in range(n_out)))
        for r in range(n_out):
            out_ref[r, :, ls] = best[r]


def _count_below(s, thr):
    c8 = s > thr[7]
    c4 = s > jnp.where(c8, thr[11], thr[3])
    c2 = s > jnp.where(c8, jnp.where(c4, thr[13], thr[9]), jnp.where(c4, thr[5], thr[1]))
    lo = jnp.where(c4, jnp.where(c2, thr[6], thr[4]), jnp.where(c2, thr[2], thr[0]))
    hi = jnp.where(c4, jnp.where(c2, thr[14], thr[12]), jnp.where(c2, thr[10], thr[8]))
    c1 = s > jnp.where(c8, hi, lo)
    one = lambda cond, w: jnp.where(cond, w, 0.0)
    return one(c8, 8.0) + one(c4, 4.0) + one(c2, 2.0) + one(c1, 1.0) + one(s > thr[15], 1.0)


def _dup_bf16(v):
    u = pltpu.bitcast(v.astype(BF16).astype(F32), jnp.uint32)
    return u | (u >> 16)


def _row_bf16(slab, hh):
    return pltpu.bitcast(jnp.broadcast_to(slab[hh:hh + 1, :], (N_KEYS // 2, LANE)), BF16)


def _peer_kernel(xt_ref, wf_ref, u0_ref, u_ref, vt_ref, x1_ref, g_ref, b_ref, x2_ref,
                 n_scr, e1_scr, rank_scr, e2_scr, s2_scr, y_scr, wt0_scr, wt1_scr, ht0_scr, ht1_scr,
                 work_scr, top_scr, cand_scr, csel_scr, thr_scr):
    g = pl.program_id(1)
    ng = pl.num_programs(1)
    tm = xt_ref.shape[1]
    eb = u_ref.shape[0] // 2
    nh = PEER_HEADS
    rows = N_KEYS * nh

    @pl.when(g == 0)
    def _select():
        y_scr[...] = jnp.zeros_like(y_scr)
        ht0_scr[...] = jnp.dot(u0_ref[...], xt_ref[...], preferred_element_type=F32)
        st = jnp.dot(wf_ref[...], xt_ref[...], preferred_element_type=F32)
        for half in range(2):
            work_scr[...] = st[half * rows:(half + 1) * rows].reshape(N_KEYS, nh, tm)
            _extract_top(work_scr, top_scr.at[half], N_KEYS, N_RANK)
        cands = _n_cand()
        for ci, (r, c) in enumerate(cands):
            cand_scr[ci] = top_scr[0, r] + top_scr[1, c]
        _extract_top(cand_scr, csel_scr, len(cands), N_RANK)
        m0 = csel_scr[0]
        zsum = jnp.zeros_like(m0)
        for r in range(PEER_TOPK):
            zsum = zsum + jnp.exp(csel_scr[r] - m0)
        tau = 0.5 * (csel_scr[PEER_TOPK - 1] + csel_scr[PEER_TOPK])
        a0 = top_scr[0, 0]
        for c in range(PEER_TOPK):
            thr_scr[c] = tau - top_scr[1, c]
        work_scr[...] = st[0:rows].reshape(N_KEYS, nh, tm)

        def key_body(i, carry):
            s = work_scr[i]
            cnt = _count_below(s, [thr_scr[c] for c in range(PEER_TOPK)])
            r0 = pl.multiple_of(i * nh, nh)
            n_scr[pl.ds(r0, nh), :] = _dup_bf16(cnt)
            e1_scr[pl.ds(r0, nh), :] = _dup_bf16(jnp.exp(s - a0) / zsum)
            return carry
        lax.fori_loop(0, N_KEYS, key_body, 0)

        s2_scr[...] = st[2 * rows:3 * rows]
        grp = 16
        for hh in range(nh):
            brow = [top_scr[1, c][hh:hh + 1, :] for c in range(N_RANK)]
            mid = [0.5 * (brow[c] + brow[c + 1]) for c in range(PEER_TOPK)]
            mid_up = mid[::-1]

            def rank_body(jg, carry):
                r0 = pl.multiple_of(hh * N_KEYS + jg * grp, grp)
                s = s2_scr[pl.ds(r0, grp), :]
                cnt = float(PEER_TOPK) - _count_below(s, mid_up)
                rank_scr[pl.ds(r0, grp), :] = cnt.astype(BF16)
                e2_scr[pl.ds(r0, grp), :] = jnp.exp(s - brow[0]).astype(BF16)
                return carry
            lax.fori_loop(0, N_KEYS // grp, rank_body, 0)

    nsub = eb // N_KEYS

    def gate_block(sub, chunks):
        for ii in range(nsub):
            base = pl.multiple_of(((g * 2 + sub) * nsub + ii) * nh, nh)
            for c in chunks:
                ls = slice(c * LANE, (c + 1) * LANE)
                n_i = n_scr[pl.ds(base, nh), ls]
                e_i = e1_scr[pl.ds(base, nh), ls]
                acc = None
                for hh in range(nh):
                    rs = slice(hh * N_KEYS, (hh + 1) * N_KEYS)
                    term = jnp.where(rank_scr[rs, ls] < _row_bf16(n_i, hh), e2_scr[rs, ls],
                                     jnp.zeros((), BF16)) * _row_bf16(e_i, hh)
                    acc = term if acc is None else acc + term
                act = _gelu(ht_scr[sub][ii * N_KEYS:(ii + 1) * N_KEYS, ls]).astype(BF16)
                wt_scr[sub][ii * N_KEYS:(ii + 1) * N_KEYS, ls] = act * acc

    ht_scr = (ht0_scr, ht1_scr)
    wt_scr = (wt0_scr, wt1_scr)
    half = tm // 2
    cpl = half // LANE
    for sub in range(2):
        for hf in range(2):
            ln = slice(hf * half, (hf + 1) * half)
            ht_scr[1 - sub][:, ln] = jnp.dot(u_ref[sub * eb:(sub + 1) * eb, :], xt_ref[:, ln],
                                             preferred_element_type=F32)
            gate_block(sub, range(hf * cpl, (hf + 1) * cpl))
            y_scr[:, ln] += jnp.dot(vt_ref[:, sub * eb:(sub + 1) * eb], wt_scr[sub][:, ln],
                                    preferred_element_type=F32)

    @pl.when(g == ng - 1)
    def _finish():
        z = ALPHA * x1_ref[...] + y_scr[...].T
        x2 = _layer_norm(z, g_ref[...], b_ref[...])
        x2_ref[...] = x2


def _peer(x1t, wf, u, v, x1, g, b, tm=512, eb=256):
    T = x1.shape[0]
    ne = u.shape[0]
    ncand = len(_n_cand())
    rows = N_KEYS * PEER_HEADS
    const = lambda t, k: (0, 0)
    slab = lambda n: pltpu.VMEM((n, PEER_HEADS, tm), F32)
    u_roll = jnp.roll(u, -eb, axis=0)
    vt = v.reshape(ne // (2 * eb), 2 * eb, D_MODEL).transpose(0, 2, 1)
    return pl.pallas_call(
        _peer_kernel,
        grid=(T // tm, ne // (2 * eb)),
        in_specs=[pl.BlockSpec((D_MODEL, tm), lambda t, k: (0, t)),
                  pl.BlockSpec(wf.shape, const),
                  pl.BlockSpec((eb, D_MODEL), const),
                  pl.BlockSpec((2 * eb, D_MODEL), lambda t, k: (k, 0)),
                  pl.BlockSpec((None, D_MODEL, 2 * eb), lambda t, k: (k, 0, 0)),
                  pl.BlockSpec((tm, D_MODEL), lambda t, k: (t, 0)),
                  pl.BlockSpec((1, D_MODEL), const),
                  pl.BlockSpec((1, D_MODEL), const)],
        out_specs=pl.BlockSpec((tm, D_MODEL), lambda t, k: (t, 0)),
        out_shape=jax.ShapeDtypeStruct((T, D_MODEL), F32),
        scratch_shapes=[pltpu.VMEM((rows, tm), jnp.uint32), pltpu.VMEM((rows, tm), jnp.uint32),
                        pltpu.VMEM((rows, tm), BF16), pltpu.VMEM((rows, tm), BF16),
                        pltpu.VMEM((rows, tm), F32),
                        pltpu.VMEM((D_MODEL, tm), F32),
                        pltpu.VMEM((eb, tm), BF16), pltpu.VMEM((eb, tm), BF16),
                        pltpu.VMEM((eb, tm), F32), pltpu.VMEM((eb, tm), F32),
                        slab(N_KEYS), pltpu.VMEM((2, N_RANK, PEER_HEADS, tm), F32),
                        slab(ncand), slab(N_RANK), slab(N_RANK)],
        compiler_params=_cparams(("parallel", "arbitrary")),
        name="peer_ln2",
    )(x1t, wf, u, u_roll, vt, x1, g, b)


def _prep_w_in(w_in):
    sizes = (512, 512, 512, 8, 512, 128, 128, 512, 1024, 4096)
    offs = np.cumsum((0,) + sizes)
    q_f, k_f, v_f, f_l, q_s, k_s, v_s, x_pool, x_conv, gl = (w_in[:, offs[i]:offs[i + 1]] for i in range(10))
    scale = HEAD_DIM ** -0.5
    q_s = q_s.reshape(D_MODEL, 2, 4, HEAD_DIM).transpose(0, 2, 1, 3).reshape(D_MODEL, 512)
    zeros = lambda n: jnp.zeros((D_MODEL, n), w_in.dtype)
    w_h = jnp.concatenate([q_f * scale, k_f, v_f, q_s * scale, x_pool, k_s, v_s, f_l, zeros(120), zeros(128), x_conv],
                          axis=1)
    return w_h.astype(BF16), gl.astype(BF16)


def _prep_peer(wq, k1, k2):
    wq_t = wq.T.astype(BF16)
    keys = jnp.stack([k1, k2]).astype(BF16)
    wf = _peer_score_weights(wq_t, keys)
    wf = wf.reshape(PEER_HEADS, 2, N_KEYS, D_MODEL)
    kh = lambda half: wf[:, half].transpose(1, 0, 2).reshape(N_KEYS * PEER_HEADS, D_MODEL)
    hk = wf[:, 1].reshape(PEER_HEADS * N_KEYS, D_MODEL)
    return jnp.concatenate([kh(0), kh(1), hk], axis=0)


def _pad_lanes(v, n):
    return jnp.zeros((1, n), F32).at[0, :v.shape[0]].set(v.astype(F32))


def kernel(x, w_in, b_f, swa_sinks, pool_w, pool_scale, dw_w, dw_b, conv_ln_g, conv_ln_b, p_fox, p_swa, p_pool,
           p_conv, w_out, ln1_g, ln1_b, peer_wq, peer_k1, peer_k2, peer_u, peer_v, ln2_g, ln2_b):
    B, S, D = x.shape
    T = B * S
    xf = x.reshape(T, D)
    row = lambda v: v.reshape(1, -1).astype(F32)
    for l in range(DEPTH):
        w_h, w_gate = _prep_w_in(w_in[l])
        h = _inproj(xf, w_h)
        aq, ak = _decay(h, _pad_lanes(b_f[l], LANE), B, S)
        y_fox = _fox(h, aq, ak, B, S)
        y_swa = _swa(h, swa_sinks[l].astype(F32), B, S)
        y_pool = _pool(h, pool_w[l].astype(BF16), row(pool_scale[l]), B, S)
        dw = jnp.zeros((32, CONV_W), F32).at[:CONV_K].set(dw_w[l])
        y_conv = _conv(h, dw, row(dw_b[l]), row(conv_ln_g[l]), row(conv_ln_b[l]), B, S)
        ps = p_swa[l].reshape(2, 4, HEAD_DIM, D).transpose(1, 0, 2, 3).reshape(512, D)
        x1, x1t = _merge(xf, y_fox, y_pool, y_conv, y_swa, w_gate, p_fox[l].astype(BF16),
                         p_pool[l].astype(BF16), p_conv[l].astype(BF16), ps.astype(BF16),
                         w_out[l].astype(BF16), row(ln1_g[l]), row(ln1_b[l]))
        wf = _prep_peer(peer_wq[l], peer_k1[l], peer_k2[l])
        xf = _peer(x1t, wf, peer_u[l].astype(BF16), peer_v[l].astype(BF16), x1,
                       row(ln2_g[l]), row(ln2_b[l]))
    return xf.reshape(B, S, D)
```

```python
import functools
import math

import numpy as np
import jax
import jax.numpy as jnp
from jax import lax
from jax.experimental import pallas as pl
from jax.experimental.pallas import tpu as pltpu

F32 = jnp.float32
BF16 = jnp.bfloat16

D_MODEL = 1024
DEPTH = 2
HEAD_DIM = 64
SWA_HEADS = 8
WINDOW = 128
POOL_WINDOWS = (2, 4, 8, 16)
POOL_GW = 128
POOL_W = 512
CONV_W = 512
CONV_K = 31
PEER_HEADS = 8
N_KEYS = 128
PEER_TOPK = 16
LN_EPS = 1e-5
ALPHA = (2 * DEPTH) ** 0.25
NEG_BIG = -1e30

COL_QF, COL_KF, COL_VF, COL_QS, COL_POOL, COL_KS, COL_VS, COL_FL, COL_CONV = (
    0, 512, 1024, 1536, 2048, 2560, 2688, 2816, 3072)
LANE = 128
AUG_A = 6
AUG_B = 12

VMEM_LIMIT = 56 * 1024 * 1024

TOKEN_TILE = 512
MERGE_TILE = 256
SWA_BLOCK = WINDOW
HALO = 128
PEER_EXPERT_BLOCK = 256


def _cparams(sem):
    return pltpu.CompilerParams(dimension_semantics=sem, vmem_limit_bytes=VMEM_LIMIT)


def _layer_norm(z, g, b):
    mu = jnp.mean(z, axis=-1, keepdims=True)
    zc = z - mu
    var = jnp.mean(zc * zc, axis=-1, keepdims=True)
    return zc * lax.rsqrt(var + LN_EPS) * g + b


def _sigmoid(z):
    return 1.0 / (1.0 + jnp.exp(-z))


def _inproj_kernel(x_ref, w_ref, o_ref, *, n_chunk):
    x = x_ref[...].astype(BF16)
    for c in range(0, o_ref.shape[1], n_chunk):
        o_ref[:, c:c + n_chunk] = jnp.dot(
            x, w_ref[:, c:c + n_chunk], preferred_element_type=F32).astype(o_ref.dtype)


def _inproj(x, w, tm=TOKEN_TILE):
    T, K = x.shape
    N = w.shape[1]
    return pl.pallas_call(
        functools.partial(_inproj_kernel, n_chunk=1024),
        grid=(T // tm,),
        in_specs=[pl.BlockSpec((tm, K), lambda i: (i, 0)),
                  pl.BlockSpec((K, N), lambda i: (0, 0))],
        out_specs=pl.BlockSpec((tm, N), lambda i: (i, 0)),
        out_shape=jax.ShapeDtypeStruct((T, N), BF16),
        compiler_params=_cparams(("parallel",)),
        name="inproj",
    )(x, w)


def _split3(v):
    hi = v.astype(BF16)
    r1 = v - hi.astype(F32)
    mid = r1.astype(BF16)
    r2 = r1 - mid.astype(F32)
    return hi, mid, r2.astype(BF16)


def _decay_kernel(fl_ref, bf_ref, selq_ref, selk_ref, cq_ref, ck_ref, aq_ref, ak_ref, carry_ref):
    ts = fl_ref.shape[0]

    @pl.when(pl.program_id(1) == 0)
    def _():
        carry_ref[...] = jnp.zeros_like(carry_ref)

    z = fl_ref[...].astype(F32) + bf_ref[...]
    ls = jnp.minimum(z, 0.0) - jnp.log1p(jnp.exp(-jnp.abs(z)))
    row = lax.broadcasted_iota(jnp.int32, (ts, ts), 0)
    col = lax.broadcasted_iota(jnp.int32, (ts, ts), 1)
    tri = jnp.where(col <= row, 1.0, 0.0).astype(BF16)
    parts = jnp.concatenate(_split3(ls), axis=1)
    cs = jnp.dot(tri, parts, preferred_element_type=F32)
    c = cs[:, :LANE] + cs[:, LANE:2 * LANE] + cs[:, 2 * LANE:] + carry_ref[0:1, :]
    carry_ref[...] = jnp.broadcast_to(c[ts - 1:ts, :], carry_ref.shape)
    cparts = jnp.concatenate(_split3(c), axis=1)
    aq_ref[...] = (jnp.dot(cparts, selq_ref[...], preferred_element_type=F32) + cq_ref[...]).astype(BF16)
    ak_ref[...] = (jnp.dot(cparts, selk_ref[...], preferred_element_type=F32) + ck_ref[...]).astype(BF16)


def _decay_consts():
    selq = np.zeros((3 * LANE, 4 * LANE), np.float32)
    selk = np.zeros((3 * LANE, 4 * LANE), np.float32)
    cq = np.zeros((1, 4 * LANE), np.float32)
    ck = np.zeros((1, 4 * LANE), np.float32)
    for p in range(4):
        for part in range(3):
            for hh in range(2):
                base = p * LANE + hh * AUG_A
                selq[part * LANE + 2 * p + hh, base + part] = 1.0
                cq[0, base + 3 + part] = 1.0
                ck[0, base + part] = 1.0
                selk[part * LANE + 2 * p + hh, base + 3 + part] = -1.0
    return (jnp.asarray(selq, BF16), jnp.asarray(selk, BF16), jnp.asarray(cq), jnp.asarray(ck))


def _decay(h, bf_pad, B, S, ts=TOKEN_TILE):
    T = B * S
    nt = S // ts
    selq, selk, cq, ck = _decay_consts()
    const = lambda b, j: (0, 0)
    return pl.pallas_call(
        _decay_kernel,
        grid=(B, nt),
        in_specs=[pl.BlockSpec((ts, LANE), lambda b, j: (b * nt + j, COL_FL // LANE)),
                  pl.BlockSpec((1, LANE), const),
                  pl.BlockSpec((3 * LANE, 4 * LANE), const),
                  pl.BlockSpec((3 * LANE, 4 * LANE), const),
                  pl.BlockSpec((1, 4 * LANE), const),
                  pl.BlockSpec((1, 4 * LANE), const)],
        out_specs=[pl.BlockSpec((ts, 4 * LANE), lambda b, j: (b * nt + j, 0)),
                   pl.BlockSpec((ts, 4 * LANE), lambda b, j: (b * nt + j, 0))],
        out_shape=[jax.ShapeDtypeStruct((T, 4 * LANE), BF16)] * 2,
        scratch_shapes=[pltpu.VMEM((8, LANE), F32)],
        compiler_params=_cparams(("parallel", "arbitrary")),
        name="fox_decay",
    )(h, bf_pad, selq, selk, cq, ck)


def _fox_kernel(q_ref, aq_ref, k_ref, ak_ref, v_ref, o_ref, m_scr, acc_scr, s0_scr, s1_scr):
    tq = q_ref.shape[0]
    qi = pl.program_id(2)
    lane2 = lax.broadcasted_iota(jnp.int32, (1, 2 * LANE), 1)
    head_mask = (
        (lane2 < HEAD_DIM) | ((lane2 >= LANE) & (lane2 < LANE + AUG_A)),
        ((lane2 >= HEAD_DIM) & (lane2 < LANE)) | ((lane2 >= LANE + AUG_A) & (lane2 < LANE + AUG_B)),
    )
    qf = jnp.concatenate([q_ref[...], aq_ref[...]], axis=1)
    qs = [jnp.where(mk, qf, jnp.zeros_like(qf)) for mk in head_mask]
    ones_col = jnp.where(lax.broadcasted_iota(jnp.int32, (tq, LANE), 1) == 0, 1.0, 0.0).astype(BF16)
    row = lax.broadcasted_iota(jnp.int32, (tq, tq), 0)
    col = lax.broadcasted_iota(jnp.int32, (tq, tq), 1)

    m_scr[...] = jnp.full(m_scr.shape, NEG_BIG, F32)
    acc_scr[...] = jnp.zeros(acc_scr.shape, F32)

    def logits(j, dst):
        off = pl.multiple_of(j * tq, tq)
        kf = jnp.concatenate([k_ref[pl.ds(off, tq), :], ak_ref[pl.ds(off, tq), :]], axis=1)
        for x in range(2):
            dst[x] = lax.dot_general(qs[x], kf, (((1,), (1,)), ((), ())), preferred_element_type=F32)

    def accumulate(j, src, masked):
        off = pl.multiple_of(j * tq, tq)
        vf = jnp.concatenate([v_ref[pl.ds(off, tq), :], ones_col], axis=1)
        for x in range(2):
            s = src[x]
            if masked:
                s = jnp.where(col <= row, s, NEG_BIG)
            m_prev = m_scr[x]
            m_new = jnp.maximum(m_prev, jnp.max(s, axis=1, keepdims=True))
            alpha = jnp.exp(m_prev - m_new)
            p = jnp.exp(s - jnp.concatenate([m_new] * (tq // LANE), axis=1))
            acc_scr[x] = (acc_scr[x] * jnp.concatenate([alpha, alpha], axis=1)
                          + jnp.dot(p.astype(BF16), vf, preferred_element_type=F32))
            m_scr[x] = m_new

    logits(0, s0_scr)

    def body(jj, carry):
        j = 2 * jj
        logits(j + 1, s1_scr)
        accumulate(j, s0_scr, False)
        logits(j + 2, s0_scr)
        accumulate(j + 1, s1_scr, False)
        return carry

    lax.fori_loop(0, qi // 2, body, 0)

    @pl.when(qi % 2 == 0)
    def _():
        accumulate(qi, s0_scr, True)

    @pl.when(qi % 2 == 1)
    def _():
        logits(qi, s1_scr)
        accumulate(qi - 1, s0_scr, False)
        accumulate(qi, s1_scr, True)

    outs = []
    for x in range(2):
        acc = acc_scr[x]
        outs.append(acc[:, :LANE] / acc[:, LANE:LANE + 1])
    lane = lax.broadcasted_iota(jnp.int32, (tq, LANE), 1)
    o_ref[...] = jnp.where(lane < HEAD_DIM, outs[0], outs[1]).astype(o_ref.dtype)


def _fox(h, aq, ak, B, S, tq=TOKEN_TILE):
    T = B * S
    nq = S // tq
    return pl.pallas_call(
        _fox_kernel,
        grid=(B, 4, nq),
        in_specs=[pl.BlockSpec((tq, LANE), lambda b, p, i: (b * nq + i, COL_QF // LANE + p)),
                  pl.BlockSpec((tq, LANE), lambda b, p, i: (b * nq + i, p)),
                  pl.BlockSpec((S, LANE), lambda b, p, i: (b, COL_KF // LANE + p)),
                  pl.BlockSpec((S, LANE), lambda b, p, i: (b, p)),
                  pl.BlockSpec((S, LANE), lambda b, p, i: (b, COL_VF // LANE + p))],
        out_specs=pl.BlockSpec((tq, LANE), lambda b, p, i: (b * nq + i, p)),
        out_shape=jax.ShapeDtypeStruct((T, 4 * LANE), BF16),
        scratch_shapes=[pltpu.VMEM((2, tq, LANE), F32), pltpu.VMEM((2, tq, 2 * LANE), F32),
                        pltpu.VMEM((2, tq, tq), F32), pltpu.VMEM((2, tq, tq), F32)],
        compiler_params=_cparams(("parallel", "parallel", "arbitrary")),
        name="fox_attn",
    )(h, aq, h, ak, h)


def _swa_kernel(sink_ref, q_ref, kc_ref, kp_ref, vc_ref, vp_ref, o_ref):
    n = pl.program_id(1)
    blk = q_ref.shape[0]
    kb = jnp.concatenate([kp_ref[...], kc_ref[...]], axis=0)
    vb = jnp.concatenate([vp_ref[...], vc_ref[...]], axis=0)
    qi = lax.broadcasted_iota(jnp.int32, (blk, 2 * blk), 0)
    kj = lax.broadcasted_iota(jnp.int32, (blk, 2 * blk), 1)
    dist = qi + blk - kj
    valid = (dist >= 0) & (dist < WINDOW) & ((kj >= blk) | (n > 0))
    distf = dist.astype(F32)
    lane = lax.broadcasted_iota(jnp.int32, (blk, LANE), 1)
    lo = lane < HEAD_DIM
    heads = [(m, half) for m in range(4) for half in range(2)]
    logits = []
    for m, half in heads:
        hd = m + 4 * half
        slope = 2.0 ** (-8.0 * (hd + 1) / SWA_HEADS)
        qm = q_ref[:, m * LANE:(m + 1) * LANE]
        qh = jnp.where(lo if half == 0 else jnp.logical_not(lo), qm, jnp.zeros_like(qm))
        s = lax.dot_general(qh, kb, (((1,), (1,)), ((), ())), preferred_element_type=F32)
        logits.append(jnp.where(valid, s - slope * distf, NEG_BIG))
    probs = []
    for (m, half), s in zip(heads, logits):
        sink = sink_ref[m + 4 * half]
        mx = jnp.maximum(jnp.max(s, axis=1, keepdims=True), sink)
        e = jnp.exp(s - mx)
        den = jnp.sum(e, axis=1, keepdims=True) + jnp.exp(sink - mx)
        probs.append((e / den).astype(BF16))
    outs = [jnp.dot(p, vb, preferred_element_type=F32) for p in probs]
    for m in range(4):
        o_ref[:, m * LANE:(m + 1) * LANE] = jnp.where(lo, outs[2 * m], outs[2 * m + 1]).astype(o_ref.dtype)


def _swa(h, sinks, B, S, blk=SWA_BLOCK):
    T = B * S
    nb = S // blk
    cur = lambda c: (lambda b, n: (b * nb + n, c))
    prev = lambda c: (lambda b, n: (b * nb + jnp.maximum(n - 1, 0), c))
    return pl.pallas_call(
        _swa_kernel,
        grid=(B, nb),
        in_specs=[pl.BlockSpec(memory_space=pltpu.SMEM),
                  pl.BlockSpec((blk, 4 * LANE), cur(COL_QS // (4 * LANE))),
                  pl.BlockSpec((blk, LANE), cur(COL_KS // LANE)),
                  pl.BlockSpec((blk, LANE), prev(COL_KS // LANE)),
                  pl.BlockSpec((blk, LANE), cur(COL_VS // LANE)),
                  pl.BlockSpec((blk, LANE), prev(COL_VS // LANE))],
        out_specs=pl.BlockSpec((blk, 4 * LANE), lambda b, n: (b * nb + n, 0)),
        out_shape=jax.ShapeDtypeStruct((T, 4 * LANE), BF16),
        compiler_params=_cparams(("parallel", "arbitrary")),
        name="swa_attn",
    )(sinks, h, h, h, h, h)


def _pool_kernel(xc_ref, xp_ref, w_ref, sc_ref, o_ref):
    j = pl.program_id(1)
    ts = xc_ref.shape[0]
    hal = xp_ref.shape[0]
    r = lax.broadcasted_iota(jnp.int32, (ts, ts + hal), 0)
    c = lax.broadcasted_iota(jnp.int32, (ts, ts + hal), 1) - hal
    t_glob = (lax.broadcasted_iota(jnp.int32, (ts, LANE), 0) + j * ts + 1).astype(F32)
    has_prev = j > 0
    for g, w in enumerate(POOL_WINDOWS):
        xg = xc_ref[:, g * LANE:(g + 1) * LANE]
        xp = xp_ref[:, g * LANE:(g + 1) * LANE]
        xp = jnp.where(has_prev, xp, jnp.zeros_like(xp))
        ext = jnp.concatenate([xp, xg], axis=0)
        band = jnp.where((c <= r) & (c > r - w), 1.0, 0.0).astype(BF16)
        win = jnp.dot(band, ext, preferred_element_type=F32)
        cnt = jnp.minimum(t_glob, float(w))
        pooled = win / cnt - xg.astype(F32)
        y = jnp.dot(pooled.astype(BF16), w_ref[g], preferred_element_type=F32)
        o_ref[:, g * LANE:(g + 1) * LANE] = (y * sc_ref[:, g * LANE:(g + 1) * LANE]).astype(o_ref.dtype)


def _pool(h, pool_w, pool_scale, B, S, ts=TOKEN_TILE, hal=HALO):
    T = B * S
    nt = S // ts
    r = ts // hal
    return pl.pallas_call(
        _pool_kernel,
        grid=(B, nt),
        in_specs=[pl.BlockSpec((ts, POOL_W), lambda b, j: (b * nt + j, COL_POOL // POOL_W)),
                  pl.BlockSpec((hal, POOL_W),
                               lambda b, j: (jnp.maximum((b * nt + j) * r - 1, 0), COL_POOL // POOL_W)),
                  pl.BlockSpec((4, POOL_GW, POOL_GW), lambda b, j: (0, 0, 0)),
                  pl.BlockSpec((1, POOL_W), lambda b, j: (0, 0))],
        out_specs=pl.BlockSpec((ts, POOL_W), lambda b, j: (b * nt + j, 0)),
        out_shape=jax.ShapeDtypeStruct((T, POOL_W), BF16),
        compiler_params=_cparams(("parallel", "arbitrary")),
        name="ms_pool",
    )(h, h, pool_w, pool_scale)


CONV_PAD = 32
SUBLANES = 8


def _conv_kernel(uc_ref, up_ref, w_ref, b_ref, g_ref, bb_ref, o_ref, ext_ref, sh_ref):
    j = pl.program_id(1)
    ts = uc_ref.shape[0]
    pad = CONV_PAD

    def glu(u):
        u = u.astype(F32)
        return u[:, :CONV_W] * _sigmoid(u[:, CONV_W:])

    hp = glu(up_ref[up_ref.shape[0] - pad:, :])
    ext_ref[0:pad, :] = jnp.where(j > 0, hp, jnp.zeros_like(hp))
    ext_ref[pad:pad + ts, :] = glu(uc_ref[...])
    ext_ref[pad + ts:, :] = jnp.zeros((SUBLANES, CONV_W), F32)
    for r in range(SUBLANES):
        sh_ref[r] = ext_ref[r:r + ts + pad, :]
    acc = jnp.zeros((ts, CONV_W), F32)
    for k in range(CONV_K):
        off = pad - (CONV_K - 1) + k
        r = off % SUBLANES
        acc = acc + sh_ref[r, off - r:off - r + ts, :] * w_ref[k:k + 1, :]
    y = _layer_norm(acc + b_ref[...], g_ref[...], bb_ref[...])
    o_ref[...] = (y * _sigmoid(y)).astype(o_ref.dtype)


def _conv(h, dw_w, dw_b, ln_g, ln_b, B, S, ts=TOKEN_TILE, hal=HALO):
    T = B * S
    nt = S // ts
    r = ts // hal
    vec = pl.BlockSpec((1, CONV_W), lambda b, j: (0, 0))
    return pl.pallas_call(
        _conv_kernel,
        grid=(B, nt),
        in_specs=[pl.BlockSpec((ts, 2 * CONV_W), lambda b, j: (b * nt + j, COL_CONV // (2 * CONV_W))),
                  pl.BlockSpec((hal, 2 * CONV_W),
                               lambda b, j: (jnp.maximum((b * nt + j) * r - 1, 0), COL_CONV // (2 * CONV_W))),
                  pl.BlockSpec((32, CONV_W), lambda b, j: (0, 0)),
                  vec, vec, vec],
        out_specs=pl.BlockSpec((ts, CONV_W), lambda b, j: (b * nt + j, 0)),
        out_shape=jax.ShapeDtypeStruct((T, CONV_W), BF16),
        scratch_shapes=[pltpu.VMEM((ts + CONV_PAD + SUBLANES, CONV_W), F32),
                        pltpu.VMEM((SUBLANES, ts + CONV_PAD, CONV_W), F32)],
        compiler_params=_cparams(("parallel", "arbitrary")),
        name="conf_conv",
    )(h, h, dw_w, dw_b, ln_g, ln_b)


def _merge_kernel(x_ref, yf_ref, yp_ref, yc_ref, ys_ref, wg_ref, pf_ref, pp_ref, pc_ref, ps_ref,
                  wo_ref, g_ref, b_ref, x1_ref, x1t_ref):
    xb = x_ref[...].astype(BF16)
    merged = None
    for br, (y_ref, p_ref) in enumerate(((yf_ref, pf_ref), (yp_ref, pp_ref), (yc_ref, pc_ref), (ys_ref, ps_ref))):
        gate = _sigmoid(jnp.dot(xb, wg_ref[:, br * D_MODEL:(br + 1) * D_MODEL], preferred_element_type=F32))
        term = gate * jnp.dot(y_ref[...], p_ref[...], preferred_element_type=F32)
        merged = term if merged is None else merged + term
    mix = jnp.dot(merged.astype(BF16), wo_ref[...], preferred_element_type=F32)
    x1 = _layer_norm(ALPHA * x_ref[...] + mix, g_ref[...], b_ref[...])
    x1_ref[...] = x1
    x1t_ref[...] = x1.T.astype(BF16)


def _merge(x, yf, yp, yc, ys, wg, pf, pp, pc, ps, wo, g, b, tm=MERGE_TILE):
    T = x.shape[0]
    const = lambda i: (0, 0)
    tok = lambda w: pl.BlockSpec((tm, w), lambda i: (i, 0))
    wspec = lambda a: pl.BlockSpec(a.shape, const)
    return pl.pallas_call(
        _merge_kernel,
        grid=(T // tm,),
        in_specs=[tok(D_MODEL), tok(512), tok(512), tok(512), tok(512),
                  wspec(wg), wspec(pf), wspec(pp), wspec(pc), wspec(ps), wspec(wo), wspec(g), wspec(b)],
        out_specs=[tok(D_MODEL), pl.BlockSpec((D_MODEL, tm), lambda i: (0, i))],
        out_shape=[jax.ShapeDtypeStruct((T, D_MODEL), F32), jax.ShapeDtypeStruct((D_MODEL, T), BF16)],
        compiler_params=_cparams(("parallel",)),
        name="merge_ln1",
    )(x, yf, yp, yc, ys, wg, pf, pp, pc, ps, wo, g, b)


def _wf_kernel(k_ref, wq_ref, o_ref):
    o_ref[...] = jnp.dot(k_ref[0], wq_ref[...], preferred_element_type=F32).astype(o_ref.dtype)


def _peer_score_weights(wq_t, keys):
    nblk = wq_t.shape[0] // N_KEYS
    return pl.pallas_call(
        _wf_kernel,
        grid=(nblk,),
        in_specs=[pl.BlockSpec((1, N_KEYS, N_KEYS), lambda j: (j % 2, 0, 0)),
                  pl.BlockSpec((N_KEYS, D_MODEL), lambda j: (j, 0))],
        out_specs=pl.BlockSpec((N_KEYS, D_MODEL), lambda j: (j, 0)),
        out_shape=jax.ShapeDtypeStruct(wq_t.shape, BF16),
        compiler_params=_cparams(("parallel",)),
        name="peer_wf",
    )(keys, wq_t)


N_RANK = PEER_TOPK + 1


def _n_cand():
    return [(r, c) for r in range(N_RANK) for c in range(N_RANK) if (r + 1) * (c + 1) <= N_RANK]


def _gelu(z):
    return 0.5 * z * (1.0 + lax.erf(z * (1.0 / math.sqrt(2.0))))


def _extract_top(work_ref, out_ref, n_slab, n_out):
    nh, tm = work_ref.shape[1:]
    unroll = 8 if n_slab % 8 == 0 else 4
    assert n_slab % unroll == 0
    for c in range(tm // LANE):
        ls = slice(c * LANE, (c + 1) * LANE)

        def insert(it, best):
            best = list(best)
            for k in range(unroll):
                w = work_ref[it * unroll + k, :, ls]
                for r in range(n_out):
                    best[r], w = jnp.maximum(best[r], w), jnp.minimum(best[r], w)
            return tuple(best)

        best = lax.fori_loop(0, n_slab // unroll, insert,
                             tuple(jnp.full((nh, LANE), NEG_BIG, F32) for _ in range(n_out)))
        for r in range(n_out):
            out_ref[r, :, ls] = best[r]


def _count_below(s, thr):
    c8 = s > thr[7]
    c4 = s > jnp.where(c8, thr[11], thr[3])
    c2 = s > jnp.where(c8, jnp.where(c4, thr[13], thr[9]), jnp.where(c4, thr[5], thr[1]))
    lo = jnp.where(c4, jnp.where(c2, thr[6], thr[4]), jnp.where(c2, thr[2], thr[0]))
    hi = jnp.where(c4, jnp.where(c2, thr[14], thr[12]), jnp.where(c2, thr[10], thr[8]))
    c1 = s > jnp.where(c8, hi, lo)
    one = lambda cond, w: jnp.where(cond, w, 0.0)
    return one(c8, 8.0) + one(c4, 4.0) + one(c2, 2.0) + one(c1, 1.0) + one(s > thr[15], 1.0)


def _dup_bf16(v):
    u = pltpu.bitcast(v.astype(BF16).astype(F32), jnp.uint32)
    return u | (u >> 16)


def _row_bf16(slab, hh):
    return pltpu.bitcast(jnp.broadcast_to(slab[hh:hh + 1, :], (N_KEYS // 2, LANE)), BF16)


def _peer_kernel(xt_ref, wf_ref, u0_ref, u_ref, vt_ref, x1_ref, g_ref, b_ref, x2_ref,
                 n_scr, e1_scr, rank_scr, e2_scr, s2_scr, y_scr, wt0_scr, wt1_scr, ht0_scr, ht1_scr,
                 work_scr, top_scr, cand_scr, csel_scr, thr_scr):
    g = pl.program_id(1)
    ng = pl.num_programs(1)
    tm = xt_ref.shape[1]
    eb = u_ref.shape[0] // 2
    nh = PEER_HEADS
    rows = N_KEYS * nh

    @pl.when(g == 0)
    def _select():
        y_scr[...] = jnp.zeros_like(y_scr)
        ht0_scr[...] = jnp.dot(u0_ref[...], xt_ref[...], preferred_element_type=F32)
        st = jnp.dot(wf_ref[...], xt_ref[...], preferred_element_type=F32)
        for half in range(2):
            work_scr[...] = st[half * rows:(half + 1) * rows].reshape(N_KEYS, nh, tm)
            _extract_top(work_scr, top_scr.at[half], N_KEYS, N_RANK)
        cands = _n_cand()
        for ci, (r, c) in enumerate(cands):
            cand_scr[ci] = top_scr[0, r] + top_scr[1, c]
        _extract_top(cand_scr, csel_scr, len(cands), N_RANK)
        m0 = csel_scr[0]
        zsum = jnp.zeros_like(m0)
        for r in range(PEER_TOPK):
            zsum = zsum + jnp.exp(csel_scr[r] - m0)
        tau = 0.5 * (csel_scr[PEER_TOPK - 1] + csel_scr[PEER_TOPK])
        a0 = top_scr[0, 0]
        for c in range(PEER_TOPK):
            thr_scr[c] = tau - top_scr[1, c]
        work_scr[...] = st[0:rows].reshape(N_KEYS, nh, tm)

        def key_body(i, carry):
            s = work_scr[i]
            cnt = _count_below(s, [thr_scr[c] for c in range(PEER_TOPK)])
            r0 = pl.multiple_of(i * nh, nh)
            n_scr[pl.ds(r0, nh), :] = _dup_bf16(cnt)
            e1_scr[pl.ds(r0, nh), :] = _dup_bf16(jnp.exp(s - a0) / zsum)
            return carry
        lax.fori_loop(0, N_KEYS, key_body, 0)

        s2_scr[...] = st[2 * rows:3 * rows]
        grp = 16
        for hh in range(nh):
            brow = [top_scr[1, c][hh:hh + 1, :] for c in range(N_RANK)]
            mid = [0.5 * (brow[c] + brow[c + 1]) for c in range(PEER_TOPK)]
            mid_up = mid[::-1]

            def rank_body(jg, carry):
                r0 = pl.multiple_of(hh * N_KEYS + jg * grp, grp)
                s = s2_scr[pl.ds(r0, grp), :]
                cnt = float(PEER_TOPK) - _count_below(s, mid_up)
                rank_scr[pl.ds(r0, grp), :] = cnt.astype(BF16)
                e2_scr[pl.ds(r0, grp), :] = jnp.exp(s - brow[0]).astype(BF16)
                return carry
            lax.fori_loop(0, N_KEYS // grp, rank_body, 0)

    nsub = eb // N_KEYS

    def gate_block(sub, chunks):
        for ii in range(nsub):
            base = pl.multiple_of(((g * 2 + sub) * nsub + ii) * nh, nh)
            for c in chunks:
                ls = slice(c * LANE, (c + 1) * LANE)
                n_i = n_scr[pl.ds(base, nh), ls]
                e_i = e1_scr[pl.ds(base, nh), ls]
                acc = None
                for hh in range(nh):
                    rs = slice(hh * N_KEYS, (hh + 1) * N_KEYS)
                    term = jnp.where(rank_scr[rs, ls] < _row_bf16(n_i, hh), e2_scr[rs, ls],
                                     jnp.zeros((), BF16)) * _row_bf16(e_i, hh)
                    acc = term if acc is None else acc + term
                act = _gelu(ht_scr[sub][ii * N_KEYS:(ii + 1) * N_KEYS, ls]).astype(BF16)
                wt_scr[sub][ii * N_KEYS:(ii + 1) * N_KEYS, ls] = act * acc

    ht_scr = (ht0_scr, ht1_scr)
    wt_scr = (wt0_scr, wt1_scr)
    half = tm // 2
    cpl = half // LANE
    for sub in range(2):
        for hf in range(2):
            ln = slice(hf * half, (hf + 1) * half)
            ht_scr[1 - sub][:, ln] = jnp.dot(u_ref[sub * eb:(sub + 1) * eb, :], xt_ref[:, ln],
                                             preferred_element_type=F32)
            gate_block(sub, range(hf * cpl, (hf + 1) * cpl))
            y_scr[:, ln] += jnp.dot(vt_ref[:, sub * eb:(sub + 1) * eb], wt_scr[sub][:, ln],
                                    preferred_element_type=F32)

    @pl.when(g == ng - 1)
    def _finish():
        z = ALPHA * x1_ref[...] + y_scr[...].T
        x2 = _layer_norm(z, g_ref[...], b_ref[...])
        x2_ref[...] = x2


def _peer(x1t, wf, u, v, x1, g, b, tm=TOKEN_TILE, eb=PEER_EXPERT_BLOCK):
    T = x1.shape[0]
    ne = u.shape[0]
    ncand = len(_n_cand())
    rows = N_KEYS * PEER_HEADS
    const = lambda t, k: (0, 0)
    slab = lambda n: pltpu.VMEM((n, PEER_HEADS, tm), F32)
    u_roll = jnp.roll(u, -eb, axis=0)
    vt = v.reshape(ne // (2 * eb), 2 * eb, D_MODEL).transpose(0, 2, 1)
    return pl.pallas_call(
        _peer_kernel,
        grid=(T // tm, ne // (2 * eb)),
        in_specs=[pl.BlockSpec((D_MODEL, tm), lambda t, k: (0, t)),
                  pl.BlockSpec(wf.shape, const),
                  pl.BlockSpec((eb, D_MODEL), const),
                  pl.BlockSpec((2 * eb, D_MODEL), lambda t, k: (k, 0)),
                  pl.BlockSpec((None, D_MODEL, 2 * eb), lambda t, k: (k, 0, 0)),
                  pl.BlockSpec((tm, D_MODEL), lambda t, k: (t, 0)),
                  pl.BlockSpec((1, D_MODEL), const),
                  pl.BlockSpec((1, D_MODEL), const)],
        out_specs=pl.BlockSpec((tm, D_MODEL), lambda t, k: (t, 0)),
        out_shape=jax.ShapeDtypeStruct((T, D_MODEL), F32),
        scratch_shapes=[pltpu.VMEM((rows, tm), jnp.uint32), pltpu.VMEM((rows, tm), jnp.uint32),
                        pltpu.VMEM((rows, tm), BF16), pltpu.VMEM((rows, tm), BF16),
                        pltpu.VMEM((rows, tm), F32),
                        pltpu.VMEM((D_MODEL, tm), F32),
                        pltpu.VMEM((eb, tm), BF16), pltpu.VMEM((eb, tm), BF16),
                        pltpu.VMEM((eb, tm), F32), pltpu.VMEM((eb, tm), F32),
                        slab(N_KEYS), pltpu.VMEM((2, N_RANK, PEER_HEADS, tm), F32),
                        slab(ncand), slab(N_RANK), slab(N_RANK)],
        compiler_params=_cparams(("parallel", "arbitrary")),
        name="peer_ln2",
    )(x1t, wf, u, u_roll, vt, x1, g, b)


def _prep_w_in(w_in):
    sizes = (512, 512, 512, 8, 512, 128, 128, 512, 1024, 4096)
    offs = np.cumsum((0,) + sizes)
    q_f, k_f, v_f, f_l, q_s, k_s, v_s, x_pool, x_conv, gl = (w_in[:, offs[i]:offs[i + 1]] for i in range(10))
    scale = HEAD_DIM ** -0.5
    q_s = q_s.reshape(D_MODEL, 2, 4, HEAD_DIM).transpose(0, 2, 1, 3).reshape(D_MODEL, 512)
    zeros = lambda n: jnp.zeros((D_MODEL, n), w_in.dtype)
    w_h = jnp.concatenate([q_f * scale, k_f, v_f, q_s * scale, x_pool, k_s, v_s, f_l, zeros(120), zeros(128), x_conv],
                          axis=1)
    return w_h.astype(BF16), gl.astype(BF16)


def _prep_peer(wq, k1, k2):
    wq_t = wq.T.astype(BF16)
    keys = jnp.stack([k1, k2]).astype(BF16)
    wf = _peer_score_weights(wq_t, keys)
    wf = wf.reshape(PEER_HEADS, 2, N_KEYS, D_MODEL)
    kh = lambda half: wf[:, half].transpose(1, 0, 2).reshape(N_KEYS * PEER_HEADS, D_MODEL)
    hk = wf[:, 1].reshape(PEER_HEADS * N_KEYS, D_MODEL)
    return jnp.concatenate([kh(0), kh(1), hk], axis=0)


def _pad_lanes(v, n):
    return jnp.zeros((1, n), F32).at[0, :v.shape[0]].set(v.astype(F32))


def kernel(x, w_in, b_f, swa_sinks, pool_w, pool_scale, dw_w, dw_b, conv_ln_g, conv_ln_b, p_fox, p_swa, p_pool,
           p_conv, w_out, ln1_g, ln1_b, peer_wq, peer_k1, peer_k2, peer_u, peer_v, ln2_g, ln2_b):
    B, S, D = x.shape
    T = B * S
    xf = x.reshape(T, D)
    row = lambda v: v.reshape(1, -1).astype(F32)
    for l in range(DEPTH):
        w_h, w_gate = _prep_w_in(w_in[l])
        h = _inproj(xf, w_h)
        aq, ak = _decay(h, _pad_lanes(b_f[l], LANE), B, S)
        y_fox = _fox(h, aq, ak, B, S)
        y_swa = _swa(h, swa_sinks[l].astype(F32), B, S)
        y_pool = _pool(h, pool_w[l].astype(BF16), row(pool_scale[l]), B, S)
        dw = jnp.zeros((32, CONV_W), F32).at[:CONV_K].set(dw_w[l])
        y_conv = _conv(h, dw, row(dw_b[l]), row(conv_ln_g[l]), row(conv_ln_b[l]), B, S)
        ps = p_swa[l].reshape(2, 4, HEAD_DIM, D).transpose(1, 0, 2, 3).reshape(512, D)
        x1, x1t = _merge(xf, y_fox, y_pool, y_conv, y_swa, w_gate, p_fox[l].astype(BF16),
                         p_pool[l].astype(BF16), p_conv[l].astype(BF16), ps.astype(BF16),
                         w_out[l].astype(BF16), row(ln1_g[l]), row(ln1_b[l]))
        wf = _prep_peer(peer_wq[l], peer_k1[l], peer_k2[l])
        xf = _peer(x1t, wf, peer_u[l].astype(BF16), peer_v[l].astype(BF16), x1,
                       row(ln2_g[l]), row(ln2_b[l]))
    return xf.reshape(B, S, D)
```

```python
import functools
import math

import numpy as np
import jax
import jax.numpy as jnp
from jax import lax
from jax.experimental import pallas as pl
from jax.experimental.pallas import tpu as pltpu

F32 = jnp.float32
BF16 = jnp.bfloat16
FP8 = jnp.float8_e4m3fn
FP8_TARGET = 256.0

D_MODEL = 1024
DEPTH = 2
HEAD_DIM = 64
SWA_HEADS = 8
WINDOW = 128
POOL_WINDOWS = (2, 4, 8, 16)
POOL_GW = 128
POOL_W = 512
CONV_W = 512
CONV_K = 31
PEER_HEADS = 8
N_KEYS = 128
PEER_TOPK = 16
LN_EPS = 1e-5
ALPHA = (2 * DEPTH) ** 0.25
NEG_BIG = -1e30

COL_QF, COL_KF, COL_VF, COL_QS, COL_POOL, COL_KS, COL_VS, COL_FL, COL_CONV = (
    0, 512, 1024, 1536, 2048, 2560, 2688, 2816, 3072)
LANE = 128
AUG_A = 6
AUG_B = 12

VMEM_LIMIT = 56 * 1024 * 1024

TOKEN_TILE = 512
MERGE_TILE = 256
SWA_BLOCK = WINDOW
HALO = 128
PEER_EXPERT_BLOCK = 256


def _cparams(sem):
    return pltpu.CompilerParams(dimension_semantics=sem, vmem_limit_bytes=VMEM_LIMIT)


def _layer_norm(z, g, b):
    mu = jnp.mean(z, axis=-1, keepdims=True)
    zc = z - mu
    var = jnp.mean(zc * zc, axis=-1, keepdims=True)
    return zc * lax.rsqrt(var + LN_EPS) * g + b


def _sigmoid(z):
    return 1.0 / (1.0 + jnp.exp(-z))


def _inproj_kernel(x_ref, w_ref, o_ref, *, n_chunk):
    x = x_ref[...].astype(BF16)
    for c in range(0, o_ref.shape[1], n_chunk):
        o_ref[:, c:c + n_chunk] = jnp.dot(
            x, w_ref[:, c:c + n_chunk], preferred_element_type=F32).astype(o_ref.dtype)


def _inproj(x, w, tm=TOKEN_TILE):
    T, K = x.shape
    N = w.shape[1]
    return pl.pallas_call(
        functools.partial(_inproj_kernel, n_chunk=1024),
        grid=(T // tm,),
        in_specs=[pl.BlockSpec((tm, K), lambda i: (i, 0)),
                  pl.BlockSpec((K, N), lambda i: (0, 0))],
        out_specs=pl.BlockSpec((tm, N), lambda i: (i, 0)),
        out_shape=jax.ShapeDtypeStruct((T, N), BF16),
        compiler_params=_cparams(("parallel",)),
        name="inproj",
    )(x, w)


def _split3(v):
    hi = v.astype(BF16)
    r1 = v - hi.astype(F32)
    mid = r1.astype(BF16)
    r2 = r1 - mid.astype(F32)
    return hi, mid, r2.astype(BF16)


def _decay_kernel(fl_ref, bf_ref, selq_ref, selk_ref, cq_ref, ck_ref, aq_ref, ak_ref, carry_ref):
    ts = fl_ref.shape[0]

    @pl.when(pl.program_id(1) == 0)
    def _():
        carry_ref[...] = jnp.zeros_like(carry_ref)

    z = fl_ref[...].astype(F32) + bf_ref[...]
    ls = jnp.minimum(z, 0.0) - jnp.log1p(jnp.exp(-jnp.abs(z)))
    row = lax.broadcasted_iota(jnp.int32, (ts, ts), 0)
    col = lax.broadcasted_iota(jnp.int32, (ts, ts), 1)
    tri = jnp.where(col <= row, 1.0, 0.0).astype(BF16)
    parts = jnp.concatenate(_split3(ls), axis=1)
    cs = jnp.dot(tri, parts, preferred_element_type=F32)
    c = cs[:, :LANE] + cs[:, LANE:2 * LANE] + cs[:, 2 * LANE:] + carry_ref[0:1, :]
    carry_ref[...] = jnp.broadcast_to(c[ts - 1:ts, :], carry_ref.shape)
    cparts = jnp.concatenate(_split3(c), axis=1)
    aq_ref[...] = (jnp.dot(cparts, selq_ref[...], preferred_element_type=F32) + cq_ref[...]).astype(BF16)
    ak_ref[...] = (jnp.dot(cparts, selk_ref[...], preferred_element_type=F32) + ck_ref[...]).astype(BF16)


def _decay_consts():
    selq = np.zeros((3 * LANE, 4 * LANE), np.float32)
    selk = np.zeros((3 * LANE, 4 * LANE), np.float32)
    cq = np.zeros((1, 4 * LANE), np.float32)
    ck = np.zeros((1, 4 * LANE), np.float32)
    for p in range(4):
        for part in range(3):
            for hh in range(2):
                base = p * LANE + hh * AUG_A
                selq[part * LANE + 2 * p + hh, base + part] = 1.0
                cq[0, base + 3 + part] = 1.0
                ck[0, base + part] = 1.0
                selk[part * LANE + 2 * p + hh, base + 3 + part] = -1.0
    return (jnp.asarray(selq, BF16), jnp.asarray(selk, BF16), jnp.asarray(cq), jnp.asarray(ck))


def _decay(h, bf_pad, B, S, ts=TOKEN_TILE):
    T = B * S
    nt = S // ts
    selq, selk, cq, ck = _decay_consts()
    const = lambda b, j: (0, 0)
    return pl.pallas_call(
        _decay_kernel,
        grid=(B, nt),
        in_specs=[pl.BlockSpec((ts, LANE), lambda b, j: (b * nt + j, COL_FL // LANE)),
                  pl.BlockSpec((1, LANE), const),
                  pl.BlockSpec((3 * LANE, 4 * LANE), const),
                  pl.BlockSpec((3 * LANE, 4 * LANE), const),
                  pl.BlockSpec((1, 4 * LANE), const),
                  pl.BlockSpec((1, 4 * LANE), const)],
        out_specs=[pl.BlockSpec((ts, 4 * LANE), lambda b, j: (b * nt + j, 0)),
                   pl.BlockSpec((ts, 4 * LANE), lambda b, j: (b * nt + j, 0))],
        out_shape=[jax.ShapeDtypeStruct((T, 4 * LANE), BF16)] * 2,
        scratch_shapes=[pltpu.VMEM((8, LANE), F32)],
        compiler_params=_cparams(("parallel", "arbitrary")),
        name="fox_decay",
    )(h, bf_pad, selq, selk, cq, ck)


def _fox_kernel(q_ref, aq_ref, k_ref, ak_ref, v_ref, o_ref, m_scr, acc_scr, s0_scr, s1_scr):
    tq = q_ref.shape[0]
    qi = pl.program_id(2)
    lane2 = lax.broadcasted_iota(jnp.int32, (1, 2 * LANE), 1)
    head_mask = (
        (lane2 < HEAD_DIM) | ((lane2 >= LANE) & (lane2 < LANE + AUG_A)),
        ((lane2 >= HEAD_DIM) & (lane2 < LANE)) | ((lane2 >= LANE + AUG_A) & (lane2 < LANE + AUG_B)),
    )
    qf = jnp.concatenate([q_ref[...], aq_ref[...]], axis=1)
    qs = [jnp.where(mk, qf, jnp.zeros_like(qf)) for mk in head_mask]
    ones_col = jnp.where(lax.broadcasted_iota(jnp.int32, (tq, LANE), 1) == 0, 1.0, 0.0).astype(BF16)
    row = lax.broadcasted_iota(jnp.int32, (tq, tq), 0)
    col = lax.broadcasted_iota(jnp.int32, (tq, tq), 1)

    m_scr[...] = jnp.full(m_scr.shape, NEG_BIG, F32)
    acc_scr[...] = jnp.zeros(acc_scr.shape, F32)

    def logits(j, dst):
        off = pl.multiple_of(j * tq, tq)
        kf = jnp.concatenate([k_ref[pl.ds(off, tq), :], ak_ref[pl.ds(off, tq), :]], axis=1)
        for x in range(2):
            dst[x] = lax.dot_general(qs[x], kf, (((1,), (1,)), ((), ())), preferred_element_type=F32)

    def accumulate(j, src, masked):
        off = pl.multiple_of(j * tq, tq)
        vf = jnp.concatenate([v_ref[pl.ds(off, tq), :], ones_col], axis=1)
        for x in range(2):
            s = src[x]
            if masked:
                s = jnp.where(col <= row, s, NEG_BIG)
            m_prev = m_scr[x]
            m_new = jnp.maximum(m_prev, jnp.max(s, axis=1, keepdims=True))
            alpha = jnp.exp(m_prev - m_new)
            p = jnp.exp(s - jnp.concatenate([m_new] * (tq // LANE), axis=1))
            acc_scr[x] = (acc_scr[x] * jnp.concatenate([alpha, alpha], axis=1)
                          + jnp.dot(p.astype(BF16), vf, preferred_element_type=F32))
            m_scr[x] = m_new

    logits(0, s0_scr)

    def body(jj, carry):
        j = 2 * jj
        logits(j + 1, s1_scr)
        accumulate(j, s0_scr, False)
        logits(j + 2, s0_scr)
        accumulate(j + 1, s1_scr, False)
        return carry

    lax.fori_loop(0, qi // 2, body, 0)

    @pl.when(qi % 2 == 0)
    def _():
        accumulate(qi, s0_scr, True)

    @pl.when(qi % 2 == 1)
    def _():
        logits(qi, s1_scr)
        accumulate(qi - 1, s0_scr, False)
        accumulate(qi, s1_scr, True)

    outs = []
    for x in range(2):
        acc = acc_scr[x]
        outs.append(acc[:, :LANE] / acc[:, LANE:LANE + 1])
    lane = lax.broadcasted_iota(jnp.int32, (tq, LANE), 1)
    o_ref[...] = jnp.where(lane < HEAD_DIM, outs[0], outs[1]).astype(o_ref.dtype)


def _fox(h, aq, ak, B, S, tq=TOKEN_TILE):
    T = B * S
    nq = S // tq
    return pl.pallas_call(
        _fox_kernel,
        grid=(B, 4, nq),
        in_specs=[pl.BlockSpec((tq, LANE), lambda b, p, i: (b * nq + i, COL_QF // LANE + p)),
                  pl.BlockSpec((tq, LANE), lambda b, p, i: (b * nq + i, p)),
                  pl.BlockSpec((S, LANE), lambda b, p, i: (b, COL_KF // LANE + p)),
                  pl.BlockSpec((S, LANE), lambda b, p, i: (b, p)),
                  pl.BlockSpec((S, LANE), lambda b, p, i: (b, COL_VF // LANE + p))],
        out_specs=pl.BlockSpec((tq, LANE), lambda b, p, i: (b * nq + i, p)),
        out_shape=jax.ShapeDtypeStruct((T, 4 * LANE), BF16),
        scratch_shapes=[pltpu.VMEM((2, tq, LANE), F32), pltpu.VMEM((2, tq, 2 * LANE), F32),
                        pltpu.VMEM((2, tq, tq), F32), pltpu.VMEM((2, tq, tq), F32)],
        compiler_params=_cparams(("parallel", "parallel", "arbitrary")),
        name="fox_attn",
    )(h, aq, h, ak, h)


def _swa_kernel(sink_ref, q_ref, kc_ref, kp_ref, vc_ref, vp_ref, o_ref):
    n = pl.program_id(1)
    blk = q_ref.shape[0]
    kb = jnp.concatenate([kp_ref[...], kc_ref[...]], axis=0)
    vb = jnp.concatenate([vp_ref[...], vc_ref[...]], axis=0)
    qi = lax.broadcasted_iota(jnp.int32, (blk, 2 * blk), 0)
    kj = lax.broadcasted_iota(jnp.int32, (blk, 2 * blk), 1)
    dist = qi + blk - kj
    valid = (dist >= 0) & (dist < WINDOW) & ((kj >= blk) | (n > 0))
    distf = dist.astype(F32)
    lane = lax.broadcasted_iota(jnp.int32, (blk, LANE), 1)
    lo = lane < HEAD_DIM
    heads = [(m, half) for m in range(4) for half in range(2)]
    logits = []
    for m, half in heads:
        hd = m + 4 * half
        slope = 2.0 ** (-8.0 * (hd + 1) / SWA_HEADS)
        qm = q_ref[:, m * LANE:(m + 1) * LANE]
        qh = jnp.where(lo if half == 0 else jnp.logical_not(lo), qm, jnp.zeros_like(qm))
        s = lax.dot_general(qh, kb, (((1,), (1,)), ((), ())), preferred_element_type=F32)
        logits.append(jnp.where(valid, s - slope * distf, NEG_BIG))
    probs = []
    for (m, half), s in zip(heads, logits):
        sink = sink_ref[m + 4 * half]
        mx = jnp.maximum(jnp.max(s, axis=1, keepdims=True), sink)
        e = jnp.exp(s - mx)
        den = jnp.sum(e, axis=1, keepdims=True) + jnp.exp(sink - mx)
        probs.append((e / den).astype(BF16))
    outs = [jnp.dot(p, vb, preferred_element_type=F32) for p in probs]
    for m in range(4):
        o_ref[:, m * LANE:(m + 1) * LANE] = jnp.where(lo, outs[2 * m], outs[2 * m + 1]).astype(o_ref.dtype)


def _swa(h, sinks, B, S, blk=SWA_BLOCK):
    T = B * S
    nb = S // blk
    cur = lambda c: (lambda b, n: (b * nb + n, c))
    prev = lambda c: (lambda b, n: (b * nb + jnp.maximum(n - 1, 0), c))
    return pl.pallas_call(
        _swa_kernel,
        grid=(B, nb),
        in_specs=[pl.BlockSpec(memory_space=pltpu.SMEM),
                  pl.BlockSpec((blk, 4 * LANE), cur(COL_QS // (4 * LANE))),
                  pl.BlockSpec((blk, LANE), cur(COL_KS // LANE)),
                  pl.BlockSpec((blk, LANE), prev(COL_KS // LANE)),
                  pl.BlockSpec((blk, LANE), cur(COL_VS // LANE)),
                  pl.BlockSpec((blk, LANE), prev(COL_VS // LANE))],
        out_specs=pl.BlockSpec((blk, 4 * LANE), lambda b, n: (b * nb + n, 0)),
        out_shape=jax.ShapeDtypeStruct((T, 4 * LANE), BF16),
        compiler_params=_cparams(("parallel", "arbitrary")),
        name="swa_attn",
    )(sinks, h, h, h, h, h)


def _pool_kernel(xc_ref, xp_ref, w_ref, sc_ref, o_ref):
    j = pl.program_id(1)
    ts = xc_ref.shape[0]
    hal = xp_ref.shape[0]
    r = lax.broadcasted_iota(jnp.int32, (ts, ts + hal), 0)
    c = lax.broadcasted_iota(jnp.int32, (ts, ts + hal), 1) - hal
    t_glob = (lax.broadcasted_iota(jnp.int32, (ts, LANE), 0) + j * ts + 1).astype(F32)
    has_prev = j > 0
    for g, w in enumerate(POOL_WINDOWS):
        xg = xc_ref[:, g * LANE:(g + 1) * LANE]
        xp = xp_ref[:, g * LANE:(g + 1) * LANE]
        xp = jnp.where(has_prev, xp, jnp.zeros_like(xp))
        ext = jnp.concatenate([xp, xg], axis=0)
        band = jnp.where((c <= r) & (c > r - w), 1.0, 0.0).astype(BF16)
        win = jnp.dot(band, ext, preferred_element_type=F32)
        cnt = jnp.minimum(t_glob, float(w))
        pooled = win / cnt - xg.astype(F32)
        y = jnp.dot(pooled.astype(BF16), w_ref[g], preferred_element_type=F32)
        o_ref[:, g * LANE:(g + 1) * LANE] = (y * sc_ref[:, g * LANE:(g + 1) * LANE]).astype(o_ref.dtype)


def _pool(h, pool_w, pool_scale, B, S, ts=TOKEN_TILE, hal=HALO):
    T = B * S
    nt = S // ts
    r = ts // hal
    return pl.pallas_call(
        _pool_kernel,
        grid=(B, nt),
        in_specs=[pl.BlockSpec((ts, POOL_W), lambda b, j: (b * nt + j, COL_POOL // POOL_W)),
                  pl.BlockSpec((hal, POOL_W),
                               lambda b, j: (jnp.maximum((b * nt + j) * r - 1, 0), COL_POOL // POOL_W)),
                  pl.BlockSpec((4, POOL_GW, POOL_GW), lambda b, j: (0, 0, 0)),
                  pl.BlockSpec((1, POOL_W), lambda b, j: (0, 0))],
        out_specs=pl.BlockSpec((ts, POOL_W), lambda b, j: (b * nt + j, 0)),
        out_shape=jax.ShapeDtypeStruct((T, POOL_W), BF16),
        compiler_params=_cparams(("parallel", "arbitrary")),
        name="ms_pool",
    )(h, h, pool_w, pool_scale)


CONV_PAD = 32
SUBLANES = 8


def _conv_kernel(uc_ref, up_ref, w_ref, b_ref, g_ref, bb_ref, o_ref, ext_ref, sh_ref):
    j = pl.program_id(1)
    ts = uc_ref.shape[0]
    pad = CONV_PAD

    def glu(u):
        u = u.astype(F32)
        return u[:, :CONV_W] * _sigmoid(u[:, CONV_W:])

    hp = glu(up_ref[up_ref.shape[0] - pad:, :])
    ext_ref[0:pad, :] = jnp.where(j > 0, hp, jnp.zeros_like(hp))
    ext_ref[pad:pad + ts, :] = glu(uc_ref[...])
    ext_ref[pad + ts:, :] = jnp.zeros((SUBLANES, CONV_W), F32)
    for r in range(SUBLANES):
        sh_ref[r] = ext_ref[r:r + ts + pad, :]
    acc = jnp.zeros((ts, CONV_W), F32)
    for k in range(CONV_K):
        off = pad - (CONV_K - 1) + k
        r = off % SUBLANES
        acc = acc + sh_ref[r, off - r:off - r + ts, :] * w_ref[k:k + 1, :]
    y = _layer_norm(acc + b_ref[...], g_ref[...], bb_ref[...])
    o_ref[...] = (y * _sigmoid(y)).astype(o_ref.dtype)


def _conv(h, dw_w, dw_b, ln_g, ln_b, B, S, ts=TOKEN_TILE, hal=HALO):
    T = B * S
    nt = S // ts
    r = ts // hal
    vec = pl.BlockSpec((1, CONV_W), lambda b, j: (0, 0))
    return pl.pallas_call(
        _conv_kernel,
        grid=(B, nt),
        in_specs=[pl.BlockSpec((ts, 2 * CONV_W), lambda b, j: (b * nt + j, COL_CONV // (2 * CONV_W))),
                  pl.BlockSpec((hal, 2 * CONV_W),
                               lambda b, j: (jnp.maximum((b * nt + j) * r - 1, 0), COL_CONV // (2 * CONV_W))),
                  pl.BlockSpec((32, CONV_W), lambda b, j: (0, 0)),
                  vec, vec, vec],
        out_specs=pl.BlockSpec((ts, CONV_W), lambda b, j: (b * nt + j, 0)),
        out_shape=jax.ShapeDtypeStruct((T, CONV_W), BF16),
        scratch_shapes=[pltpu.VMEM((ts + CONV_PAD + SUBLANES, CONV_W), F32),
                        pltpu.VMEM((SUBLANES, ts + CONV_PAD, CONV_W), F32)],
        compiler_params=_cparams(("parallel", "arbitrary")),
        name="conf_conv",
    )(h, h, dw_w, dw_b, ln_g, ln_b)


def _merge_kernel(x_ref, yf_ref, yp_ref, yc_ref, ys_ref, wg_ref, pf_ref, pp_ref, pc_ref, ps_ref,
                  wo_ref, g_ref, b_ref, x1_ref, x1t_ref):
    xb = x_ref[...].astype(BF16)
    merged = None
    for br, (y_ref, p_ref) in enumerate(((yf_ref, pf_ref), (yp_ref, pp_ref), (yc_ref, pc_ref), (ys_ref, ps_ref))):
        gate = _sigmoid(jnp.dot(xb, wg_ref[:, br * D_MODEL:(br + 1) * D_MODEL], preferred_element_type=F32))
        term = gate * jnp.dot(y_ref[...], p_ref[...], preferred_element_type=F32)
        merged = term if merged is None else merged + term
    mix = jnp.dot(merged.astype(BF16), wo_ref[...], preferred_element_type=F32)
    x1 = _layer_norm(ALPHA * x_ref[...] + mix, g_ref[...], b_ref[...])
    x1_ref[...] = x1
    x1t_ref[...] = x1.T.astype(BF16)


def _merge(x, yf, yp, yc, ys, wg, pf, pp, pc, ps, wo, g, b, tm=MERGE_TILE):
    T = x.shape[0]
    const = lambda i: (0, 0)
    tok = lambda w: pl.BlockSpec((tm, w), lambda i: (i, 0))
    wspec = lambda a: pl.BlockSpec(a.shape, const)
    return pl.pallas_call(
        _merge_kernel,
        grid=(T // tm,),
        in_specs=[tok(D_MODEL), tok(512), tok(512), tok(512), tok(512),
                  wspec(wg), wspec(pf), wspec(pp), wspec(pc), wspec(ps), wspec(wo), wspec(g), wspec(b)],
        out_specs=[tok(D_MODEL), pl.BlockSpec((D_MODEL, tm), lambda i: (0, i))],
        out_shape=[jax.ShapeDtypeStruct((T, D_MODEL), F32), jax.ShapeDtypeStruct((D_MODEL, T), BF16)],
        compiler_params=_cparams(("parallel",)),
        name="merge_ln1",
    )(x, yf, yp, yc, ys, wg, pf, pp, pc, ps, wo, g, b)


def _wf_kernel(k_ref, wq_ref, o_ref):
    o_ref[...] = jnp.dot(k_ref[0], wq_ref[...], preferred_element_type=F32).astype(o_ref.dtype)


def _peer_score_weights(wq_t, keys):
    nblk = wq_t.shape[0] // N_KEYS
    return pl.pallas_call(
        _wf_kernel,
        grid=(nblk,),
        in_specs=[pl.BlockSpec((1, N_KEYS, N_KEYS), lambda j: (j % 2, 0, 0)),
                  pl.BlockSpec((N_KEYS, D_MODEL), lambda j: (j, 0))],
        out_specs=pl.BlockSpec((N_KEYS, D_MODEL), lambda j: (j, 0)),
        out_shape=jax.ShapeDtypeStruct(wq_t.shape, BF16),
        compiler_params=_cparams(("parallel",)),
        name="peer_wf",
    )(keys, wq_t)


N_RANK = PEER_TOPK + 1


def _n_cand():
    return [(r, c) for r in range(N_RANK) for c in range(N_RANK) if (r + 1) * (c + 1) <= N_RANK]


def _gelu(z):
    return 0.5 * z * (1.0 + lax.erf(z * (1.0 / math.sqrt(2.0))))


def _extract_top(work_ref, out_ref, n_slab, n_out):
    nh, tm = work_ref.shape[1:]
    unroll = 8 if n_slab % 8 == 0 else 4
    assert n_slab % unroll == 0
    for c in range(tm // LANE):
        ls = slice(c * LANE, (c + 1) * LANE)

        def insert(it, best):
            best = list(best)
            for k in range(unroll):
                w = work_ref[it * unroll + k, :, ls]
                for r in range(n_out):
                    best[r], w = jnp.maximum(best[r], w), jnp.minimum(best[r], w)
            return tuple(best)

        best = lax.fori_loop(0, n_slab // unroll, insert,
                             tuple(jnp.full((nh, LANE), NEG_BIG, F32) for _ in range(n_out)))
        for r in range(n_out):
            out_ref[r, :, ls] = best[r]


def _count_below(s, thr):
    c8 = s > thr[7]
    c4 = s > jnp.where(c8, thr[11], thr[3])
    c2 = s > jnp.where(c8, jnp.where(c4, thr[13], thr[9]), jnp.where(c4, thr[5], thr[1]))
    lo = jnp.where(c4, jnp.where(c2, thr[6], thr[4]), jnp.where(c2, thr[2], thr[0]))
    hi = jnp.where(c4, jnp.where(c2, thr[14], thr[12]), jnp.where(c2, thr[10], thr[8]))
    c1 = s > jnp.where(c8, hi, lo)
    one = lambda cond, w: jnp.where(cond, w, 0.0)
    return one(c8, 8.0) + one(c4, 4.0) + one(c2, 2.0) + one(c1, 1.0) + one(s > thr[15], 1.0)


def _dup_bf16(v):
    u = pltpu.bitcast(v.astype(BF16).astype(F32), jnp.uint32)
    return u | (u >> 16)


def _row_bf16(slab, hh):
    return pltpu.bitcast(jnp.broadcast_to(slab[hh:hh + 1, :], (N_KEYS // 2, LANE)), BF16)


def _peer_kernel(xt_ref, wf_ref, u0_ref, u_ref, vt_ref, x1_ref, g_ref, b_ref, sc_ref, x2_ref,
                 n_scr, e1_scr, rank_scr, e2_scr, s2_scr, y_scr, wt0_scr, wt1_scr, ht0_scr, ht1_scr,
                 work_scr, top_scr, cand_scr, csel_scr, thr_scr, x8_scr):
    g = pl.program_id(1)
    ng = pl.num_programs(1)
    tm = xt_ref.shape[1]
    eb = u_ref.shape[0] // 2
    nh = PEER_HEADS
    rows = N_KEYS * nh

    @pl.when(g == 0)
    def _select():
        y_scr[...] = jnp.zeros_like(y_scr)
        x8_scr[...] = (xt_ref[...].astype(F32) * sc_ref[0:1, 0:1]).astype(FP8)
        ht0_scr[...] = jnp.dot(u0_ref[...], x8_scr[...], preferred_element_type=F32)
        st = jnp.dot(wf_ref[...], xt_ref[...], preferred_element_type=F32)
        for half in range(2):
            work_scr[...] = st[half * rows:(half + 1) * rows].reshape(N_KEYS, nh, tm)
            _extract_top(work_scr, top_scr.at[half], N_KEYS, N_RANK)
        cands = _n_cand()
        for ci, (r, c) in enumerate(cands):
            cand_scr[ci] = top_scr[0, r] + top_scr[1, c]
        _extract_top(cand_scr, csel_scr, len(cands), N_RANK)
        m0 = csel_scr[0]
        zsum = jnp.zeros_like(m0)
        for r in range(PEER_TOPK):
            zsum = zsum + jnp.exp(csel_scr[r] - m0)
        tau = 0.5 * (csel_scr[PEER_TOPK - 1] + csel_scr[PEER_TOPK])
        a0 = top_scr[0, 0]
        for c in range(PEER_TOPK):
            thr_scr[c] = tau - top_scr[1, c]
        work_scr[...] = st[0:rows].reshape(N_KEYS, nh, tm)

        def key_body(i, carry):
            s = work_scr[i]
            cnt = _count_below(s, [thr_scr[c] for c in range(PEER_TOPK)])
            r0 = pl.multiple_of(i * nh, nh)
            n_scr[pl.ds(r0, nh), :] = _dup_bf16(cnt)
            e1_scr[pl.ds(r0, nh), :] = _dup_bf16(jnp.exp(s - a0) / zsum)
            return carry
        lax.fori_loop(0, N_KEYS, key_body, 0)

        s2_scr[...] = st[2 * rows:3 * rows]
        grp = 16
        for hh in range(nh):
            brow = [top_scr[1, c][hh:hh + 1, :] for c in range(N_RANK)]
            mid = [0.5 * (brow[c] + brow[c + 1]) for c in range(PEER_TOPK)]
            mid_up = mid[::-1]

            def rank_body(jg, carry):
                r0 = pl.multiple_of(hh * N_KEYS + jg * grp, grp)
                s = s2_scr[pl.ds(r0, grp), :]
                cnt = float(PEER_TOPK) - _count_below(s, mid_up)
                rank_scr[pl.ds(r0, grp), :] = cnt.astype(BF16)
                e2_scr[pl.ds(r0, grp), :] = jnp.exp(s - brow[0]).astype(BF16)
                return carry
            lax.fori_loop(0, N_KEYS // grp, rank_body, 0)

    nsub = eb // N_KEYS

    def gate_block(sub, chunks):
        for ii in range(nsub):
            base = pl.multiple_of(((g * 2 + sub) * nsub + ii) * nh, nh)
            for c in chunks:
                ls = slice(c * LANE, (c + 1) * LANE)
                n_i = n_scr[pl.ds(base, nh), ls]
                e_i = e1_scr[pl.ds(base, nh), ls]
                acc = None
                for hh in range(nh):
                    rs = slice(hh * N_KEYS, (hh + 1) * N_KEYS)
                    term = jnp.where(rank_scr[rs, ls] < _row_bf16(n_i, hh), e2_scr[rs, ls],
                                     jnp.zeros((), BF16)) * _row_bf16(e_i, hh)
                    acc = term if acc is None else acc + term
                act = _gelu(ht_scr[sub][ii * N_KEYS:(ii + 1) * N_KEYS, ls] * sc_ref[0:1, 1:2]).astype(BF16)
                wt_scr[sub][ii * N_KEYS:(ii + 1) * N_KEYS, ls] = act * acc

    ht_scr = (ht0_scr, ht1_scr)
    wt_scr = (wt0_scr, wt1_scr)
    half = tm // 2
    cpl = half // LANE
    for sub in range(2):
        for hf in range(2):
            ln = slice(hf * half, (hf + 1) * half)
            ht_scr[1 - sub][:, ln] = jnp.dot(u_ref[sub * eb:(sub + 1) * eb, :], x8_scr[:, ln],
                                             preferred_element_type=F32)
            gate_block(sub, range(hf * cpl, (hf + 1) * cpl))
            y_scr[:, ln] += jnp.dot(vt_ref[:, sub * eb:(sub + 1) * eb], wt_scr[sub][:, ln],
                                    preferred_element_type=F32)

    @pl.when(g == ng - 1)
    def _finish():
        z = ALPHA * x1_ref[...] + y_scr[...].T
        x2 = _layer_norm(z, g_ref[...], b_ref[...])
        x2_ref[...] = x2


def _peer(x1t, wf, u, v, x1, g, b, x_bound, tm=TOKEN_TILE, eb=PEER_EXPERT_BLOCK):
    T = x1.shape[0]
    ne = u.shape[0]
    ncand = len(_n_cand())
    rows = N_KEYS * PEER_HEADS
    const = lambda t, k: (0, 0)
    slab = lambda n: pltpu.VMEM((n, PEER_HEADS, tm), F32)
    pow2 = lambda bound: jnp.exp2(jnp.floor(jnp.log2(FP8_TARGET / jnp.maximum(bound, 1e-30))))
    s_u = pow2(jnp.max(jnp.abs(u)))
    s_x = pow2(x_bound)
    scales = jnp.zeros((1, LANE), F32).at[0, 0].set(s_x).at[0, 1].set(1.0 / (s_x * s_u))
    u = (u * s_u).astype(FP8)
    u_roll = jnp.roll(u, -eb, axis=0)
    vt = v.reshape(ne // (2 * eb), 2 * eb, D_MODEL).transpose(0, 2, 1)
    return pl.pallas_call(
        _peer_kernel,
        grid=(T // tm, ne // (2 * eb)),
        in_specs=[pl.BlockSpec((D_MODEL, tm), lambda t, k: (0, t)),
                  pl.BlockSpec(wf.shape, const),
                  pl.BlockSpec((eb, D_MODEL), const),
                  pl.BlockSpec((2 * eb, D_MODEL), lambda t, k: (k, 0)),
                  pl.BlockSpec((None, D_MODEL, 2 * eb), lambda t, k: (k, 0, 0)),
                  pl.BlockSpec((tm, D_MODEL), lambda t, k: (t, 0)),
                  pl.BlockSpec((1, D_MODEL), const),
                  pl.BlockSpec((1, D_MODEL), const),
                  pl.BlockSpec((1, LANE), const)],
        out_specs=pl.BlockSpec((tm, D_MODEL), lambda t, k: (t, 0)),
        out_shape=jax.ShapeDtypeStruct((T, D_MODEL), F32),
        scratch_shapes=[pltpu.VMEM((rows, tm), jnp.uint32), pltpu.VMEM((rows, tm), jnp.uint32),
                        pltpu.VMEM((rows, tm), BF16), pltpu.VMEM((rows, tm), BF16),
                        pltpu.VMEM((rows, tm), F32),
                        pltpu.VMEM((D_MODEL, tm), F32),
                        pltpu.VMEM((eb, tm), BF16), pltpu.VMEM((eb, tm), BF16),
                        pltpu.VMEM((eb, tm), F32), pltpu.VMEM((eb, tm), F32),
                        slab(N_KEYS), pltpu.VMEM((2, N_RANK, PEER_HEADS, tm), F32),
                        slab(ncand), slab(N_RANK), slab(N_RANK), pltpu.VMEM((D_MODEL, tm), FP8)],
        compiler_params=_cparams(("parallel", "arbitrary")),
        name="peer_ln2",
    )(x1t, wf, u, u_roll, vt, x1, g, b, scales)


def _prep_w_in(w_in):
    sizes = (512, 512, 512, 8, 512, 128, 128, 512, 1024, 4096)
    offs = np.cumsum((0,) + sizes)
    q_f, k_f, v_f, f_l, q_s, k_s, v_s, x_pool, x_conv, gl = (w_in[:, offs[i]:offs[i + 1]] for i in range(10))
    scale = HEAD_DIM ** -0.5
    q_s = q_s.reshape(D_MODEL, 2, 4, HEAD_DIM).transpose(0, 2, 1, 3).reshape(D_MODEL, 512)
    zeros = lambda n: jnp.zeros((D_MODEL, n), w_in.dtype)
    w_h = jnp.concatenate([q_f * scale, k_f, v_f, q_s * scale, x_pool, k_s, v_s, f_l, zeros(120), zeros(128), x_conv],
                          axis=1)
    return w_h.astype(BF16), gl.astype(BF16)


def _prep_peer(wq, k1, k2):
    wq_t = wq.T.astype(BF16)
    keys = jnp.stack([k1, k2]).astype(BF16)
    wf = _peer_score_weights(wq_t, keys)
    wf = wf.reshape(PEER_HEADS, 2, N_KEYS, D_MODEL)
    kh = lambda half: wf[:, half].transpose(1, 0, 2).reshape(N_KEYS * PEER_HEADS, D_MODEL)
    hk = wf[:, 1].reshape(PEER_HEADS * N_KEYS, D_MODEL)
    return jnp.concatenate([kh(0), kh(1), hk], axis=0)


def _pad_lanes(v, n):
    return jnp.zeros((1, n), F32).at[0, :v.shape[0]].set(v.astype(F32))


def kernel(x, w_in, b_f, swa_sinks, pool_w, pool_scale, dw_w, dw_b, conv_ln_g, conv_ln_b, p_fox, p_swa, p_pool,
           p_conv, w_out, ln1_g, ln1_b, peer_wq, peer_k1, peer_k2, peer_u, peer_v, ln2_g, ln2_b):
    B, S, D = x.shape
    T = B * S
    xf = x.reshape(T, D)
    row = lambda v: v.reshape(1, -1).astype(F32)
    for l in range(DEPTH):
        w_h, w_gate = _prep_w_in(w_in[l])
        h = _inproj(xf, w_h)
        aq, ak = _decay(h, _pad_lanes(b_f[l], LANE), B, S)
        y_fox = _fox(h, aq, ak, B, S)
        y_swa = _swa(h, swa_sinks[l].astype(F32), B, S)
        y_pool = _pool(h, pool_w[l].astype(BF16), row(pool_scale[l]), B, S)
        dw = jnp.zeros((32, CONV_W), F32).at[:CONV_K].set(dw_w[l])
        y_conv = _conv(h, dw, row(dw_b[l]), row(conv_ln_g[l]), row(conv_ln_b[l]), B, S)
        ps = p_swa[l].reshape(2, 4, HEAD_DIM, D).transpose(1, 0, 2, 3).reshape(512, D)
        x1, x1t = _merge(xf, y_fox, y_pool, y_conv, y_swa, w_gate, p_fox[l].astype(BF16),
                         p_pool[l].astype(BF16), p_conv[l].astype(BF16), ps.astype(BF16),
                         w_out[l].astype(BF16), row(ln1_g[l]), row(ln1_b[l]))
        wf = _prep_peer(peer_wq[l], peer_k1[l], peer_k2[l])
        x_bound = math.sqrt(D_MODEL) * jnp.max(jnp.abs(ln1_g[l])) + jnp.max(jnp.abs(ln1_b[l]))
        xf = _peer(x1t, wf, peer_u[l], peer_v[l].astype(BF16), x1, row(ln2_g[l]), row(ln2_b[l]), x_bound)
    return xf.reshape(B, S, D)
```

```python
import functools
import math

import numpy as np
import jax
import jax.numpy as jnp
from jax import lax
from jax.experimental import pallas as pl
from jax.experimental.pallas import tpu as pltpu

F32 = jnp.float32
BF16 = jnp.bfloat16
FP8 = jnp.float8_e4m3fn
FP8_TARGET = 256.0

D_MODEL = 1024
DEPTH = 2
HEAD_DIM = 64
SWA_HEADS = 8
WINDOW = 128
POOL_WINDOWS = (2, 4, 8, 16)
POOL_GW = 128
POOL_W = 512
CONV_W = 512
CONV_K = 31
PEER_HEADS = 8
N_KEYS = 128
PEER_TOPK = 16
LN_EPS = 1e-5
ALPHA = (2 * DEPTH) ** 0.25
NEG_BIG = -1e30

COL_QF, COL_KF, COL_VF, COL_QS, COL_POOL, COL_KS, COL_VS, COL_FL, COL_CONV = (
    0, 512, 1024, 1536, 2048, 2560, 2688, 2816, 3072)
LANE = 128
AUG_A = 6
AUG_B = 12

VMEM_LIMIT = 56 * 1024 * 1024

TOKEN_TILE = 512
MERGE_TILE = 256
SWA_BLOCK = WINDOW
HALO = 128
PEER_EXPERT_BLOCK = 256


def _cparams(sem):
    return pltpu.CompilerParams(dimension_semantics=sem, vmem_limit_bytes=VMEM_LIMIT)


def _layer_norm(z, g, b):
    mu = jnp.mean(z, axis=-1, keepdims=True)
    zc = z - mu
    var = jnp.mean(zc * zc, axis=-1, keepdims=True)
    return zc * lax.rsqrt(var + LN_EPS) * g + b


def _sigmoid(z):
    return 1.0 / (1.0 + jnp.exp(-z))


def _inproj_kernel(x_ref, w_ref, o_ref, *, n_chunk):
    x = x_ref[...].astype(BF16)
    for c in range(0, o_ref.shape[1], n_chunk):
        o_ref[:, c:c + n_chunk] = jnp.dot(
            x, w_ref[:, c:c + n_chunk], preferred_element_type=F32).astype(o_ref.dtype)


def _inproj(x, w, tm=TOKEN_TILE):
    T, K = x.shape
    N = w.shape[1]
    return pl.pallas_call(
        functools.partial(_inproj_kernel, n_chunk=1024),
        grid=(T // tm,),
        in_specs=[pl.BlockSpec((tm, K), lambda i: (i, 0)),
                  pl.BlockSpec((K, N), lambda i: (0, 0))],
        out_specs=pl.BlockSpec((tm, N), lambda i: (i, 0)),
        out_shape=jax.ShapeDtypeStruct((T, N), BF16),
        compiler_params=_cparams(("parallel",)),
        name="inproj",
    )(x, w)


def _split3(v):
    hi = v.astype(BF16)
    r1 = v - hi.astype(F32)
    mid = r1.astype(BF16)
    r2 = r1 - mid.astype(F32)
    return hi, mid, r2.astype(BF16)


def _decay_kernel(fl_ref, bf_ref, selq_ref, selk_ref, cq_ref, ck_ref, aq_ref, ak_ref, carry_ref):
    ts = fl_ref.shape[0]

    @pl.when(pl.program_id(1) == 0)
    def _():
        carry_ref[...] = jnp.zeros_like(carry_ref)

    z = fl_ref[...].astype(F32) + bf_ref[...]
    ls = jnp.minimum(z, 0.0) - jnp.log1p(jnp.exp(-jnp.abs(z)))
    row = lax.broadcasted_iota(jnp.int32, (ts, ts), 0)
    col = lax.broadcasted_iota(jnp.int32, (ts, ts), 1)
    tri = jnp.where(col <= row, 1.0, 0.0).astype(BF16)
    parts = jnp.concatenate(_split3(ls), axis=1)
    cs = jnp.dot(tri, parts, preferred_element_type=F32)
    c = cs[:, :LANE] + cs[:, LANE:2 * LANE] + cs[:, 2 * LANE:] + carry_ref[0:1, :]
    carry_ref[...] = jnp.broadcast_to(c[ts - 1:ts, :], carry_ref.shape)
    cparts = jnp.concatenate(_split3(c), axis=1)
    aq_ref[...] = (jnp.dot(cparts, selq_ref[...], preferred_element_type=F32) + cq_ref[...]).astype(BF16)
    ak_ref[...] = (jnp.dot(cparts, selk_ref[...], preferred_element_type=F32) + ck_ref[...]).astype(BF16)


def _decay_consts():
    selq = np.zeros((3 * LANE, 4 * LANE), np.float32)
    selk = np.zeros((3 * LANE, 4 * LANE), np.float32)
    cq = np.zeros((1, 4 * LANE), np.float32)
    ck = np.zeros((1, 4 * LANE), np.float32)
    for p in range(4):
        for part in range(3):
            for hh in range(2):
                base = p * LANE + hh * AUG_A
                selq[part * LANE + 2 * p + hh, base + part] = 1.0
                cq[0, base + 3 + part] = 1.0
                ck[0, base + part] = 1.0
                selk[part * LANE + 2 * p + hh, base + 3 + part] = -1.0
    return (jnp.asarray(selq, BF16), jnp.asarray(selk, BF16), jnp.asarray(cq), jnp.asarray(ck))


def _decay(h, bf_pad, B, S, ts=TOKEN_TILE):
    T = B * S
    nt = S // ts
    selq, selk, cq, ck = _decay_consts()
    const = lambda b, j: (0, 0)
    return pl.pallas_call(
        _decay_kernel,
        grid=(B, nt),
        in_specs=[pl.BlockSpec((ts, LANE), lambda b, j: (b * nt + j, COL_FL // LANE)),
                  pl.BlockSpec((1, LANE), const),
                  pl.BlockSpec((3 * LANE, 4 * LANE), const),
                  pl.BlockSpec((3 * LANE, 4 * LANE), const),
                  pl.BlockSpec((1, 4 * LANE), const),
                  pl.BlockSpec((1, 4 * LANE), const)],
        out_specs=[pl.BlockSpec((ts, 4 * LANE), lambda b, j: (b * nt + j, 0)),
                   pl.BlockSpec((ts, 4 * LANE), lambda b, j: (b * nt + j, 0))],
        out_shape=[jax.ShapeDtypeStruct((T, 4 * LANE), BF16)] * 2,
        scratch_shapes=[pltpu.VMEM((8, LANE), F32)],
        compiler_params=_cparams(("parallel", "arbitrary")),
        name="fox_decay",
    )(h, bf_pad, selq, selk, cq, ck)


def _fox_kernel(q_ref, aq_ref, k_ref, ak_ref, v_ref, o_ref, m_scr, acc_scr, s0_scr, s1_scr):
    tq = q_ref.shape[0]
    qi = pl.program_id(2)
    lane2 = lax.broadcasted_iota(jnp.int32, (1, 2 * LANE), 1)
    head_mask = (
        (lane2 < HEAD_DIM) | ((lane2 >= LANE) & (lane2 < LANE + AUG_A)),
        ((lane2 >= HEAD_DIM) & (lane2 < LANE)) | ((lane2 >= LANE + AUG_A) & (lane2 < LANE + AUG_B)),
    )
    qf = jnp.concatenate([q_ref[...], aq_ref[...]], axis=1)
    qs = [jnp.where(mk, qf, jnp.zeros_like(qf)) for mk in head_mask]
    ones_col = jnp.where(lax.broadcasted_iota(jnp.int32, (tq, LANE), 1) == 0, 1.0, 0.0).astype(BF16)
    row = lax.broadcasted_iota(jnp.int32, (tq, tq), 0)
    col = lax.broadcasted_iota(jnp.int32, (tq, tq), 1)

    m_scr[...] = jnp.full(m_scr.shape, NEG_BIG, F32)
    acc_scr[...] = jnp.zeros(acc_scr.shape, F32)

    def logits(j, dst):
        off = pl.multiple_of(j * tq, tq)
        kf = jnp.concatenate([k_ref[pl.ds(off, tq), :], ak_ref[pl.ds(off, tq), :]], axis=1)
        for x in range(2):
            dst[x] = lax.dot_general(qs[x], kf, (((1,), (1,)), ((), ())), preferred_element_type=F32)

    def accumulate(j, src, masked):
        off = pl.multiple_of(j * tq, tq)
        vf = jnp.concatenate([v_ref[pl.ds(off, tq), :], ones_col], axis=1)
        for x in range(2):
            s = src[x]
            if masked:
                s = jnp.where(col <= row, s, NEG_BIG)
            m_prev = m_scr[x]
            m_new = jnp.maximum(m_prev, jnp.max(s, axis=1, keepdims=True))
            alpha = jnp.exp(m_prev - m_new)
            p = jnp.exp(s - jnp.concatenate([m_new] * (tq // LANE), axis=1))
            acc_scr[x] = (acc_scr[x] * jnp.concatenate([alpha, alpha], axis=1)
                          + jnp.dot(p.astype(BF16), vf, preferred_element_type=F32))
            m_scr[x] = m_new

    logits(0, s0_scr)

    def body(jj, carry):
        j = 2 * jj
        logits(j + 1, s1_scr)
        accumulate(j, s0_scr, False)
        logits(j + 2, s0_scr)
        accumulate(j + 1, s1_scr, False)
        return carry

    lax.fori_loop(0, qi // 2, body, 0)

    @pl.when(qi % 2 == 0)
    def _():
        accumulate(qi, s0_scr, True)

    @pl.when(qi % 2 == 1)
    def _():
        logits(qi, s1_scr)
        accumulate(qi - 1, s0_scr, False)
        accumulate(qi, s1_scr, True)

    outs = []
    for x in range(2):
        acc = acc_scr[x]
        outs.append(acc[:, :LANE] / acc[:, LANE:LANE + 1])
    lane = lax.broadcasted_iota(jnp.int32, (tq, LANE), 1)
    o_ref[...] = jnp.where(lane < HEAD_DIM, outs[0], outs[1]).astype(o_ref.dtype)


def _fox(h, aq, ak, B, S, tq=TOKEN_TILE):
    T = B * S
    nq = S // tq
    return pl.pallas_call(
        _fox_kernel,
        grid=(B, 4, nq),
        in_specs=[pl.BlockSpec((tq, LANE), lambda b, p, i: (b * nq + i, COL_QF // LANE + p)),
                  pl.BlockSpec((tq, LANE), lambda b, p, i: (b * nq + i, p)),
                  pl.BlockSpec((S, LANE), lambda b, p, i: (b, COL_KF // LANE + p)),
                  pl.BlockSpec((S, LANE), lambda b, p, i: (b, p)),
                  pl.BlockSpec((S, LANE), lambda b, p, i: (b, COL_VF // LANE + p))],
        out_specs=pl.BlockSpec((tq, LANE), lambda b, p, i: (b * nq + i, p)),
        out_shape=jax.ShapeDtypeStruct((T, 4 * LANE), BF16),
        scratch_shapes=[pltpu.VMEM((2, tq, LANE), F32), pltpu.VMEM((2, tq, 2 * LANE), F32),
                        pltpu.VMEM((2, tq, tq), F32), pltpu.VMEM((2, tq, tq), F32)],
        compiler_params=_cparams(("parallel", "parallel", "arbitrary")),
        name="fox_attn",
    )(h, aq, h, ak, h)


def _swa_kernel(sink_ref, q_ref, kc_ref, kp_ref, vc_ref, vp_ref, o_ref):
    n = pl.program_id(1)
    blk = q_ref.shape[0]
    kb = jnp.concatenate([kp_ref[...], kc_ref[...]], axis=0)
    vb = jnp.concatenate([vp_ref[...], vc_ref[...]], axis=0)
    qi = lax.broadcasted_iota(jnp.int32, (blk, 2 * blk), 0)
    kj = lax.broadcasted_iota(jnp.int32, (blk, 2 * blk), 1)
    dist = qi + blk - kj
    valid = (dist >= 0) & (dist < WINDOW) & ((kj >= blk) | (n > 0))
    distf = dist.astype(F32)
    lane = lax.broadcasted_iota(jnp.int32, (blk, LANE), 1)
    lo = lane < HEAD_DIM
    heads = [(m, half) for m in range(4) for half in range(2)]
    logits = []
    for m, half in heads:
        hd = m + 4 * half
        slope = 2.0 ** (-8.0 * (hd + 1) / SWA_HEADS)
        qm = q_ref[:, m * LANE:(m + 1) * LANE]
        qh = jnp.where(lo if half == 0 else jnp.logical_not(lo), qm, jnp.zeros_like(qm))
        s = lax.dot_general(qh, kb, (((1,), (1,)), ((), ())), preferred_element_type=F32)
        logits.append(jnp.where(valid, s - slope * distf, NEG_BIG))
    probs = []
    for (m, half), s in zip(heads, logits):
        sink = sink_ref[m + 4 * half]
        mx = jnp.maximum(jnp.max(s, axis=1, keepdims=True), sink)
        e = jnp.exp(s - mx)
        den = jnp.sum(e, axis=1, keepdims=True) + jnp.exp(sink - mx)
        probs.append((e / den).astype(BF16))
    outs = [jnp.dot(p, vb, preferred_element_type=F32) for p in probs]
    for m in range(4):
        o_ref[:, m * LANE:(m + 1) * LANE] = jnp.where(lo, outs[2 * m], outs[2 * m + 1]).astype(o_ref.dtype)


def _swa(h, sinks, B, S, blk=SWA_BLOCK):
    T = B * S
    nb = S // blk
    cur = lambda c: (lambda b, n: (b * nb + n, c))
    prev = lambda c: (lambda b, n: (b * nb + jnp.maximum(n - 1, 0), c))
    return pl.pallas_call(
        _swa_kernel,
        grid=(B, nb),
        in_specs=[pl.BlockSpec(memory_space=pltpu.SMEM),
                  pl.BlockSpec((blk, 4 * LANE), cur(COL_QS // (4 * LANE))),
                  pl.BlockSpec((blk, LANE), cur(COL_KS // LANE)),
                  pl.BlockSpec((blk, LANE), prev(COL_KS // LANE)),
                  pl.BlockSpec((blk, LANE), cur(COL_VS // LANE)),
                  pl.BlockSpec((blk, LANE), prev(COL_VS // LANE))],
        out_specs=pl.BlockSpec((blk, 4 * LANE), lambda b, n: (b * nb + n, 0)),
        out_shape=jax.ShapeDtypeStruct((T, 4 * LANE), BF16),
        compiler_params=_cparams(("parallel", "arbitrary")),
        name="swa_attn",
    )(sinks, h, h, h, h, h)


def _pool_kernel(xc_ref, xp_ref, w_ref, sc_ref, o_ref):
    j = pl.program_id(1)
    ts = xc_ref.shape[0]
    hal = xp_ref.shape[0]
    r = lax.broadcasted_iota(jnp.int32, (ts, ts + hal), 0)
    c = lax.broadcasted_iota(jnp.int32, (ts, ts + hal), 1) - hal
    t_glob = (lax.broadcasted_iota(jnp.int32, (ts, LANE), 0) + j * ts + 1).astype(F32)
    has_prev = j > 0
    for g, w in enumerate(POOL_WINDOWS):
        xg = xc_ref[:, g * LANE:(g + 1) * LANE]
        xp = xp_ref[:, g * LANE:(g + 1) * LANE]
        xp = jnp.where(has_prev, xp, jnp.zeros_like(xp))
        ext = jnp.concatenate([xp, xg], axis=0)
        band = jnp.where((c <= r) & (c > r - w), 1.0, 0.0).astype(BF16)
        win = jnp.dot(band, ext, preferred_element_type=F32)
        cnt = jnp.minimum(t_glob, float(w))
        pooled = win / cnt - xg.astype(F32)
        y = jnp.dot(pooled.astype(BF16), w_ref[g], preferred_element_type=F32)
        o_ref[:, g * LANE:(g + 1) * LANE] = (y * sc_ref[:, g * LANE:(g + 1) * LANE]).astype(o_ref.dtype)


def _pool(h, pool_w, pool_scale, B, S, ts=TOKEN_TILE, hal=HALO):
    T = B * S
    nt = S // ts
    r = ts // hal
    return pl.pallas_call(
        _pool_kernel,
        grid=(B, nt),
        in_specs=[pl.BlockSpec((ts, POOL_W), lambda b, j: (b * nt + j, COL_POOL // POOL_W)),
                  pl.BlockSpec((hal, POOL_W),
                               lambda b, j: (jnp.maximum((b * nt + j) * r - 1, 0), COL_POOL // POOL_W)),
                  pl.BlockSpec((4, POOL_GW, POOL_GW), lambda b, j: (0, 0, 0)),
                  pl.BlockSpec((1, POOL_W), lambda b, j: (0, 0))],
        out_specs=pl.BlockSpec((ts, POOL_W), lambda b, j: (b * nt + j, 0)),
        out_shape=jax.ShapeDtypeStruct((T, POOL_W), BF16),
        compiler_params=_cparams(("parallel", "arbitrary")),
        name="ms_pool",
    )(h, h, pool_w, pool_scale)


CONV_PAD = 32
SUBLANES = 8


def _conv_kernel(uc_ref, up_ref, w_ref, b_ref, g_ref, bb_ref, o_ref, ext_ref, sh_ref):
    j = pl.program_id(1)
    ts = uc_ref.shape[0]
    pad = CONV_PAD

    def glu(u):
        u = u.astype(F32)
        return u[:, :CONV_W] * _sigmoid(u[:, CONV_W:])

    hp = glu(up_ref[up_ref.shape[0] - pad:, :])
    ext_ref[0:pad, :] = jnp.where(j > 0, hp, jnp.zeros_like(hp))
    ext_ref[pad:pad + ts, :] = glu(uc_ref[...])
    ext_ref[pad + ts:, :] = jnp.zeros((SUBLANES, CONV_W), F32)
    for r in range(SUBLANES):
        sh_ref[r] = ext_ref[r:r + ts + pad, :]
    acc = jnp.zeros((ts, CONV_W), F32)
    for k in range(CONV_K):
        off = pad - (CONV_K - 1) + k
        r = off % SUBLANES
        acc = acc + sh_ref[r, off - r:off - r + ts, :] * w_ref[k:k + 1, :]
    y = _layer_norm(acc + b_ref[...], g_ref[...], bb_ref[...])
    o_ref[...] = (y * _sigmoid(y)).astype(o_ref.dtype)


def _conv(h, dw_w, dw_b, ln_g, ln_b, B, S, ts=TOKEN_TILE, hal=HALO):
    T = B * S
    nt = S // ts
    r = ts // hal
    vec = pl.BlockSpec((1, CONV_W), lambda b, j: (0, 0))
    return pl.pallas_call(
        _conv_kernel,
        grid=(B, nt),
        in_specs=[pl.BlockSpec((ts, 2 * CONV_W), lambda b, j: (b * nt + j, COL_CONV // (2 * CONV_W))),
                  pl.BlockSpec((hal, 2 * CONV_W),
                               lambda b, j: (jnp.maximum((b * nt + j) * r - 1, 0), COL_CONV // (2 * CONV_W))),
                  pl.BlockSpec((32, CONV_W), lambda b, j: (0, 0)),
                  vec, vec, vec],
        out_specs=pl.BlockSpec((ts, CONV_W), lambda b, j: (b * nt + j, 0)),
        out_shape=jax.ShapeDtypeStruct((T, CONV_W), BF16),
        scratch_shapes=[pltpu.VMEM((ts + CONV_PAD + SUBLANES, CONV_W), F32),
                        pltpu.VMEM((SUBLANES, ts + CONV_PAD, CONV_W), F32)],
        compiler_params=_cparams(("parallel", "arbitrary")),
        name="conf_conv",
    )(h, h, dw_w, dw_b, ln_g, ln_b)


def _merge_kernel(x_ref, yf_ref, yp_ref, yc_ref, ys_ref, wg_ref, pf_ref, pp_ref, pc_ref, ps_ref,
                  wo_ref, g_ref, b_ref, x1_ref, x1t_ref):
    xb = x_ref[...].astype(BF16)
    merged = None
    for br, (y_ref, p_ref) in enumerate(((yf_ref, pf_ref), (yp_ref, pp_ref), (yc_ref, pc_ref), (ys_ref, ps_ref))):
        gate = _sigmoid(jnp.dot(xb, wg_ref[:, br * D_MODEL:(br + 1) * D_MODEL], preferred_element_type=F32))
        term = gate * jnp.dot(y_ref[...], p_ref[...], preferred_element_type=F32)
        merged = term if merged is None else merged + term
    mix = jnp.dot(merged.astype(BF16), wo_ref[...], preferred_element_type=F32)
    x1 = _layer_norm(ALPHA * x_ref[...] + mix, g_ref[...], b_ref[...])
    x1_ref[...] = x1
    x1t_ref[...] = x1.T.astype(BF16)


def _merge(x, yf, yp, yc, ys, wg, pf, pp, pc, ps, wo, g, b, tm=MERGE_TILE):
    T = x.shape[0]
    const = lambda i: (0, 0)
    tok = lambda w: pl.BlockSpec((tm, w), lambda i: (i, 0))
    wspec = lambda a: pl.BlockSpec(a.shape, const)
    return pl.pallas_call(
        _merge_kernel,
        grid=(T // tm,),
        in_specs=[tok(D_MODEL), tok(512), tok(512), tok(512), tok(512),
                  wspec(wg), wspec(pf), wspec(pp), wspec(pc), wspec(ps), wspec(wo), wspec(g), wspec(b)],
        out_specs=[tok(D_MODEL), pl.BlockSpec((D_MODEL, tm), lambda i: (0, i))],
        out_shape=[jax.ShapeDtypeStruct((T, D_MODEL), F32), jax.ShapeDtypeStruct((D_MODEL, T), BF16)],
        compiler_params=_cparams(("parallel",)),
        name="merge_ln1",
    )(x, yf, yp, yc, ys, wg, pf, pp, pc, ps, wo, g, b)


def _wf_kernel(k_ref, wq_ref, o_ref):
    o_ref[...] = jnp.dot(k_ref[0], wq_ref[...], preferred_element_type=F32).astype(o_ref.dtype)


def _peer_score_weights(wq_t, keys):
    nblk = wq_t.shape[0] // N_KEYS
    return pl.pallas_call(
        _wf_kernel,
        grid=(nblk,),
        in_specs=[pl.BlockSpec((1, N_KEYS, N_KEYS), lambda j: (j % 2, 0, 0)),
                  pl.BlockSpec((N_KEYS, D_MODEL), lambda j: (j, 0))],
        out_specs=pl.BlockSpec((N_KEYS, D_MODEL), lambda j: (j, 0)),
        out_shape=jax.ShapeDtypeStruct(wq_t.shape, BF16),
        compiler_params=_cparams(("parallel",)),
        name="peer_wf",
    )(keys, wq_t)


N_RANK = PEER_TOPK + 1


def _n_cand():
    return [(r, c) for r in range(N_RANK) for c in range(N_RANK) if (r + 1) * (c + 1) <= N_RANK]


def _gelu(z):
    return 0.5 * z * (1.0 + lax.erf(z * (1.0 / math.sqrt(2.0))))


def _extract_top(work_ref, out_ref, n_slab, n_out):
    nh, tm = work_ref.shape[1:]
    unroll = 8 if n_slab % 8 == 0 else 4
    assert n_slab % unroll == 0
    for c in range(tm // LANE):
        ls = slice(c * LANE, (c + 1) * LANE)

        def insert(it, best):
            best = list(best)
            for k in range(unroll):
                w = work_ref[it * unroll + k, :, ls]
                for r in range(n_out):
                    best[r], w = jnp.maximum(best[r], w), jnp.minimum(best[r], w)
            return tuple(best)

        best = lax.fori_loop(0, n_slab // unroll, insert,
                             tuple(jnp.full((nh, LANE), NEG_BIG, F32) for _ in range(n_out)))
        for r in range(n_out):
            out_ref[r, :, ls] = best[r]


def _count_below(s, thr):
    c8 = s > thr[7]
    c4 = s > jnp.where(c8, thr[11], thr[3])
    c2 = s > jnp.where(c8, jnp.where(c4, thr[13], thr[9]), jnp.where(c4, thr[5], thr[1]))
    lo = jnp.where(c4, jnp.where(c2, thr[6], thr[4]), jnp.where(c2, thr[2], thr[0]))
    hi = jnp.where(c4, jnp.where(c2, thr[14], thr[12]), jnp.where(c2, thr[10], thr[8]))
    c1 = s > jnp.where(c8, hi, lo)
    one = lambda cond, w: jnp.where(cond, w, 0.0)
    return one(c8, 8.0) + one(c4, 4.0) + one(c2, 2.0) + one(c1, 1.0) + one(s > thr[15], 1.0)


def _dup_bf16(v):
    u = pltpu.bitcast(v.astype(BF16).astype(F32), jnp.uint32)
    return u | (u >> 16)


def _row_bf16(slab, hh):
    return pltpu.bitcast(jnp.broadcast_to(slab[hh:hh + 1, :], (N_KEYS // 2, LANE)), BF16)


def _peer_kernel(xt_ref, wf_ref, u0_ref, u_ref, vt_ref, x1_ref, g_ref, b_ref, sc_ref, x2_ref,
                 n_scr, e1_scr, rank_scr, e2_scr, s2_scr, y_scr, wt0_scr, wt1_scr, ht0_scr, ht1_scr,
                 work_scr, top_scr, cand_scr, csel_scr, thr_scr, x8_scr, ws_scr, wi_scr):
    g = pl.program_id(1)
    ng = pl.num_programs(1)
    tm = xt_ref.shape[1]
    eb = u_ref.shape[0] // 2
    nh = PEER_HEADS
    rows = N_KEYS * nh

    @pl.when(g == 0)
    def _select():
        y_scr[...] = jnp.zeros_like(y_scr)
        xf32 = xt_ref[...].astype(F32)
        x8_scr[...] = (xf32 * sc_ref[0:1, 0:1]).astype(FP8)
        wt_bound = sc_ref[0:1, 2:3] * jnp.sqrt(jnp.sum(xf32 * xf32, axis=0, keepdims=True))
        s_t = jnp.exp2(jnp.floor(jnp.log2(FP8_TARGET / jnp.maximum(wt_bound, 1e-30)))).astype(BF16).astype(F32)
        ws_scr[...] = _dup_bf16(jnp.broadcast_to(s_t, ws_scr.shape))
        wi_scr[...] = jnp.broadcast_to(sc_ref[0:1, 3:4] / s_t, wi_scr.shape)
        ht0_scr[...] = jnp.dot(u0_ref[...], x8_scr[...], preferred_element_type=F32)
        st = jnp.dot(wf_ref[...], xt_ref[...], preferred_element_type=F32)
        for half in range(2):
            work_scr[...] = st[half * rows:(half + 1) * rows].reshape(N_KEYS, nh, tm)
            _extract_top(work_scr, top_scr.at[half], N_KEYS, N_RANK)
        cands = _n_cand()
        for ci, (r, c) in enumerate(cands):
            cand_scr[ci] = top_scr[0, r] + top_scr[1, c]
        _extract_top(cand_scr, csel_scr, len(cands), N_RANK)
        m0 = csel_scr[0]
        zsum = jnp.zeros_like(m0)
        for r in range(PEER_TOPK):
            zsum = zsum + jnp.exp(csel_scr[r] - m0)
        tau = 0.5 * (csel_scr[PEER_TOPK - 1] + csel_scr[PEER_TOPK])
        a0 = top_scr[0, 0]
        for c in range(PEER_TOPK):
            thr_scr[c] = tau - top_scr[1, c]
        work_scr[...] = st[0:rows].reshape(N_KEYS, nh, tm)

        def key_body(i, carry):
            s = work_scr[i]
            cnt = _count_below(s, [thr_scr[c] for c in range(PEER_TOPK)])
            r0 = pl.multiple_of(i * nh, nh)
            n_scr[pl.ds(r0, nh), :] = _dup_bf16(cnt)
            e1_scr[pl.ds(r0, nh), :] = _dup_bf16(jnp.exp(s - a0) / zsum)
            return carry
        lax.fori_loop(0, N_KEYS, key_body, 0)

        s2_scr[...] = st[2 * rows:3 * rows]
        grp = 16
        for hh in range(nh):
            brow = [top_scr[1, c][hh:hh + 1, :] for c in range(N_RANK)]
            mid = [0.5 * (brow[c] + brow[c + 1]) for c in range(PEER_TOPK)]
            mid_up = mid[::-1]

            def rank_body(jg, carry):
                r0 = pl.multiple_of(hh * N_KEYS + jg * grp, grp)
                s = s2_scr[pl.ds(r0, grp), :]
                cnt = float(PEER_TOPK) - _count_below(s, mid_up)
                rank_scr[pl.ds(r0, grp), :] = cnt.astype(BF16)
                e2_scr[pl.ds(r0, grp), :] = jnp.exp(s - brow[0]).astype(BF16)
                return carry
            lax.fori_loop(0, N_KEYS // grp, rank_body, 0)

    nsub = eb // N_KEYS

    def gate_block(sub, chunks):
        for ii in range(nsub):
            base = pl.multiple_of(((g * 2 + sub) * nsub + ii) * nh, nh)
            for c in chunks:
                ls = slice(c * LANE, (c + 1) * LANE)
                n_i = n_scr[pl.ds(base, nh), ls]
                e_i = e1_scr[pl.ds(base, nh), ls]
                acc = None
                for hh in range(nh):
                    rs = slice(hh * N_KEYS, (hh + 1) * N_KEYS)
                    term = jnp.where(rank_scr[rs, ls] < _row_bf16(n_i, hh), e2_scr[rs, ls],
                                     jnp.zeros((), BF16)) * _row_bf16(e_i, hh)
                    acc = term if acc is None else acc + term
                act = _gelu(ht_scr[sub][ii * N_KEYS:(ii + 1) * N_KEYS, ls] * sc_ref[0:1, 1:2]).astype(BF16)
                wt_scr[sub][ii * N_KEYS:(ii + 1) * N_KEYS, ls] = (act * acc * _row_bf16(ws_scr[:, ls], 0)).astype(FP8)

    ht_scr = (ht0_scr, ht1_scr)
    wt_scr = (wt0_scr, wt1_scr)
    half = tm // 2
    cpl = half // LANE
    for sub in range(2):
        for hf in range(2):
            ln = slice(hf * half, (hf + 1) * half)
            ht_scr[1 - sub][:, ln] = jnp.dot(u_ref[sub * eb:(sub + 1) * eb, :], x8_scr[:, ln],
                                             preferred_element_type=F32)
            gate_block(sub, range(hf * cpl, (hf + 1) * cpl))
            y_scr[:, ln] += jnp.dot(vt_ref[:, sub * eb:(sub + 1) * eb], wt_scr[sub][:, ln],
                                    preferred_element_type=F32)

    @pl.when(g == ng - 1)
    def _finish():
        z = ALPHA * x1_ref[...] + (y_scr[...] * wi_scr[0:1, :]).T
        x2 = _layer_norm(z, g_ref[...], b_ref[...])
        x2_ref[...] = x2


def _peer(x1t, wf, u, v, x1, g, b, x_bound, tm=TOKEN_TILE, eb=PEER_EXPERT_BLOCK):
    T = x1.shape[0]
    ne = u.shape[0]
    ncand = len(_n_cand())
    rows = N_KEYS * PEER_HEADS
    const = lambda t, k: (0, 0)
    slab = lambda n: pltpu.VMEM((n, PEER_HEADS, tm), F32)
    pow2 = lambda bound: jnp.exp2(jnp.floor(jnp.log2(FP8_TARGET / jnp.maximum(bound, 1e-30))))
    s_u = pow2(jnp.max(jnp.abs(u)))
    s_x = pow2(x_bound)
    s_v = pow2(jnp.max(jnp.abs(v)))
    row_norm = jnp.sqrt(jnp.max(jnp.sum(u * u, axis=1)))
    scales = (jnp.zeros((1, LANE), F32).at[0, 0].set(s_x).at[0, 1].set(1.0 / (s_x * s_u))
              .at[0, 2].set(1.15 * PEER_HEADS * row_norm).at[0, 3].set(1.0 / s_v))
    u = (u * s_u).astype(FP8)
    u_roll = jnp.roll(u, -eb, axis=0)
    vt = (v * s_v).astype(FP8).reshape(ne // (2 * eb), 2 * eb, D_MODEL).transpose(0, 2, 1)
    return pl.pallas_call(
        _peer_kernel,
        grid=(T // tm, ne // (2 * eb)),
        in_specs=[pl.BlockSpec((D_MODEL, tm), lambda t, k: (0, t)),
                  pl.BlockSpec(wf.shape, const),
                  pl.BlockSpec((eb, D_MODEL), const),
                  pl.BlockSpec((2 * eb, D_MODEL), lambda t, k: (k, 0)),
                  pl.BlockSpec((None, D_MODEL, 2 * eb), lambda t, k: (k, 0, 0)),
                  pl.BlockSpec((tm, D_MODEL), lambda t, k: (t, 0)),
                  pl.BlockSpec((1, D_MODEL), const),
                  pl.BlockSpec((1, D_MODEL), const),
                  pl.BlockSpec((1, LANE), const)],
        out_specs=pl.BlockSpec((tm, D_MODEL), lambda t, k: (t, 0)),
        out_shape=jax.ShapeDtypeStruct((T, D_MODEL), F32),
        scratch_shapes=[pltpu.VMEM((rows, tm), jnp.uint32), pltpu.VMEM((rows, tm), jnp.uint32),
                        pltpu.VMEM((rows, tm), BF16), pltpu.VMEM((rows, tm), BF16),
                        pltpu.VMEM((rows, tm), F32),
                        pltpu.VMEM((D_MODEL, tm), F32),
                        pltpu.VMEM((eb, tm), FP8), pltpu.VMEM((eb, tm), FP8),
                        pltpu.VMEM((eb, tm), F32), pltpu.VMEM((eb, tm), F32),
                        slab(N_KEYS), pltpu.VMEM((2, N_RANK, PEER_HEADS, tm), F32),
                        slab(ncand), slab(N_RANK), slab(N_RANK), pltpu.VMEM((D_MODEL, tm), FP8),
                        pltpu.VMEM((PEER_HEADS, tm), jnp.uint32), pltpu.VMEM((PEER_HEADS, tm), F32)],
        compiler_params=_cparams(("parallel", "arbitrary")),
        name="peer_ln2",
    )(x1t, wf, u, u_roll, vt, x1, g, b, scales)


def _prep_w_in(w_in):
    sizes = (512, 512, 512, 8, 512, 128, 128, 512, 1024, 4096)
    offs = np.cumsum((0,) + sizes)
    q_f, k_f, v_f, f_l, q_s, k_s, v_s, x_pool, x_conv, gl = (w_in[:, offs[i]:offs[i + 1]] for i in range(10))
    scale = HEAD_DIM ** -0.5
    q_s = q_s.reshape(D_MODEL, 2, 4, HEAD_DIM).transpose(0, 2, 1, 3).reshape(D_MODEL, 512)
    zeros = lambda n: jnp.zeros((D_MODEL, n), w_in.dtype)
    w_h = jnp.concatenate([q_f * scale, k_f, v_f, q_s * scale, x_pool, k_s, v_s, f_l, zeros(120), zeros(128), x_conv],
                          axis=1)
    return w_h.astype(BF16), gl.astype(BF16)


def _prep_peer(wq, k1, k2):
    wq_t = wq.T.astype(BF16)
    keys = jnp.stack([k1, k2]).astype(BF16)
    wf = _peer_score_weights(wq_t, keys)
    wf = wf.reshape(PEER_HEADS, 2, N_KEYS, D_MODEL)
    kh = lambda half: wf[:, half].transpose(1, 0, 2).reshape(N_KEYS * PEER_HEADS, D_MODEL)
    hk = wf[:, 1].reshape(PEER_HEADS * N_KEYS, D_MODEL)
    return jnp.concatenate([kh(0), kh(1), hk], axis=0)


def _pad_lanes(v, n):
    return jnp.zeros((1, n), F32).at[0, :v.shape[0]].set(v.astype(F32))


def kernel(x, w_in, b_f, swa_sinks, pool_w, pool_scale, dw_w, dw_b, conv_ln_g, conv_ln_b, p_fox, p_swa, p_pool,
           p_conv, w_out, ln1_g, ln1_b, peer_wq, peer_k1, peer_k2, peer_u, peer_v, ln2_g, ln2_b):
    B, S, D = x.shape
    T = B * S
    xf = x.reshape(T, D)
    row = lambda v: v.reshape(1, -1).astype(F32)
    for l in range(DEPTH):
        w_h, w_gate = _prep_w_in(w_in[l])
        h = _inproj(xf, w_h)
        aq, ak = _decay(h, _pad_lanes(b_f[l], LANE), B, S)
        y_fox = _fox(h, aq, ak, B, S)
        y_swa = _swa(h, swa_sinks[l].astype(F32), B, S)
        y_pool = _pool(h, pool_w[l].astype(BF16), row(pool_scale[l]), B, S)
        dw = jnp.zeros((32, CONV_W), F32).at[:CONV_K].set(dw_w[l])
        y_conv = _conv(h, dw, row(dw_b[l]), row(conv_ln_g[l]), row(conv_ln_b[l]), B, S)
        ps = p_swa[l].reshape(2, 4, HEAD_DIM, D).transpose(1, 0, 2, 3).reshape(512, D)
        x1, x1t = _merge(xf, y_fox, y_pool, y_conv, y_swa, w_gate, p_fox[l].astype(BF16),
                         p_pool[l].astype(BF16), p_conv[l].astype(BF16), ps.astype(BF16),
                         w_out[l].astype(BF16), row(ln1_g[l]), row(ln1_b[l]))
        wf = _prep_peer(peer_wq[l], peer_k1[l], peer_k2[l])
        x_bound = math.sqrt(D_MODEL) * jnp.max(jnp.abs(ln1_g[l])) + jnp.max(jnp.abs(ln1_b[l]))
        xf = _peer(x1t, wf, peer_u[l], peer_v[l], x1, row(ln2_g[l]), row(ln2_b[l]), x_bound)
    return xf.reshape(B, S, D)
```
